```python
import math
import jax, jax.numpy as jnp
from jax import lax
import numpy as np

D_MODEL = 2048
BATCH = 1
SEQ = 8192
DEPTH = 4

GRID_W = 64
CTX_LEN = 256
N_EVEN = (DEPTH + 1) // 2
N_ODD = DEPTH // 2
EPS = 1e-6
MIX = D_MODEL

A_HEADS = 8
A_DV = MIX // 2 // A_HEADS
A_DK = A_DV // 2
A_CHUNK = 64
FORGET_BIAS = 3.0

B_HEADS = 8
B_KV_HEADS = 2
B_DH = MIX // 2 // B_HEADS
B_WINDOW = 128
ROPE_THETA = 10000.0

C_HEADS = 16
C_DH = MIX // C_HEADS
NA_KH = 8
NA_KW = 16

P_HEADS = 8
P_NKEYS = 128
P_EXPERTS = P_NKEYS * P_NKEYS
P_TOPK = 16
P_DKEY = 256
P_BLOCK = 128

AB_SIZES = (A_HEADS * A_DK, A_HEADS * A_DK, A_HEADS * A_DV, A_HEADS * A_DV, 4 * A_HEADS,
            B_HEADS * B_DH, B_KV_HEADS * B_DH, B_KV_HEADS * B_DH)
AB_IN = sum(AB_SIZES)
AB_SPLITS = tuple(sum(AB_SIZES[:i + 1]) for i in range(len(AB_SIZES) - 1))

F32 = jnp.float32

kernel_name = "hybrid_mlstm_swa_natten_peer_dit"


def rmsnorm(x, g):
    x32 = x.astype(F32)
    y = x32 * lax.rsqrt(jnp.mean(x32 * x32, axis=-1, keepdims=True) + EPS)
    return (y * g.astype(F32)).astype(x.dtype)


def modulate(h, shift, scale):
    return h * (1 + scale) + shift


def rope_1d(x, pos):
    half = x.shape[-1] // 2
    freqs = ROPE_THETA ** (-jnp.arange(half, dtype=F32) / half)
    ang = pos.astype(F32)[:, None] * freqs[None, :]
    cos, sin = jnp.cos(ang)[:, None, :], jnp.sin(ang)[:, None, :]
    x1, x2 = x[..., :half].astype(F32), x[..., half:].astype(F32)
    return jnp.concatenate([x1 * cos - x2 * sin, x2 * cos + x1 * sin], -1).astype(x.dtype)


def rope_axial_2d(x):
    t = jnp.arange(x.shape[1])
    d2 = x.shape[-1] // 2
    return jnp.concatenate([rope_1d(x[..., :d2], t // GRID_W), rope_1d(x[..., d2:], t % GRID_W)], -1)


def mlstm_scan(q, k, v, ig, lf, state):
    b_, h_, t_, _ = q.shape
    nc = t_ // A_CHUNK

    def chunks(a):
        a = a.reshape(a.shape[:2] + (nc, A_CHUNK) + a.shape[3:])
        return jnp.moveaxis(a, 2, 0)

    tril = jnp.tril(jnp.ones((A_CHUNK, A_CHUNK), bool))

    def step(carry, xs):
        C, n, m = carry
        qc, kc, vc, ic, fc = xs
        F = jnp.cumsum(fc, axis=-1)
        Dm = jnp.where(tril, F[..., :, None] - F[..., None, :] + ic[..., None, :], -jnp.inf)
        inter = F + m[..., None]
        mt = jnp.maximum(inter, Dm.max(-1))
        Sw = jnp.einsum('bhtd,bhsd->bhts', qc, kc) * jnp.exp(Dm - mt[..., None])
        a = jnp.exp(inter - mt)
        num = a[..., None] * jnp.einsum('bhed,bhtd->bhte', C, qc) + jnp.einsum('bhts,bhse->bhte', Sw, vc)
        den = a * jnp.einsum('bhd,bhtd->bht', n, qc) + Sw.sum(-1)
        hc = num / jnp.maximum(jnp.abs(den), jnp.exp(-mt))[..., None]
        FL = F[..., -1]
        wend = FL[..., None] - F + ic
        m_new = jnp.maximum(FL + m, wend.max(-1))
        decay = jnp.exp(FL + m - m_new)
        wv = jnp.exp(wend - m_new[..., None])
        C_new = decay[..., None, None] * C + jnp.einsum('bhs,bhse,bhsd->bhed', wv, vc, kc)
        n_new = decay[..., None] * n + jnp.einsum('bhs,bhsd->bhd', wv, kc)
        return (C_new, n_new, m_new), hc

    state, h = lax.scan(step, state, tuple(chunks(a) for a in (q, k, v, ig, lf)))
    h = jnp.moveaxis(h, 0, 2).reshape(b_, h_, t_, A_DV)
    return h, state


def mlstm_bidir(qc, kc, vc, gc, ql, kl, vl, gl):
    b_ = ql.shape[0]
    zero = (jnp.zeros((b_, A_HEADS, A_DV, A_DK), F32), jnp.zeros((b_, A_HEADS, A_DK), F32),
            jnp.zeros((b_, A_HEADS), F32))
    hc_sum, hl_sum = None, None
    for d in range(2):
        flip = (lambda a: jnp.flip(a, axis=2)) if d == 1 else (lambda a: a)
        ig_c, lf_c = gc[..., 2 * d], jax.nn.log_sigmoid(gc[..., 2 * d + 1])
        ig_l, lf_l = gl[..., 2 * d], jax.nn.log_sigmoid(gl[..., 2 * d + 1])
        h_c, st = mlstm_scan(flip(qc), flip(kc), flip(vc), flip(ig_c), flip(lf_c), zero)
        h_l, _ = mlstm_scan(flip(ql), flip(kl), flip(vl), flip(ig_l), flip(lf_l), st)
        h_c, h_l = flip(h_c), flip(h_l)
        hc_sum = h_c if hc_sum is None else hc_sum + h_c
        hl_sum = h_l if hl_sum is None else hl_sum + h_l
    return hc_sum, hl_sum


def window_gqa(qc, kc, vc, ql, kl, vl, sink, with_ctx):
    b_, t_ = ql.shape[:2]
    lc = kc.shape[1]
    g_ = B_HEADS // B_KV_HEADS
    w_ = B_WINDOW
    nb = t_ // w_
    scale = B_DH ** -0.5
    qb = ql.reshape(b_, nb, w_, B_KV_HEADS, g_, B_DH)

    def band(a):
        ap = jnp.pad(a, ((0, 0), (w_, w_), (0, 0), (0, 0))).reshape(b_, nb + 2, w_, B_KV_HEADS, B_DH)
        return jnp.concatenate([ap[:, :-2], ap[:, 1:-1], ap[:, 2:]], axis=2)

    kw, vw = band(kl), band(vl)
    blk = jnp.arange(nb)[:, None, None] * w_
    qpos = blk + jnp.arange(w_)[None, :, None]
    kpos = blk - w_ + jnp.arange(3 * w_)[None, None, :]
    valid = (jnp.abs(kpos - qpos) <= w_) & (kpos >= 0) & (kpos < t_)
    s_loc = jnp.einsum('bnqhgd,bnkhd->bnhgqk', qb, kw).astype(F32) * scale
    s_loc = jnp.where(valid[None, :, None, None], s_loc, -jnp.inf)
    s_ctx = jnp.einsum('bnqhgd,bchd->bnhgqc', qb, kc).astype(F32) * scale
    sk = sink.astype(F32).reshape(B_KV_HEADS, g_)[:, :, None, None]
    s_snk = jnp.broadcast_to(sk, s_loc.shape[:-1] + (1,))
    p = jax.nn.softmax(jnp.concatenate([s_loc, s_ctx, s_snk], -1), axis=-1).astype(vl.dtype)
    out = (jnp.einsum('bnhgqk,bnkhd->bnqhgd', p[..., :3 * w_], vw)
           + jnp.einsum('bnhgqc,bchd->bnqhgd', p[..., 3 * w_:3 * w_ + lc], vc))
    yl = out.reshape(b_, t_, B_HEADS * B_DH)
    if not with_ctx:
        return None, yl
    qcg = qc.reshape(b_, lc, B_KV_HEADS, g_, B_DH)
    s = jnp.einsum('bqhgd,bkhd->bhgqk', qcg, kc).astype(F32) * scale
    s_snk = jnp.broadcast_to(sk, s.shape[:-1] + (1,))
    p = jax.nn.softmax(jnp.concatenate([s, s_snk], -1), axis=-1).astype(vc.dtype)
    yc = jnp.einsum('bhgqk,bkhd->bqhgd', p[..., :lc], vc).reshape(b_, lc, B_HEADS * B_DH)
    return yc, yl


def mixer_ab(hc, hl, w_in, gate_b, mh_g, sink, w_out, with_ctx):
    def project(h):
        b_, t_ = h.shape[:2]
        aq, ak, av, ao, ag, bq, bk, bv = jnp.split(h @ w_in, AB_SPLITS, axis=-1)
        hd = lambda a, n: jnp.moveaxis(a.reshape(b_, t_, n, -1).astype(F32), 2, 1)
        gates = jnp.transpose((ag + gate_b).astype(F32).reshape(b_, t_, 4, A_HEADS), (0, 3, 1, 2))
        return (hd(aq, A_HEADS) * A_DK ** -0.5, hd(ak, A_HEADS), hd(av, A_HEADS), gates, ao,
                bq.reshape(b_, t_, B_HEADS, B_DH), bk.reshape(b_, t_, B_KV_HEADS, B_DH),
                bv.reshape(b_, t_, B_KV_HEADS, B_DH))

    cq, ck, cv, cg, co, cbq, cbk, cbv = project(hc)
    lq, lk, lv, lg, lo, lbq, lbk, lbv = project(hl)
    hA_c, hA_l = mlstm_bidir(cq, ck, cv, cg, lq, lk, lv, lg)

    def a_out(h, o):
        h = jnp.moveaxis(h, 1, 2)
        h = h * lax.rsqrt(jnp.mean(h * h, axis=-1, keepdims=True) + EPS)
        b_, t_ = h.shape[:2]
        return (h.reshape(b_, t_, -1) * mh_g.astype(F32)).astype(o.dtype) * jax.nn.sigmoid(o)

    yB_c, yB_l = window_gqa(cbq, cbk, cbv, rope_axial_2d(lbq), rope_axial_2d(lbk), lbv, sink, with_ctx)
    yl = jnp.concatenate([a_out(hA_l, lo), yB_l], -1) @ w_out
    yc = (jnp.concatenate([a_out(hA_c, co), yB_c], -1) @ w_out) if with_ctx else None
    return yc, yl


def neighborhood_attn(qc, kc, vc, ql, kl, vl, rpb, with_ctx):
    b_, t_ = ql.shape[:2]
    lc = kc.shape[1]
    rows = t_ // GRID_W
    kh = min(NA_KH, rows)
    scale = C_DH ** -0.5
    r = jnp.arange(rows)
    row_idx = jnp.clip(r - kh // 2, 0, rows - kh)[:, None] + jnp.arange(kh)[None, :]
    col = jnp.arange(GRID_W)
    c0 = jnp.clip(col - NA_KW // 2, 0, GRID_W - NA_KW)
    col_ok = (col[None, :] >= c0[:, None]) & (col[None, :] < c0[:, None] + NA_KW)
    qg = ql.reshape(b_, rows, GRID_W, C_HEADS, C_DH)
    kband = kl.reshape(b_, rows, GRID_W, C_HEADS, C_DH)[:, row_idx]
    vband = vl.reshape(b_, rows, GRID_W, C_HEADS, C_DH)[:, row_idx]
    s_loc = jnp.einsum('brqhd,brakhd->brhqak', qg, kband).astype(F32) * scale
    dr = row_idx - r[:, None] + (NA_KH - 1)
    dc = jnp.clip(col[None, :] - col[:, None] + (NA_KW - 1), 0, 2 * NA_KW - 2)
    bias = rpb.astype(F32)[:, dr[:, None, :, None], dc[None, :, None, :]]
    s_loc = jnp.where(col_ok[:, None, :], s_loc + jnp.moveaxis(bias, 0, 1)[None], -jnp.inf)
    s_loc = s_loc.reshape(b_, rows, C_HEADS, GRID_W, kh * GRID_W)
    s_ctx = jnp.einsum('brqhd,bchd->brhqc', qg, kc).astype(F32) * scale
    p = jax.nn.softmax(jnp.concatenate([s_loc, s_ctx], -1), axis=-1).astype(vl.dtype)
    out = (jnp.einsum('brhqk,brkhd->brqhd', p[..., :kh * GRID_W],
                      vband.reshape(b_, rows, kh * GRID_W, C_HEADS, C_DH))
           + jnp.einsum('brhqc,bchd->brqhd', p[..., kh * GRID_W:], vc))
    yl = out.reshape(b_, t_, C_HEADS * C_DH)
    if not with_ctx:
        return None, yl
    s = jnp.einsum('bqhd,bkhd->bhqk', qc, kc).astype(F32) * scale
    p = jax.nn.softmax(s, axis=-1).astype(vc.dtype)
    yc = jnp.einsum('bhqk,bkhd->bqhd', p, vc).reshape(b_, lc, C_HEADS * C_DH)
    return yc, yl


def mixer_c(hc, hl, w_in, rpb, w_out, with_ctx):
    def project(h):
        b_, t_ = h.shape[:2]
        q, k, v = jnp.split(h @ w_in, 3, axis=-1)
        return tuple(a.reshape(b_, t_, C_HEADS, C_DH) for a in (q, k, v))

    cq, ck, cv = project(hc)
    lq, lk, lv = project(hl)
    yc, yl = neighborhood_attn(cq, ck, cv, lq, lk, lv, rpb, with_ctx)
    return ((yc @ w_out) if with_ctx else None), yl @ w_out


def peer(h, wq, keys, u, v):
    b_, n_, d_ = h.shape
    q = (h @ wq).reshape(b_, n_, P_HEADS, 2, P_DKEY // 2)
    s = jnp.einsum('bnhpd,hpkd->bnhpk', q, keys).astype(F32)
    s1, i1 = lax.top_k(s[..., 0, :], P_TOPK)
    s2, i2 = lax.top_k(s[..., 1, :], P_TOPK)
    cand = (s1[..., :, None] + s2[..., None, :]).reshape(b_, n_, P_HEADS, P_TOPK * P_TOPK)
    cidx = (i1[..., :, None] * P_NKEYS + i2[..., None, :]).reshape(b_, n_, P_HEADS, P_TOPK * P_TOPK)
    sc, pos = lax.top_k(cand, P_TOPK)
    idx = jnp.take_along_axis(cidx, pos, axis=-1)
    g = jax.nn.softmax(sc, axis=-1).astype(h.dtype)
    nblk = (b_ * n_) // P_BLOCK
    xb = h.reshape(nblk, P_BLOCK, d_)
    ib = idx.reshape(nblk, P_BLOCK, P_HEADS * P_TOPK)
    gb = g.reshape(nblk, P_BLOCK, P_HEADS * P_TOPK)

    def expert_block(args):
        xk, ik, gk = args
        act = jax.nn.gelu(jnp.einsum('td,tkd->tk', xk, u[ik]), approximate=False)
        return jnp.einsum('tk,tkd->td', gk * act, v[ik])

    return lax.map(expert_block, (xb, ib, gb)).reshape(b_, n_, d_)


def _nrm(k, shape, scale):
    return jax.random.normal(k, shape, F32) * scale


def setup_inputs(seed: int = 0) -> dict:
    key = jax.random.key(seed)
    ks = jax.random.split(key, 21)
    D = D_MODEL
    gate_base = jnp.repeat(jnp.array([0.0, FORGET_BIAS, 0.0, FORGET_BIAS], F32), A_HEADS)
    return {
        "x": _nrm(ks[0], (BATCH, SEQ, D), 1.0),
        "c": _nrm(ks[1], (BATCH, D), 1.0),
        "ctx": _nrm(ks[2], (BATCH, CTX_LEN, D), 1.0),
        "c_ctx": _nrm(ks[3], (D,), 1.0),
        "ada_w": _nrm(ks[4], (DEPTH, D, 6 * D), 0.5 * D ** -0.5),
        "ada_b": _nrm(ks[5], (DEPTH, 6 * D), 0.02),
        "norm1_g": 1.0 + _nrm(ks[6], (DEPTH, D), 0.02),
        "norm2_g": 1.0 + _nrm(ks[7], (DEPTH, D), 0.02),
        "ab_w_in": _nrm(ks[8], (N_EVEN, D, AB_IN), D ** -0.5),
        "ab_gate_b": gate_base + _nrm(ks[9], (N_EVEN, 4 * A_HEADS), 0.1),
        "ab_mh_g": 1.0 + _nrm(ks[10], (N_EVEN, A_HEADS * A_DV), 0.02),
        "ab_sink": _nrm(ks[11], (N_EVEN, B_HEADS), 0.5),
        "ab_w_out": _nrm(ks[12], (N_EVEN, MIX, D), MIX ** -0.5),
        "na_w_in": _nrm(ks[13], (N_ODD, D, 3 * MIX), D ** -0.5),
        "na_rpb": _nrm(ks[14], (N_ODD, C_HEADS, 2 * NA_KH - 1, 2 * NA_KW - 1), 0.1),
        "na_w_out": _nrm(ks[15], (N_ODD, MIX, D), MIX ** -0.5),
        "peer_wq": _nrm(ks[16], (DEPTH, D, P_HEADS * P_DKEY), D ** -0.5),
        "peer_keys": _nrm(ks[17], (DEPTH, P_HEADS, 2, P_NKEYS, P_DKEY // 2), (P_DKEY // 2) ** -0.5),
        "peer_u": _nrm(ks[18], (DEPTH, P_EXPERTS, D), D ** -0.5),
        "peer_v": _nrm(ks[19], (DEPTH, P_EXPERTS, D), P_HEADS ** -0.5),
        "final_g": 1.0 + _nrm(ks[20], (D,), 0.02),
    }


def reference(x, c, ctx, c_ctx, ada_w, ada_b, norm1_g, norm2_g, ab_w_in, ab_gate_b, ab_mh_g, ab_sink,
              ab_w_out, na_w_in, na_rpb, na_w_out, peer_wq, peer_keys, peer_u, peer_v, final_g):
    xl, xc = x, ctx
    lc = ctx.shape[1]
    s_lat = jax.nn.silu(c)
    s_ctx = jax.nn.silu(c_ctx)
    for l in range(DEPTH):
        with_ctx = l < DEPTH - 1
        ml = [m[:, None, :] for m in jnp.split(s_lat @ ada_w[l] + ada_b[l], 6, axis=-1)]
        mc = jnp.split(s_ctx @ ada_w[l] + ada_b[l], 6, axis=-1)
        hl = modulate(rmsnorm(xl, norm1_g[l]), ml[0], ml[1])
        hc = modulate(rmsnorm(xc, norm1_g[l]), mc[0], mc[1])
        if l % 2 == 0:
            e = l // 2
            yc, yl = mixer_ab(hc, hl, ab_w_in[e], ab_gate_b[e], ab_mh_g[e], ab_sink[e], ab_w_out[e], with_ctx)
        else:
            o = l // 2
            yc, yl = mixer_c(hc, hl, na_w_in[o], na_rpb[o], na_w_out[o], with_ctx)
        xl = xl + ml[2] * yl
        hl = modulate(rmsnorm(xl, norm2_g[l]), ml[3], ml[4])
        if with_ctx:
            xc = xc + mc[2] * yc
            hc = modulate(rmsnorm(xc, norm2_g[l]), mc[3], mc[4])
            y = peer(jnp.concatenate([hc, hl], axis=1), peer_wq[l], peer_keys[l], peer_u[l], peer_v[l])
            xc = xc + mc[5] * y[:, :lc]
            xl = xl + ml[5] * y[:, lc:]
        else:
            xl = xl + ml[5] * peer(hl, peer_wq[l], peer_keys[l], peer_u[l], peer_v[l])
    return rmsnorm(xl, final_g)
```

```python
import functools

import numpy as np
import jax
import jax.numpy as jnp
from jax import lax
from jax.experimental import pallas as pl
from jax.experimental.pallas import tpu as pltpu

F32 = jnp.float32
BF16 = jnp.bfloat16

D = 2048
N_LAT = 8192
N_CTX = 256
N_TOK = N_CTX + N_LAT
DEPTH = 4
GRID_W = 64
GRID_H = N_LAT // GRID_W
EPS = 1e-6
LANE = 128
NEG = -1e30

A_HEADS = 8
A_DK = 64
A_DV = 128
A_CHUNK = 64
A_SPAN = 256
B_HEADS = 8
B_KV = 2
B_WINDOW = 128
ROPE_THETA = 10000.0
C_HEADS = 16
NA_KH = 8
NA_KW = 16
NA_ROWS = 4
NA_BAND = NA_ROWS + NA_KH - 1
P_HEADS = 8
P_NKEYS = 128
P_TOPK = 16
P_EXPERTS = P_NKEYS * P_NKEYS

AB_N = 5120
COL_QK, COL_AV, COL_AO, COL_BQ, COL_BK, COL_BV, COL_G = 0, 1024, 2048, 3072, 4096, 4352, 4608

VMEM_LIMIT = 52 * 1024 * 1024


def _cparams(sem):
    return pltpu.CompilerParams(dimension_semantics=sem, vmem_limit_bytes=VMEM_LIMIT)


def _dot(a, b):
    return jnp.dot(a, b, preferred_element_type=F32)


def _dot_nt(a, b):
    return lax.dot_general(a, b, (((1,), (1,)), ((), ())), preferred_element_type=F32)


def _ada_kernel(c_ref, w_ref, b_ref, o_ref):
    c = c_ref[...]
    s = c / (1.0 + jnp.exp(-c))
    w = w_ref[0]
    s_hi = s.astype(BF16)
    s_lo = (s - s_hi.astype(F32)).astype(BF16)
    w_hi = w.astype(BF16)
    w_lo = (w - w_hi.astype(F32)).astype(BF16)
    o_ref[0] = _dot(s_hi, w_hi) + _dot(s_lo, w_hi) + _dot(s_hi, w_lo) + b_ref[0]


def _adaln(cc, ada_w, ada_b):
    tn = 1024
    n = ada_w.shape[-1]
    return pl.pallas_call(
        _ada_kernel,
        grid=(DEPTH, n // tn),
        in_specs=[pl.BlockSpec((16, D), lambda l, j: (0, 0)),
                  pl.BlockSpec((1, D, tn), lambda l, j: (l, 0, j)),
                  pl.BlockSpec((1, 1, tn), lambda l, j: (l, 0, j))],
        out_specs=pl.BlockSpec((1, 16, tn), lambda l, j: (l, 0, j)),
        out_shape=jax.ShapeDtypeStruct((DEPTH, 16, n), F32),
        compiler_params=_cparams(("arbitrary", "arbitrary")),
        name="adaln",
    )(cc, ada_w, ada_b.reshape(DEPTH, 1, n))


def _norm_kernel(*refs, has_resid, n_ctx, tm, row_off):
    if has_resid:
        x_ref, y_ref, g_ref, mv_ref, xo_ref, h_ref = refs
    else:
        x_ref, g_ref, mv_ref, h_ref = refs
    row = (pl.program_id(0) + row_off) * tm + lax.broadcasted_iota(jnp.int32, (tm, 1), 0)
    is_ctx = row < n_ctx
    x = x_ref[...]
    if has_resid:
        gate = jnp.where(is_ctx, mv_ref[5:6, :], mv_ref[4:5, :])
        x = x + gate * y_ref[...]
        xo_ref[...] = x
    ms = jnp.mean(x * x, axis=-1, keepdims=True)
    yn = (x * lax.rsqrt(ms + EPS)) * g_ref[...]
    shift = jnp.where(is_ctx, mv_ref[2:3, :], mv_ref[0:1, :])
    scale = jnp.where(is_ctx, mv_ref[3:4, :], mv_ref[1:2, :])
    h_ref[...] = (yn * (1.0 + scale) + shift).astype(h_ref.dtype)


def _norm(x, g, mv, *, y=None, out_dtype=BF16, row_off=0):
    tm = 256
    n_rows = x.shape[0] - row_off * tm
    blk = pl.BlockSpec((tm, D), lambda i: (i + row_off, 0))
    oblk = pl.BlockSpec((tm, D), lambda i: (i, 0))
    vec = pl.BlockSpec((1, D), lambda i: (0, 0))
    mvs = pl.BlockSpec((8, D), lambda i: (0, 0))
    has_resid = y is not None
    kern = functools.partial(_norm_kernel, has_resid=has_resid, n_ctx=N_CTX, tm=tm, row_off=row_off)
    h_shape = jax.ShapeDtypeStruct((n_rows, D), out_dtype)
    if has_resid:
        return pl.pallas_call(
            kern, grid=(n_rows // tm,),
            in_specs=[blk, blk, vec, mvs], out_specs=[oblk, oblk],
            out_shape=[jax.ShapeDtypeStruct((n_rows, D), F32), h_shape],
            compiler_params=_cparams(("parallel",)), name="resid_norm",
        )(x, y, g.reshape(1, D), mv)
    return pl.pallas_call(
        kern, grid=(n_rows // tm,),
        in_specs=[blk, vec, mvs], out_specs=oblk, out_shape=h_shape,
        compiler_params=_cparams(("parallel",)), name="norm",
    )(x, g.reshape(1, D), mv)


MM_TM, MM_TN = 768, 1024


def _mm_kernel(*refs, has_bias, has_resid, n_ctx, tm):
    a_ref, w_ref = refs[0], refs[1]
    o_ref = refs[-1]
    acc = _dot(a_ref[...], w_ref[...])
    k = 2
    if has_bias:
        acc = acc + refs[k][...]
        k += 1
    if has_resid:
        x_ref, gv_ref = refs[k], refs[k + 1]
        row = pl.program_id(0) * tm + lax.broadcasted_iota(jnp.int32, (tm, 1), 0)
        gate = jnp.where(row < n_ctx, gv_ref[1:2, :], gv_ref[0:1, :])
        acc = x_ref[...] + gate * acc
    o_ref[...] = acc.astype(o_ref.dtype)


def _matmul(a, w, *, bias=None, resid=None, gates=None, out_dtype=F32, tm=MM_TM, tn=MM_TN):
    m, k = a.shape
    n = w.shape[1]
    in_specs = [pl.BlockSpec((tm, k), lambda i, j: (i, 0)), pl.BlockSpec((k, tn), lambda i, j: (0, j))]
    args = [a, w]
    if bias is not None:
        in_specs.append(pl.BlockSpec((1, tn), lambda i, j: (0, j)))
        args.append(bias)
    if resid is not None:
        in_specs += [pl.BlockSpec((tm, tn), lambda i, j: (i, j)), pl.BlockSpec((8, tn), lambda i, j: (0, j))]
        args += [resid, gates]
    kern = functools.partial(_mm_kernel, has_bias=bias is not None, has_resid=resid is not None, n_ctx=N_CTX, tm=tm)
    return pl.pallas_call(
        kern, grid=(m // tm, n // tn), in_specs=in_specs,
        out_specs=pl.BlockSpec((tm, tn), lambda i, j: (i, j)),
        out_shape=jax.ShapeDtypeStruct((m, n), out_dtype),
        compiler_params=_cparams(("parallel", "arbitrary")), name="matmul",
    )(*args)


def _rope_tile(x, cos, sin):
    lane = lax.broadcasted_iota(jnp.int32, x.shape, 1)
    partner = jnp.where((lane % 64) < 32, pltpu.roll(x, 96, axis=1), pltpu.roll(x, 32, axis=1))
    return x * cos + partner * sin


def _rope_kernel(q_ref, k_ref, cos_ref, sin_ref, qo_ref, ko_ref):
    cos, sin = cos_ref[...], sin_ref[...]
    for h in range(B_HEADS):
        sl = slice(h * LANE, (h + 1) * LANE)
        qo_ref[:, sl] = _rope_tile(q_ref[:, sl], cos, sin).astype(qo_ref.dtype)
    for h in range(B_KV):
        sl = slice(h * LANE, (h + 1) * LANE)
        ko_ref[:, sl] = _rope_tile(k_ref[:, sl], cos, sin).astype(ko_ref.dtype)


def _rope_tables():
    t = jnp.arange(N_LAT)
    freqs = ROPE_THETA ** (-jnp.arange(32, dtype=F32) / 32)
    ar = (t // GRID_W).astype(F32)[:, None] * freqs[None, :]
    ac = (t % GRID_W).astype(F32)[:, None] * freqs[None, :]
    cos = jnp.concatenate([jnp.cos(ar), jnp.cos(ar), jnp.cos(ac), jnp.cos(ac)], axis=1)
    sin = jnp.concatenate([-jnp.sin(ar), jnp.sin(ar), -jnp.sin(ac), jnp.sin(ac)], axis=1)
    return cos, sin


def _rope(p, cos, sin):
    tr = 256
    off = N_CTX // tr
    return pl.pallas_call(
        _rope_kernel, grid=(N_LAT // tr,),
        in_specs=[pl.BlockSpec((tr, 1024), lambda i: (i + off, COL_BQ // 1024)),
                  pl.BlockSpec((tr, 256), lambda i: (i + off, COL_BK // 256)),
                  pl.BlockSpec((tr, LANE), lambda i: (i, 0)),
                  pl.BlockSpec((tr, LANE), lambda i: (i, 0))],
        out_specs=[pl.BlockSpec((tr, 1024), lambda i: (i, 0)), pl.BlockSpec((tr, 256), lambda i: (i, 0))],
        out_shape=[jax.ShapeDtypeStruct((N_LAT, 1024), BF16), jax.ShapeDtypeStruct((N_LAT, 256), BF16)],
        compiler_params=_cparams(("parallel",)), name="rope",
    )(p, p, cos, sin)


def _attn_kernel(*refs, g, mq, wk, back, n_keys, k_off, v_off, has_band, has_sink, scale):
    refs = list(refs)
    q_ref = refs.pop(0)
    if has_band:
        k_ref, v_ref = refs.pop(0), refs.pop(0)
    kc_ref, vc_ref = refs.pop(0), refs.pop(0)
    if has_band:
        b_ref = refs.pop(0)
    if has_sink:
        s_ref = refs.pop(0)
    o_ref = refs.pop(0)

    kc = kc_ref[...].astype(BF16)
    vc = vc_ref[...].astype(BF16)
    if has_band:
        i = pl.program_id(1)
        ub = pl.multiple_of(jnp.clip(i * mq - back, 0, n_keys - wk), 64)
        kb = k_ref[pl.ds(k_off + ub, wk), :].astype(BF16)
        vb = v_ref[pl.ds(v_off + ub, wk), :].astype(BF16)
        bias = b_ref[0, 0]
    for hh in range(g):
        sl = slice(hh * LANE, (hh + 1) * LANE)
        q = q_ref[:, sl].astype(BF16)
        s_ctx = _dot_nt(q, kc) * scale
        m = jnp.max(s_ctx, axis=-1, keepdims=True)
        if has_band:
            s_loc = _dot_nt(q, kb) * scale + bias
            m = jnp.maximum(m, jnp.max(s_loc, axis=-1, keepdims=True))
        if has_sink:
            snk = s_ref[:, hh * LANE:hh * LANE + 1]
            m = jnp.maximum(m, snk)
        p_ctx = jnp.exp(s_ctx - m)
        den = jnp.sum(p_ctx, axis=-1, keepdims=True)
        acc = _dot(p_ctx.astype(BF16), vc)
        if has_band:
            p_loc = jnp.exp(s_loc - m)
            den = den + jnp.sum(p_loc, axis=-1, keepdims=True)
            acc = acc + _dot(p_loc.astype(BF16), vb)
        if has_sink:
            den = den + jnp.exp(snk - m)
        o_ref[:, sl] = (acc / den).astype(o_ref.dtype)


def _pattern(i, n):
    return jnp.where(i == 0, 0, jnp.where(i == n - 1, 2, 1))


def _attn_latent(q, q_blk0, q_col0, k, k_col0, k_off, v, v_col0, v_off, kc, kc_col0, vc, vc_col0,
                 bias, sink, *, n_heads, n_kv, mq, wk, back, name):
    g = n_heads // n_kv
    nq = N_LAT // mq
    per_head_bias = bias.shape[0] > 1
    in_specs = [
        pl.BlockSpec((mq, g * LANE), lambda j, i: (i + q_blk0, q_col0 + j)),
        pl.BlockSpec((k.shape[0], LANE), lambda j, i: (0, k_col0 + j)),
        pl.BlockSpec((v.shape[0], LANE), lambda j, i: (0, v_col0 + j)),
        pl.BlockSpec((N_CTX, LANE), lambda j, i: (0, kc_col0 + j)),
        pl.BlockSpec((N_CTX, LANE), lambda j, i: (0, vc_col0 + j)),
        pl.BlockSpec((1, 1, mq, wk), lambda j, i: (j if per_head_bias else 0, _pattern(i, nq), 0, 0)),
    ]
    args = [q, k, v, kc, vc, bias]
    if sink is not None:
        in_specs.append(pl.BlockSpec((1, g * LANE), lambda j, i: (0, j)))
        args.append(sink)
    kern = functools.partial(_attn_kernel, g=g, mq=mq, wk=wk, back=back, n_keys=N_LAT, k_off=k_off, v_off=v_off,
                             has_band=True, has_sink=sink is not None, scale=LANE ** -0.5)
    return pl.pallas_call(
        kern, grid=(n_kv, nq), in_specs=in_specs,
        out_specs=pl.BlockSpec((mq, g * LANE), lambda j, i: (i, j)),
        out_shape=jax.ShapeDtypeStruct((N_LAT, n_heads * LANE), BF16),
        compiler_params=_cparams(("arbitrary", "arbitrary")), name=name,
    )(*args)


def _attn_context(q, q_col0, kc, kc_col0, vc, vc_col0, sink, *, n_heads, n_kv, name):
    g = n_heads // n_kv
    in_specs = [
        pl.BlockSpec((N_CTX, g * LANE), lambda j, i: (0, q_col0 + j)),
        pl.BlockSpec((N_CTX, LANE), lambda j, i: (0, kc_col0 + j)),
        pl.BlockSpec((N_CTX, LANE), lambda j, i: (0, vc_col0 + j)),
    ]
    args = [q, kc, vc]
    if sink is not None:
        in_specs.append(pl.BlockSpec((1, g * LANE), lambda j, i: (0, j)))
        args.append(sink)
    kern = functools.partial(_attn_kernel, g=g, mq=N_CTX, wk=0, back=0, n_keys=0, k_off=0, v_off=0,
                             has_band=False, has_sink=sink is not None, scale=LANE ** -0.5)
    return pl.pallas_call(
        kern, grid=(n_kv, 1), in_specs=in_specs,
        out_specs=pl.BlockSpec((N_CTX, g * LANE), lambda j, i: (0, j)),
        out_shape=jax.ShapeDtypeStruct((N_CTX, n_heads * LANE), BF16),
        compiler_params=_cparams(("arbitrary", "arbitrary")), name=name,
    )(*args)


def _window_bias():
    t, w = N_LAT, B_WINDOW
    nb = t // w
    out = []
    for bi in (0, 1, nb - 1):
        ub = min(max(bi * w - w, 0), t - 3 * w)
        qpos = bi * w + np.arange(w)[:, None]
        kpos = ub + np.arange(3 * w)[None, :]
        out.append(np.where(np.abs(kpos - qpos) <= w, 0.0, NEG))
    return jnp.asarray(np.stack(out)[None], F32)


def _na_bias(rpb):
    col = np.arange(GRID_W)
    c0 = np.clip(col - NA_KW // 2, 0, GRID_W - NA_KW)
    col_ok = (col[None, :] >= c0[:, None]) & (col[None, :] < c0[:, None] + NA_KW)
    dc = np.clip(col[None, :] - col[:, None] + (NA_KW - 1), 0, 2 * NA_KW - 2)
    n_groups = GRID_H // NA_ROWS
    out = []
    for gi in (0, 1, n_groups - 1):
        r = gi * NA_ROWS + np.arange(NA_ROWS)
        ub = min(max(gi * NA_ROWS - NA_KH // 2, 0), GRID_H - NA_BAND)
        rs = np.clip(r - NA_KH // 2, 0, GRID_H - NA_KH)
        krow = ub + np.arange(NA_BAND)
        row_ok = (krow[None, :] >= rs[:, None]) & (krow[None, :] < rs[:, None] + NA_KH)
        dr = np.clip(krow[None, :] - r[:, None] + (NA_KH - 1), 0, 2 * NA_KH - 2)
        b = rpb.astype(F32)[:, dr[:, None, :, None], dc[None, :, None, :]]
        ok = row_ok[:, None, :, None] & col_ok[None, :, None, :]
        b = jnp.where(ok[None], b, NEG)
        out.append(b.reshape(C_HEADS, NA_ROWS * GRID_W, NA_BAND * GRID_W))
    return jnp.stack(out, axis=1)


def _log_sigmoid(x):
    return jnp.minimum(x, 0.0) - jnp.log1p(jnp.exp(-jnp.abs(x)))


def _mlstm_kernel(qk_ref, v_ref, g_ref, kt_ref, gt_ref, h_ref, c_ref, m_ref, *, d):
    L = A_CHUNK

    @pl.when(pl.program_id(0) == 0)
    def _():
        c_ref[...] = jnp.zeros_like(c_ref)
        m_ref[...] = jnp.zeros_like(m_ref)

    ri = lax.broadcasted_iota(jnp.int32, (L, L), 0)
    ci = lax.broadcasted_iota(jnp.int32, (L, L), 1)
    seen = (ci <= ri) if d == 0 else (ci >= ri)
    seen_t = (ri <= ci) if d == 0 else (ri >= ci)
    ones_col = (lax.broadcasted_iota(jnp.int32, (L, LANE), 1) == 0).astype(F32)
    i_lane, f_lane = 2 * d * A_HEADS, (2 * d + 1) * A_HEADS

    def chunk(cidx, carry):
        c = cidx if d == 0 else A_SPAN // L - 1 - cidx
        r0 = pl.multiple_of(c * L, L)
        gc = g_ref[pl.ds(r0, L), :]
        gt = gt_ref[c]
        for h in range(A_HEADS):
            sl = slice(h * LANE, (h + 1) * LANE)
            q = (qk_ref[pl.ds(r0, L), sl][:, :A_DK] * (A_DK ** -0.5)).astype(BF16)
            kt = kt_ref[c, h * A_DK:(h + 1) * A_DK, :]
            v_aug = jnp.concatenate([v_ref[pl.ds(r0, L), sl], ones_col], axis=1).astype(BF16)
            i_row = gt[i_lane + h:i_lane + h + 1, :]
            f_row = _log_sigmoid(gt[f_lane + h:f_lane + h + 1, :])
            f_col = _log_sigmoid(gc[:, f_lane + h:f_lane + h + 1])
            cum_col = jnp.sum(jnp.where(seen, f_row, 0.0), axis=1, keepdims=True)
            cum_row = jnp.sum(jnp.where(seen_t, f_col, 0.0), axis=0, keepdims=True)
            total = jnp.sum(f_row, axis=1, keepdims=True)
            m_old = m_ref[h:h + 1, 0:1]
            dm = jnp.where(seen, cum_col - cum_row + i_row, NEG)
            inter = cum_col + m_old
            mt = jnp.maximum(inter, jnp.max(dm, axis=1, keepdims=True))
            sw = _dot(q, kt.astype(BF16)) * jnp.exp(dm - mt)
            a = jnp.exp(inter - mt)
            c_old = c_ref[h]
            na = _dot(sw.astype(BF16), v_aug) + a * _dot(q, c_old.astype(BF16))
            den = jnp.maximum(jnp.abs(na[:, LANE:LANE + 1]), jnp.exp(-mt))
            h_ref[pl.ds(r0, L), sl] = na[:, :LANE] / den
            wend = total - cum_row + i_row
            m_new = jnp.maximum(total + m_old, jnp.max(wend, axis=1, keepdims=True))
            decay = jnp.exp(total + m_old - m_new)
            wv = jnp.exp(wend - m_new)
            c_ref[h] = decay * c_old + _dot((kt * wv).astype(BF16), v_aug)
            m_ref[h:h + 1, :] = jnp.broadcast_to(m_new, (1, LANE))
        return carry

    lax.fori_loop(0, A_SPAN // L, chunk, 0)


def _mlstm(p, kt3, gt3, d):
    n_span = N_TOK // A_SPAN
    cps = A_SPAN // A_CHUNK
    if d == 0:
        span = lambda s: s
    else:
        span = lambda s: jnp.where(s == 0, 0, n_span - s)
    return pl.pallas_call(
        functools.partial(_mlstm_kernel, d=d), grid=(n_span,),
        in_specs=[pl.BlockSpec((A_SPAN, 1024), lambda s: (span(s), COL_QK // 1024)),
                  pl.BlockSpec((A_SPAN, 1024), lambda s: (span(s), COL_AV // 1024)),
                  pl.BlockSpec((A_SPAN, LANE), lambda s: (span(s), COL_G // LANE)),
                  pl.BlockSpec((cps, A_HEADS * A_DK, A_CHUNK), lambda s: (span(s), 0, 0)),
                  pl.BlockSpec((cps, 4 * A_HEADS, A_CHUNK), lambda s: (span(s), 0, 0))],
        out_specs=pl.BlockSpec((A_SPAN, 1024), lambda s: (span(s), 0)),
        out_shape=jax.ShapeDtypeStruct((N_TOK, 1024), F32),
        scratch_shapes=[pltpu.VMEM((A_HEADS, A_DK, 2 * LANE), F32), pltpu.VMEM((A_HEADS, LANE), F32)],
        compiler_params=_cparams(("arbitrary",)), name="mlstm_fwd" if d == 0 else "mlstm_bwd",
    )(p, p, p, kt3, gt3)


def _aout_kernel(hf_ref, hb_ref, o_ref, g_ref, y_ref):
    for h in range(A_HEADS):
        sl = slice(h * LANE, (h + 1) * LANE)
        x = hf_ref[:, sl] + hb_ref[:, sl]
        x = x * lax.rsqrt(jnp.mean(x * x, axis=-1, keepdims=True) + EPS)
        o = o_ref[:, sl]
        y_ref[:, sl] = ((x * g_ref[:, sl]) * (1.0 / (1.0 + jnp.exp(-o)))).astype(y_ref.dtype)


def _aout(hf, hb, p, mh_g):
    tm = 256
    blk = pl.BlockSpec((tm, 1024), lambda i: (i, 0))
    return pl.pallas_call(
        _aout_kernel, grid=(N_TOK // tm,),
        in_specs=[blk, blk, pl.BlockSpec((tm, 1024), lambda i: (i, COL_AO // 1024)),
                  pl.BlockSpec((1, 1024), lambda i: (0, 0))],
        out_specs=blk, out_shape=jax.ShapeDtypeStruct((N_TOK, 1024), BF16),
        compiler_params=_cparams(("parallel",)), name="mlstm_out",
    )(hf, hb, p, mh_g.reshape(1, 1024))


def _top_values(s, k):
    vals = []
    cur = s
    for _ in range(k):
        mx = jnp.max(cur, axis=0, keepdims=True)
        vals.append(mx)
        cur = jnp.where(cur == mx, NEG, cur)
    return vals


def _router_kernel(ht_ref, wqt_ref, keys_ref, s_ref, e_ref, thr_ref, *, tn):
    ht = ht_ref[...]
    row8 = lax.broadcasted_iota(jnp.int32, (8, tn), 0)
    row16 = lax.broadcasted_iota(jnp.int32, (16, tn), 0)
    for h in range(P_HEADS):
        tops = []
        for p in range(2):
            hp = 2 * h + p
            q = _dot(wqt_ref[hp * LANE:(hp + 1) * LANE, :], ht).astype(BF16)
            s = _dot(keys_ref[hp], q)
            s_ref[hp] = s
            tops.append(_top_values(s, P_TOPK))
        ta, tb = tops
        a_all = jnp.full((16, tn), NEG, F32)
        b_hi = jnp.full((8, tn), NEG, F32)
        for i in range(16):
            a_all = jnp.where(row16 == i, ta[i], a_all)
        for j in range(8, 16):
            b_hi = jnp.where(row8 == j - 8, tb[j], b_hi)
        a_lo = a_all[:8]
        parts = [a_all + tb[0], a_lo + tb[1]]
        for j in range(2, 8):
            parts.append(jnp.where(row8 < P_TOPK // (j + 1), a_lo + tb[j], NEG))
        parts.append(ta[0] + b_hi)
        cand = jnp.concatenate(parts, axis=0)
        best = _top_values(cand, P_TOPK)
        z = jnp.zeros_like(best[0])
        for c in best:
            z = z + jnp.exp(c - best[0])
        thr_ref[h:h + 1, :] = best[P_TOPK - 1]
        e_ref[2 * h] = jnp.exp(s_ref[2 * h] - ta[0]) / z
        e_ref[2 * h + 1] = jnp.exp(s_ref[2 * h + 1] - tb[0])


def _router(ht, wqt, keys):
    tn = 256
    t = ht.shape[1]
    big = pl.BlockSpec((2 * P_HEADS, P_NKEYS, tn), lambda i: (0, 0, i))
    shp = jax.ShapeDtypeStruct((2 * P_HEADS, P_NKEYS, t), F32)
    return pl.pallas_call(
        functools.partial(_router_kernel, tn=tn), grid=(t // tn,),
        in_specs=[pl.BlockSpec((D, tn), lambda i: (0, i)),
                  pl.BlockSpec((D, D), lambda i: (0, 0)),
                  pl.BlockSpec((2 * P_HEADS, P_NKEYS, LANE), lambda i: (0, 0, 0))],
        out_specs=[big, big, pl.BlockSpec((P_HEADS, tn), lambda i: (0, i))],
        out_shape=[shp, shp, jax.ShapeDtypeStruct((P_HEADS, t), F32)],
        compiler_params=_cparams(("parallel",)), name="peer_router",
    )(ht, wqt, keys)


P_EC = 1024
P_TN = 768


def _gelu(x):
    return 0.5 * x * (1.0 + lax.erf(x * (2.0 ** -0.5)))


def _expert_kernel(ht_ref, u_ref, vt_ref, s1_ref, s2_ref, e1_ref, e2_ref, thr_ref, y_ref, z_scr, w_scr):
    j = pl.program_id(1)

    @pl.when(j == 0)
    def _():
        y_ref[...] = jnp.zeros_like(y_ref)

    z_scr[...] = _dot(u_ref[...], ht_ref[...])

    def a_row(ai, carry):
        r0 = pl.multiple_of(ai * P_NKEYS, P_NKEYS)
        to_top = (8 - ai) % 8
        for lb in range(P_TN // LANE):
            sl = slice(lb * LANE, (lb + 1) * LANE)
            gate = jnp.zeros((P_NKEYS, LANE), F32)
            for h in range(P_HEADS):
                s1 = pltpu.roll(s1_ref[h, 0, :, sl], to_top, axis=0)[0:1, :]
                e1 = pltpu.roll(e1_ref[h, 0, :, sl], to_top, axis=0)[0:1, :]
                pair = s1 + s2_ref[h, 0, :, sl]
                w = e1 * e2_ref[h, 0, :, sl]
                gate = gate + jnp.where(pair >= thr_ref[h:h + 1, sl], w, 0.0)
            w_scr[pl.ds(r0, P_NKEYS), sl] = (gate * _gelu(z_scr[pl.ds(r0, P_NKEYS), sl])).astype(BF16)
        return carry

    lax.fori_loop(0, P_EC // P_NKEYS, a_row, 0)
    y_ref[...] += _dot(vt_ref[...], w_scr[...])


def _experts(ht, u, vt, s, e, thr):
    t = ht.shape[1]
    ac = P_EC // P_NKEYS
    s4 = s.reshape(P_HEADS, 2, P_NKEYS, t)
    e4 = e.reshape(P_HEADS, 2, P_NKEYS, t)
    once = pl.Buffered(1)
    first = pl.BlockSpec((P_HEADS, 1, ac, P_TN), lambda i, j: (0, 0, j, i))
    second = pl.BlockSpec((P_HEADS, 1, P_NKEYS, P_TN), lambda i, j: (0, 1, 0, i), pipeline_mode=once)
    return pl.pallas_call(
        _expert_kernel, grid=(t // P_TN, P_EXPERTS // P_EC),
        in_specs=[pl.BlockSpec((D, P_TN), lambda i, j: (0, i), pipeline_mode=once),
                  pl.BlockSpec((P_EC, D), lambda i, j: (j, 0)),
                  pl.BlockSpec((D, P_EC), lambda i, j: (0, j)),
                  first, second, first, second,
                  pl.BlockSpec((P_HEADS, P_TN), lambda i, j: (0, i), pipeline_mode=once)],
        out_specs=pl.BlockSpec((D, P_TN), lambda i, j: (0, i)),
        out_shape=jax.ShapeDtypeStruct((D, t), F32),
        scratch_shapes=[pltpu.VMEM((P_EC, P_TN), F32), pltpu.VMEM((P_EC, P_TN), BF16)],
        compiler_params=_cparams(("parallel", "arbitrary")), name="peer_experts",
    )(ht, u, vt, s4, s4, e4, e4, thr)


def _peer(h, wq, keys, u, v):
    ht = h.T
    s, e, thr = _router(ht, wq.T.astype(BF16), keys.reshape(2 * P_HEADS, P_NKEYS, LANE).astype(BF16))
    yt = _experts(ht, u.astype(BF16), v.T.astype(BF16), s, e, thr)
    return yt.T


def _ab_weights(w_in, gate_b):
    aq, ak, av, ao, ag, bq, bk, bv = jnp.split(w_in, np.cumsum([512, 512, 1024, 1024, 32, 1024, 256])[:].tolist(), axis=1)
    qk = jnp.concatenate([aq.reshape(D, A_HEADS, A_DK), ak.reshape(D, A_HEADS, A_DK)], axis=2).reshape(D, 1024)
    pad = jnp.zeros((D, AB_N - COL_G - 32), w_in.dtype)
    w = jnp.concatenate([qk, av, ao, bq, bk, bv, ag, pad], axis=1).astype(BF16)
    bias = jnp.zeros((1, AB_N), F32).at[0, COL_G:COL_G + 32].set(gate_b.astype(F32))
    return w, bias


def _mixer_ab(h, w_in, gate_b, mh_g, sink, rope_tabs, win_bias):
    w, bias = _ab_weights(w_in, gate_b)
    p = _matmul(h, w, bias=bias)
    n_chunk = N_TOK // A_CHUNK
    k_only = p[:, COL_QK:COL_QK + 1024].reshape(n_chunk, A_CHUNK, A_HEADS, 2, A_DK)[:, :, :, 1, :]
    kt3 = k_only.reshape(n_chunk, A_CHUNK, A_HEADS * A_DK).transpose(0, 2, 1)
    gt3 = p[:, COL_G:COL_G + 4 * A_HEADS].reshape(n_chunk, A_CHUNK, 4 * A_HEADS).transpose(0, 2, 1)
    hf = _mlstm(p, kt3, gt3, 0)
    hb = _mlstm(p, kt3, gt3, 1)
    ya = _aout(hf, hb, p, mh_g)
    qr, kr = _rope(p, *rope_tabs)
    sink_b = jnp.repeat(sink.astype(F32), LANE).reshape(1, B_HEADS * LANE)
    yb_l = _attn_latent(qr, 0, 0, kr, 0, 0, p, COL_BV // LANE, N_CTX, p, COL_BK // LANE, p, COL_BV // LANE,
                        win_bias, sink_b, n_heads=B_HEADS, n_kv=B_KV, mq=B_WINDOW, wk=3 * B_WINDOW, back=B_WINDOW,
                        name="window_attn")
    yb_c = _attn_context(p, COL_BQ // (4 * LANE), p, COL_BK // LANE, p, COL_BV // LANE, sink_b,
                         n_heads=B_HEADS, n_kv=B_KV, name="window_attn_ctx")
    return jnp.concatenate([ya, jnp.concatenate([yb_c, yb_l], axis=0)], axis=1)


def _mixer_c(h, w_in, rpb):
    p = _matmul(h, w_in.astype(BF16), out_dtype=BF16)
    bias = _na_bias(rpb)
    mq = NA_ROWS * GRID_W
    y_l = _attn_latent(p, N_CTX // mq, 0, p, C_HEADS, N_CTX, p, 2 * C_HEADS, N_CTX, p, C_HEADS, p, 2 * C_HEADS,
                       bias, None, n_heads=C_HEADS, n_kv=C_HEADS, mq=mq, wk=NA_BAND * GRID_W,
                       back=(NA_KH // 2) * GRID_W, name="na_attn")
    y_c = _attn_context(p, 0, p, C_HEADS, p, 2 * C_HEADS, None, n_heads=C_HEADS, n_kv=C_HEADS, name="na_attn_ctx")
    return jnp.concatenate([y_c, y_l], axis=0)


def _mod_rows(m6, i_shift, i_scale, i_gate):
    z = jnp.zeros((D,), F32)
    pick = lambda r, i: m6[r, i] if i is not None else z
    return jnp.stack([pick(0, i_shift), pick(0, i_scale), pick(1, i_shift), pick(1, i_scale),
                      pick(0, i_gate), pick(1, i_gate), z, z])


def kernel(x, c, ctx, c_ctx, ada_w, ada_b, norm1_g, norm2_g, ab_w_in, ab_gate_b, ab_mh_g, ab_sink, ab_w_out,
           na_w_in, na_rpb, na_w_out, peer_wq, peer_keys, peer_u, peer_v, final_g):
    xs = jnp.concatenate([ctx[0], x[0]], axis=0).astype(F32)
    cc = jnp.zeros((16, D), F32).at[0].set(c[0]).at[1].set(c_ctx)
    mods = _adaln(cc, ada_w, ada_b)[:, :2].reshape(DEPTH, 2, 6, D)
    rope_tabs = _rope_tables()
    win_bias = _window_bias()

    h = _norm(xs, norm1_g[0], _mod_rows(mods[0], 0, 1, None))
    for l in range(DEPTH):
        m6 = mods[l]
        if l % 2 == 0:
            e = l // 2
            ymix = _mixer_ab(h, ab_w_in[e], ab_gate_b[e], ab_mh_g[e], ab_sink[e], rope_tabs, win_bias)
            w_out = ab_w_out[e]
        else:
            o = l // 2
            ymix = _mixer_c(h, na_w_in[o], na_rpb[o])
            w_out = na_w_out[o]
        gv = jnp.concatenate([m6[:, 2], jnp.zeros((6, D), F32)], axis=0)
        xs = _matmul(ymix, w_out.astype(BF16), resid=xs, gates=gv)
        h2 = _norm(xs, norm2_g[l], _mod_rows(m6, 3, 4, None))
        y = _peer(h2, peer_wq[l], peer_keys[l], peer_u[l], peer_v[l])
        if l + 1 < DEPTH:
            mv = _mod_rows(mods[l + 1], 0, 1, None).at[4].set(m6[0, 5]).at[5].set(m6[1, 5])
            xs, h = _norm(xs, norm1_g[l + 1], mv, y=y)
        else:
            mv = _mod_rows(m6, None, None, 5)
            _, out = _norm(xs, final_g, mv, y=y, out_dtype=F32, row_off=N_CTX // 256)
    return out[None]
```

```python
import functools

import numpy as np
import jax
import jax.numpy as jnp
from jax import lax
from jax.experimental import pallas as pl
from jax.experimental.pallas import tpu as pltpu

F32 = jnp.float32
BF16 = jnp.bfloat16

D = 2048
N_LAT = 8192
N_CTX = 256
N_TOK = N_CTX + N_LAT
DEPTH = 4
GRID_W = 64
GRID_H = N_LAT // GRID_W
EPS = 1e-6
LANE = 128
NEG = -1e30

A_HEADS = 8
A_DK = 64
A_DV = 128
A_CHUNK = 64
A_SPAN = 256
B_HEADS = 8
B_KV = 2
B_WINDOW = 128
ROPE_THETA = 10000.0
C_HEADS = 16
NA_KH = 8
NA_KW = 16
NA_ROWS = 4
NA_BAND = NA_ROWS + NA_KH - 1
P_HEADS = 8
P_NKEYS = 128
P_TOPK = 16
P_EXPERTS = P_NKEYS * P_NKEYS

AB_N = 5120
COL_QK, COL_AV, COL_AO, COL_BQ, COL_BK, COL_BV, COL_G = 0, 1024, 2048, 3072, 4096, 4352, 4608

VMEM_LIMIT = 52 * 1024 * 1024


def _cparams(sem):
    return pltpu.CompilerParams(dimension_semantics=sem, vmem_limit_bytes=VMEM_LIMIT)


def _dot(a, b):
    return jnp.dot(a, b, preferred_element_type=F32)


def _dot_nt(a, b):
    return lax.dot_general(a, b, (((1,), (1,)), ((), ())), preferred_element_type=F32)


def _ada_kernel(c_ref, w_ref, b_ref, o_ref):
    c = c_ref[...]
    s = c / (1.0 + jnp.exp(-c))
    w = w_ref[0]
    s_hi = s.astype(BF16)
    s_lo = (s - s_hi.astype(F32)).astype(BF16)
    w_hi = w.astype(BF16)
    w_lo = (w - w_hi.astype(F32)).astype(BF16)
    o_ref[0] = _dot(s_hi, w_hi) + _dot(s_lo, w_hi) + _dot(s_hi, w_lo) + b_ref[0]


def _adaln(cc, ada_w, ada_b):
    tn = 1024
    n = ada_w.shape[-1]
    return pl.pallas_call(
        _ada_kernel,
        grid=(DEPTH, n // tn),
        in_specs=[pl.BlockSpec((16, D), lambda l, j: (0, 0)),
                  pl.BlockSpec((1, D, tn), lambda l, j: (l, 0, j)),
                  pl.BlockSpec((1, 1, tn), lambda l, j: (l, 0, j))],
        out_specs=pl.BlockSpec((1, 16, tn), lambda l, j: (l, 0, j)),
        out_shape=jax.ShapeDtypeStruct((DEPTH, 16, n), F32),
        compiler_params=_cparams(("arbitrary", "arbitrary")),
        name="adaln",
    )(cc, ada_w, ada_b.reshape(DEPTH, 1, n))


def _norm_kernel(*refs, has_resid, n_ctx, tm, row_off):
    if has_resid:
        x_ref, y_ref, g_ref, mv_ref, xo_ref, h_ref = refs
    else:
        x_ref, g_ref, mv_ref, h_ref = refs
    row = (pl.program_id(0) + row_off) * tm + lax.broadcasted_iota(jnp.int32, (tm, 1), 0)
    is_ctx = row < n_ctx
    x = x_ref[...]
    if has_resid:
        gate = jnp.where(is_ctx, mv_ref[5:6, :], mv_ref[4:5, :])
        x = x + gate * y_ref[...]
        xo_ref[...] = x
    ms = jnp.mean(x * x, axis=-1, keepdims=True)
    yn = (x * lax.rsqrt(ms + EPS)) * g_ref[...]
    shift = jnp.where(is_ctx, mv_ref[2:3, :], mv_ref[0:1, :])
    scale = jnp.where(is_ctx, mv_ref[3:4, :], mv_ref[1:2, :])
    h_ref[...] = (yn * (1.0 + scale) + shift).astype(h_ref.dtype)


def _norm(x, g, mv, *, y=None, out_dtype=BF16, row_off=0):
    tm = 256
    n_rows = x.shape[0] - row_off * tm
    blk = pl.BlockSpec((tm, D), lambda i: (i + row_off, 0))
    oblk = pl.BlockSpec((tm, D), lambda i: (i, 0))
    vec = pl.BlockSpec((1, D), lambda i: (0, 0))
    mvs = pl.BlockSpec((8, D), lambda i: (0, 0))
    has_resid = y is not None
    kern = functools.partial(_norm_kernel, has_resid=has_resid, n_ctx=N_CTX, tm=tm, row_off=row_off)
    h_shape = jax.ShapeDtypeStruct((n_rows, D), out_dtype)
    if has_resid:
        return pl.pallas_call(
            kern, grid=(n_rows // tm,),
            in_specs=[blk, blk, vec, mvs], out_specs=[oblk, oblk],
            out_shape=[jax.ShapeDtypeStruct((n_rows, D), F32), h_shape],
            compiler_params=_cparams(("parallel",)), name="resid_norm",
        )(x, y, g.reshape(1, D), mv)
    return pl.pallas_call(
        kern, grid=(n_rows // tm,),
        in_specs=[blk, vec, mvs], out_specs=oblk, out_shape=h_shape,
        compiler_params=_cparams(("parallel",)), name="norm",
    )(x, g.reshape(1, D), mv)


MM_TM, MM_TN = 768, 1024


def _mm_kernel(*refs, has_bias, has_resid, n_ctx, tm):
    a_ref, w_ref = refs[0], refs[1]
    o_ref = refs[-1]
    acc = _dot(a_ref[...], w_ref[...])
    k = 2
    if has_bias:
        acc = acc + refs[k][...]
        k += 1
    if has_resid:
        x_ref, gv_ref = refs[k], refs[k + 1]
        row = pl.program_id(0) * tm + lax.broadcasted_iota(jnp.int32, (tm, 1), 0)
        gate = jnp.where(row < n_ctx, gv_ref[1:2, :], gv_ref[0:1, :])
        acc = x_ref[...] + gate * acc
    o_ref[...] = acc.astype(o_ref.dtype)


def _matmul(a, w, *, bias=None, resid=None, gates=None, out_dtype=F32, tm=MM_TM, tn=MM_TN):
    m, k = a.shape
    n = w.shape[1]
    in_specs = [pl.BlockSpec((tm, k), lambda i, j: (i, 0)), pl.BlockSpec((k, tn), lambda i, j: (0, j))]
    args = [a, w]
    if bias is not None:
        in_specs.append(pl.BlockSpec((1, tn), lambda i, j: (0, j)))
        args.append(bias)
    if resid is not None:
        in_specs += [pl.BlockSpec((tm, tn), lambda i, j: (i, j)), pl.BlockSpec((8, tn), lambda i, j: (0, j))]
        args += [resid, gates]
    kern = functools.partial(_mm_kernel, has_bias=bias is not None, has_resid=resid is not None, n_ctx=N_CTX, tm=tm)
    return pl.pallas_call(
        kern, grid=(m // tm, n // tn), in_specs=in_specs,
        out_specs=pl.BlockSpec((tm, tn), lambda i, j: (i, j)),
        out_shape=jax.ShapeDtypeStruct((m, n), out_dtype),
        compiler_params=_cparams(("parallel", "arbitrary")), name="matmul",
    )(*args)


def _rope_tile(x, cos, sin):
    lane = lax.broadcasted_iota(jnp.int32, x.shape, 1)
    partner = jnp.where((lane % 64) < 32, pltpu.roll(x, 96, axis=1), pltpu.roll(x, 32, axis=1))
    return x * cos + partner * sin


def _rope_kernel(q_ref, k_ref, cos_ref, sin_ref, qo_ref, ko_ref):
    cos, sin = cos_ref[...], sin_ref[...]
    for h in range(B_HEADS):
        sl = slice(h * LANE, (h + 1) * LANE)
        qo_ref[:, sl] = _rope_tile(q_ref[:, sl], cos, sin).astype(qo_ref.dtype)
    for h in range(B_KV):
        sl = slice(h * LANE, (h + 1) * LANE)
        ko_ref[:, sl] = _rope_tile(k_ref[:, sl], cos, sin).astype(ko_ref.dtype)


def _rope_tables():
    t = jnp.arange(N_LAT)
    freqs = ROPE_THETA ** (-jnp.arange(32, dtype=F32) / 32)
    ar = (t // GRID_W).astype(F32)[:, None] * freqs[None, :]
    ac = (t % GRID_W).astype(F32)[:, None] * freqs[None, :]
    cos = jnp.concatenate([jnp.cos(ar), jnp.cos(ar), jnp.cos(ac), jnp.cos(ac)], axis=1)
    sin = jnp.concatenate([-jnp.sin(ar), jnp.sin(ar), -jnp.sin(ac), jnp.sin(ac)], axis=1)
    return cos, sin


def _rope(p, cos, sin):
    tr = 256
    off = N_CTX // tr
    return pl.pallas_call(
        _rope_kernel, grid=(N_LAT // tr,),
        in_specs=[pl.BlockSpec((tr, 1024), lambda i: (i + off, COL_BQ // 1024)),
                  pl.BlockSpec((tr, 256), lambda i: (i + off, COL_BK // 256)),
                  pl.BlockSpec((tr, LANE), lambda i: (i, 0)),
                  pl.BlockSpec((tr, LANE), lambda i: (i, 0))],
        out_specs=[pl.BlockSpec((tr, 1024), lambda i: (i, 0)), pl.BlockSpec((tr, 256), lambda i: (i, 0))],
        out_shape=[jax.ShapeDtypeStruct((N_LAT, 1024), BF16), jax.ShapeDtypeStruct((N_LAT, 256), BF16)],
        compiler_params=_cparams(("parallel",)), name="rope",
    )(p, p, cos, sin)


def _attn_kernel(*refs, g, mq, wk, back, n_keys, k_off, v_off, has_band, has_sink, scale):
    refs = list(refs)
    q_ref = refs.pop(0)
    if has_band:
        k_ref, v_ref = refs.pop(0), refs.pop(0)
    kc_ref, vc_ref = refs.pop(0), refs.pop(0)
    if has_band:
        b_ref = refs.pop(0)
    if has_sink:
        s_ref = refs.pop(0)
    o_ref = refs.pop(0)

    kc = kc_ref[...].astype(BF16)
    vc = vc_ref[...].astype(BF16)
    if has_band:
        i = pl.program_id(1)
        ub = pl.multiple_of(jnp.clip(i * mq - back, 0, n_keys - wk), 64)
        kb = k_ref[pl.ds(k_off + ub, wk), :].astype(BF16)
        vb = v_ref[pl.ds(v_off + ub, wk), :].astype(BF16)
        bias = b_ref[0, 0]
    for hh in range(g):
        sl = slice(hh * LANE, (hh + 1) * LANE)
        q = q_ref[:, sl].astype(BF16)
        s_ctx = _dot_nt(q, kc) * scale
        m = jnp.max(s_ctx, axis=-1, keepdims=True)
        if has_band:
            s_loc = _dot_nt(q, kb) * scale + bias
            m = jnp.maximum(m, jnp.max(s_loc, axis=-1, keepdims=True))
        if has_sink:
            snk = s_ref[:, hh * LANE:hh * LANE + 1]
            m = jnp.maximum(m, snk)
        p_ctx = jnp.exp(s_ctx - m)
        den = jnp.sum(p_ctx, axis=-1, keepdims=True)
        acc = _dot(p_ctx.astype(BF16), vc)
        if has_band:
            p_loc = jnp.exp(s_loc - m)
            den = den + jnp.sum(p_loc, axis=-1, keepdims=True)
            acc = acc + _dot(p_loc.astype(BF16), vb)
        if has_sink:
            den = den + jnp.exp(snk - m)
        o_ref[:, sl] = (acc / den).astype(o_ref.dtype)


def _pattern(i, n):
    return jnp.where(i == 0, 0, jnp.where(i == n - 1, 2, 1))


def _attn_latent(q, q_blk0, q_col0, k, k_col0, k_off, v, v_col0, v_off, kc, kc_col0, vc, vc_col0,
                 bias, sink, *, n_heads, n_kv, mq, wk, back, name):
    g = n_heads // n_kv
    nq = N_LAT // mq
    per_head_bias = bias.shape[0] > 1
    in_specs = [
        pl.BlockSpec((mq, g * LANE), lambda j, i: (i + q_blk0, q_col0 + j)),
        pl.BlockSpec((k.shape[0], LANE), lambda j, i: (0, k_col0 + j)),
        pl.BlockSpec((v.shape[0], LANE), lambda j, i: (0, v_col0 + j)),
        pl.BlockSpec((N_CTX, LANE), lambda j, i: (0, kc_col0 + j)),
        pl.BlockSpec((N_CTX, LANE), lambda j, i: (0, vc_col0 + j)),
        pl.BlockSpec((1, 1, mq, wk), lambda j, i: (j if per_head_bias else 0, _pattern(i, nq), 0, 0)),
    ]
    args = [q, k, v, kc, vc, bias]
    if sink is not None:
        in_specs.append(pl.BlockSpec((1, g * LANE), lambda j, i: (0, j)))
        args.append(sink)
    kern = functools.partial(_attn_kernel, g=g, mq=mq, wk=wk, back=back, n_keys=N_LAT, k_off=k_off, v_off=v_off,
                             has_band=True, has_sink=sink is not None, scale=LANE ** -0.5)
    return pl.pallas_call(
        kern, grid=(n_kv, nq), in_specs=in_specs,
        out_specs=pl.BlockSpec((mq, g * LANE), lambda j, i: (i, j)),
        out_shape=jax.ShapeDtypeStruct((N_LAT, n_heads * LANE), BF16),
        compiler_params=_cparams(("arbitrary", "arbitrary")), name=name,
    )(*args)


def _attn_context(q, q_col0, kc, kc_col0, vc, vc_col0, sink, *, n_heads, n_kv, name):
    g = n_heads // n_kv
    in_specs = [
        pl.BlockSpec((N_CTX, g * LANE), lambda j, i: (0, q_col0 + j)),
        pl.BlockSpec((N_CTX, LANE), lambda j, i: (0, kc_col0 + j)),
        pl.BlockSpec((N_CTX, LANE), lambda j, i: (0, vc_col0 + j)),
    ]
    args = [q, kc, vc]
    if sink is not None:
        in_specs.append(pl.BlockSpec((1, g * LANE), lambda j, i: (0, j)))
        args.append(sink)
    kern = functools.partial(_attn_kernel, g=g, mq=N_CTX, wk=0, back=0, n_keys=0, k_off=0, v_off=0,
                             has_band=False, has_sink=sink is not None, scale=LANE ** -0.5)
    return pl.pallas_call(
        kern, grid=(n_kv, 1), in_specs=in_specs,
        out_specs=pl.BlockSpec((N_CTX, g * LANE), lambda j, i: (0, j)),
        out_shape=jax.ShapeDtypeStruct((N_CTX, n_heads * LANE), BF16),
        compiler_params=_cparams(("arbitrary", "arbitrary")), name=name,
    )(*args)


def _window_bias():
    t, w = N_LAT, B_WINDOW
    nb = t // w
    out = []
    for bi in (0, 1, nb - 1):
        ub = min(max(bi * w - w, 0), t - 3 * w)
        qpos = bi * w + np.arange(w)[:, None]
        kpos = ub + np.arange(3 * w)[None, :]
        out.append(np.where(np.abs(kpos - qpos) <= w, 0.0, NEG))
    return jnp.asarray(np.stack(out)[None], F32)


def _na_bias(rpb):
    col = np.arange(GRID_W)
    c0 = np.clip(col - NA_KW // 2, 0, GRID_W - NA_KW)
    col_ok = (col[None, :] >= c0[:, None]) & (col[None, :] < c0[:, None] + NA_KW)
    dc = np.clip(col[None, :] - col[:, None] + (NA_KW - 1), 0, 2 * NA_KW - 2)
    onehot = (dc[None] == np.arange(2 * NA_KW - 1)[:, None, None]).astype(np.float32)
    t = jnp.einsum("hrd,dck->hrck", rpb.astype(F32), jnp.asarray(onehot), precision=lax.Precision.HIGHEST)
    t = jnp.where(col_ok[None, None], t, NEG)
    masked = jnp.full((C_HEADS, GRID_W, GRID_W), NEG, F32)
    n_groups = GRID_H // NA_ROWS
    out = []
    for gi in (0, 1, n_groups - 1):
        r = gi * NA_ROWS + np.arange(NA_ROWS)
        ub = min(max(gi * NA_ROWS - NA_KH // 2, 0), GRID_H - NA_BAND)
        rs = np.clip(r - NA_KH // 2, 0, GRID_H - NA_KH)
        krow = ub + np.arange(NA_BAND)
        row_ok = (krow[None, :] >= rs[:, None]) & (krow[None, :] < rs[:, None] + NA_KH)
        dr = krow[None, :] - r[:, None] + (NA_KH - 1)
        rows = [jnp.stack([t[:, dr[ri, a]] if row_ok[ri, a] else masked for a in range(NA_BAND)], axis=2)
                for ri in range(NA_ROWS)]
        out.append(jnp.stack(rows, axis=1).reshape(C_HEADS, NA_ROWS * GRID_W, NA_BAND * GRID_W))
    return jnp.stack(out, axis=1)


def _log_sigmoid(x):
    return jnp.minimum(x, 0.0) - jnp.log1p(jnp.exp(-jnp.abs(x)))


def _mlstm_kernel(qk_ref, v_ref, g_ref, kt_ref, gt_ref, h_ref, c_ref, m_ref, *, d):
    L = A_CHUNK

    @pl.when(pl.program_id(0) == 0)
    def _():
        c_ref[...] = jnp.zeros_like(c_ref)
        m_ref[...] = jnp.zeros_like(m_ref)

    ri = lax.broadcasted_iota(jnp.int32, (L, L), 0)
    ci = lax.broadcasted_iota(jnp.int32, (L, L), 1)
    seen = (ci <= ri) if d == 0 else (ci >= ri)
    seen_t = (ri <= ci) if d == 0 else (ri >= ci)
    ones_col = (lax.broadcasted_iota(jnp.int32, (L, LANE), 1) == 0).astype(F32)
    i_lane, f_lane = 2 * d * A_HEADS, (2 * d + 1) * A_HEADS

    def chunk(cidx, carry):
        c = cidx if d == 0 else A_SPAN // L - 1 - cidx
        r0 = pl.multiple_of(c * L, L)
        gc = g_ref[pl.ds(r0, L), :]
        gt = gt_ref[c]
        for h in range(A_HEADS):
            sl = slice(h * LANE, (h + 1) * LANE)
            q = (qk_ref[pl.ds(r0, L), sl][:, :A_DK] * (A_DK ** -0.5)).astype(BF16)
            kt = kt_ref[c, h * A_DK:(h + 1) * A_DK, :]
            v_aug = jnp.concatenate([v_ref[pl.ds(r0, L), sl], ones_col], axis=1).astype(BF16)
            i_row = gt[i_lane + h:i_lane + h + 1, :]
            f_row = _log_sigmoid(gt[f_lane + h:f_lane + h + 1, :])
            f_col = _log_sigmoid(gc[:, f_lane + h:f_lane + h + 1])
            cum_col = jnp.sum(jnp.where(seen, f_row, 0.0), axis=1, keepdims=True)
            cum_row = jnp.sum(jnp.where(seen_t, f_col, 0.0), axis=0, keepdims=True)
            total = jnp.sum(f_row, axis=1, keepdims=True)
            m_old = m_ref[h:h + 1, 0:1]
            dm = jnp.where(seen, cum_col - cum_row + i_row, NEG)
            inter = cum_col + m_old
            mt = jnp.maximum(inter, jnp.max(dm, axis=1, keepdims=True))
            sw = _dot(q, kt.astype(BF16)) * jnp.exp(dm - mt)
            a = jnp.exp(inter - mt)
            c_old = c_ref[h]
            na = _dot(sw.astype(BF16), v_aug) + a * _dot(q, c_old.astype(BF16))
            den = jnp.maximum(jnp.abs(na[:, LANE:LANE + 1]), jnp.exp(-mt))
            h_ref[pl.ds(r0, L), sl] = na[:, :LANE] / den
            wend = total - cum_row + i_row
            m_new = jnp.maximum(total + m_old, jnp.max(wend, axis=1, keepdims=True))
            decay = jnp.exp(total + m_old - m_new)
            wv = jnp.exp(wend - m_new)
            c_ref[h] = decay * c_old + _dot((kt * wv).astype(BF16), v_aug)
            m_ref[h:h + 1, :] = jnp.broadcast_to(m_new, (1, LANE))
        return carry

    lax.fori_loop(0, A_SPAN // L, chunk, 0)


def _mlstm(p, kt3, gt3, d):
    n_span = N_TOK // A_SPAN
    cps = A_SPAN // A_CHUNK
    if d == 0:
        span = lambda s: s
    else:
        span = lambda s: jnp.where(s == 0, 0, n_span - s)
    return pl.pallas_call(
        functools.partial(_mlstm_kernel, d=d), grid=(n_span,),
        in_specs=[pl.BlockSpec((A_SPAN, 1024), lambda s: (span(s), COL_QK // 1024)),
                  pl.BlockSpec((A_SPAN, 1024), lambda s: (span(s), COL_AV // 1024)),
                  pl.BlockSpec((A_SPAN, LANE), lambda s: (span(s), COL_G // LANE)),
                  pl.BlockSpec((cps, A_HEADS * A_DK, A_CHUNK), lambda s: (span(s), 0, 0)),
                  pl.BlockSpec((cps, 4 * A_HEADS, A_CHUNK), lambda s: (span(s), 0, 0))],
        out_specs=pl.BlockSpec((A_SPAN, 1024), lambda s: (span(s), 0)),
        out_shape=jax.ShapeDtypeStruct((N_TOK, 1024), F32),
        scratch_shapes=[pltpu.VMEM((A_HEADS, A_DK, 2 * LANE), F32), pltpu.VMEM((A_HEADS, LANE), F32)],
        compiler_params=_cparams(("arbitrary",)), name="mlstm_fwd" if d == 0 else "mlstm_bwd",
    )(p, p, p, kt3, gt3)


def _aout_kernel(hf_ref, hb_ref, o_ref, g_ref, y_ref):
    for h in range(A_HEADS):
        sl = slice(h * LANE, (h + 1) * LANE)
        x = hf_ref[:, sl] + hb_ref[:, sl]
        x = x * lax.rsqrt(jnp.mean(x * x, axis=-1, keepdims=True) + EPS)
        o = o_ref[:, sl]
        y_ref[:, sl] = ((x * g_ref[:, sl]) * (1.0 / (1.0 + jnp.exp(-o)))).astype(y_ref.dtype)


def _aout(hf, hb, p, mh_g):
    tm = 256
    blk = pl.BlockSpec((tm, 1024), lambda i: (i, 0))
    return pl.pallas_call(
        _aout_kernel, grid=(N_TOK // tm,),
        in_specs=[blk, blk, pl.BlockSpec((tm, 1024), lambda i: (i, COL_AO // 1024)),
                  pl.BlockSpec((1, 1024), lambda i: (0, 0))],
        out_specs=blk, out_shape=jax.ShapeDtypeStruct((N_TOK, 1024), BF16),
        compiler_params=_cparams(("parallel",)), name="mlstm_out",
    )(hf, hb, p, mh_g.reshape(1, 1024))


def _top_values(s, k):
    vals = []
    cur = s
    for _ in range(k):
        mx = jnp.max(cur, axis=0, keepdims=True)
        vals.append(mx)
        cur = jnp.where(cur == mx, NEG, cur)
    return vals


def _router_kernel(ht_ref, wqt_ref, keys_ref, s_ref, e_ref, thr_ref, *, tn):
    ht = ht_ref[...]
    row8 = lax.broadcasted_iota(jnp.int32, (8, tn), 0)
    row16 = lax.broadcasted_iota(jnp.int32, (16, tn), 0)
    for h in range(P_HEADS):
        tops = []
        for p in range(2):
            hp = 2 * h + p
            q = _dot(wqt_ref[hp * LANE:(hp + 1) * LANE, :], ht).astype(BF16)
            s = _dot(keys_ref[hp], q)
            s_ref[hp] = s
            tops.append(_top_values(s, P_TOPK))
        ta, tb = tops
        a_all = jnp.full((16, tn), NEG, F32)
        b_hi = jnp.full((8, tn), NEG, F32)
        for i in range(16):
            a_all = jnp.where(row16 == i, ta[i], a_all)
        for j in range(8, 16):
            b_hi = jnp.where(row8 == j - 8, tb[j], b_hi)
        a_lo = a_all[:8]
        parts = [a_all + tb[0], a_lo + tb[1]]
        for j in range(2, 8):
            parts.append(jnp.where(row8 < P_TOPK // (j + 1), a_lo + tb[j], NEG))
        parts.append(ta[0] + b_hi)
        cand = jnp.concatenate(parts, axis=0)
        best = _top_values(cand, P_TOPK)
        z = jnp.zeros_like(best[0])
        for c in best:
            z = z + jnp.exp(c - best[0])
        thr_ref[h:h + 1, :] = best[P_TOPK - 1]
        e_ref[2 * h] = jnp.exp(s_ref[2 * h] - ta[0]) / z
        e_ref[2 * h + 1] = jnp.exp(s_ref[2 * h + 1] - tb[0])


def _router(ht, wqt, keys):
    tn = 256
    t = ht.shape[1]
    big = pl.BlockSpec((2 * P_HEADS, P_NKEYS, tn), lambda i: (0, 0, i))
    shp = jax.ShapeDtypeStruct((2 * P_HEADS, P_NKEYS, t), F32)
    return pl.pallas_call(
        functools.partial(_router_kernel, tn=tn), grid=(t // tn,),
        in_specs=[pl.BlockSpec((D, tn), lambda i: (0, i)),
                  pl.BlockSpec((D, D), lambda i: (0, 0)),
                  pl.BlockSpec((2 * P_HEADS, P_NKEYS, LANE), lambda i: (0, 0, 0))],
        out_specs=[big, big, pl.BlockSpec((P_HEADS, tn), lambda i: (0, i))],
        out_shape=[shp, shp, jax.ShapeDtypeStruct((P_HEADS, t), F32)],
        compiler_params=_cparams(("parallel",)), name="peer_router",
    )(ht, wqt, keys)


P_EC = 1024
P_TN = 768


def _gelu(x):
    return 0.5 * x * (1.0 + lax.erf(x * (2.0 ** -0.5)))


P_HALF = P_EC // 2
P_STEPS = P_EXPERTS // P_EC + 1


def _expert_gates(z_ref, w_ref, s1_ref, e1_ref, row0, s2_ref, e2_ref, thr_ref):
    for r in range(P_HALF // P_NKEYS):
        rows = slice(r * P_NKEYS, (r + 1) * P_NKEYS)
        for lb in range(P_TN // LANE):
            sl = slice(lb * LANE, (lb + 1) * LANE)
            gate = jnp.zeros((P_NKEYS, LANE), F32)
            for h in range(P_HEADS):
                pair = s1_ref[h, 0, row0 + r:row0 + r + 1, sl] + s2_ref[h, 0, :, sl]
                w = e1_ref[h, 0, row0 + r:row0 + r + 1, sl] * e2_ref[h, 0, :, sl]
                gate = gate + jnp.where(pair >= thr_ref[h:h + 1, sl], w, 0.0)
            w_ref[lb, rows, :] = (gate * _gelu(z_ref[lb, rows, :])).astype(BF16)


def _expert_kernel(ht_ref, u_ref, vt_ref, s1p_ref, s1c_ref, s2_ref, e1p_ref, e1c_ref, e2_ref, thr_ref, y_ref,
                   z0, z1, w0, w1):
    j = pl.program_id(1)
    last = P_STEPS - 1
    half_rows = P_HALF // P_NKEYS

    n_lb = P_TN // LANE

    def stage_a(z, half):
        zf = _dot(u_ref[half * P_HALF:(half + 1) * P_HALF, :], ht_ref[...])
        for lb in range(n_lb):
            z[lb] = zf[:, lb * LANE:(lb + 1) * LANE]

    def stage_b(w, half, y_old):
        wf = jnp.concatenate([w[lb] for lb in range(n_lb)], axis=1)
        y_ref[...] = y_old + _dot(vt_ref[:, half * P_HALF:(half + 1) * P_HALF], wf)

    @pl.when(j == 0)
    def _():
        stage_a(z0, 0)
        stage_a(z1, 1)
        _expert_gates(z0, w0, s1c_ref, e1c_ref, 0, s2_ref, e2_ref, thr_ref)

    @pl.when((j > 0) & (j < last))
    def _():
        stage_a(z0, 0)
        _expert_gates(z1, w1, s1p_ref, e1p_ref, half_rows, s2_ref, e2_ref, thr_ref)
        stage_b(w0, 0, jnp.where(j == 1, 0.0, y_ref[...]))
        stage_a(z1, 1)
        _expert_gates(z0, w0, s1c_ref, e1c_ref, 0, s2_ref, e2_ref, thr_ref)
        stage_b(w1, 1, y_ref[...])

    @pl.when(j == last)
    def _():
        _expert_gates(z1, w1, s1p_ref, e1p_ref, half_rows, s2_ref, e2_ref, thr_ref)
        stage_b(w0, 0, y_ref[...])
        stage_b(w1, 1, y_ref[...])


def _experts(ht, u, vt, s, e, thr):
    t = ht.shape[1]
    ac = P_EC // P_NKEYS
    n_blk = P_EXPERTS // P_EC
    s4 = s.reshape(P_HEADS, 2, P_NKEYS, t)
    e4 = e.reshape(P_HEADS, 2, P_NKEYS, t)
    once = pl.Buffered(1)
    cur = lambda j: jnp.minimum(j, n_blk - 1)
    prev = lambda j: jnp.maximum(j - 1, 0)
    first_p = pl.BlockSpec((P_HEADS, 1, ac, P_TN), lambda i, j: (0, 0, prev(j), i))
    first_c = pl.BlockSpec((P_HEADS, 1, ac, P_TN), lambda i, j: (0, 0, cur(j), i))
    second = pl.BlockSpec((P_HEADS, 1, P_NKEYS, P_TN), lambda i, j: (0, 1, 0, i), pipeline_mode=once)
    return pl.pallas_call(
        _expert_kernel, grid=(t // P_TN, P_STEPS),
        in_specs=[pl.BlockSpec((D, P_TN), lambda i, j: (0, i), pipeline_mode=once),
                  pl.BlockSpec((P_EC, D), lambda i, j: (cur(j), 0)),
                  pl.BlockSpec((D, P_EC), lambda i, j: (0, prev(j))),
                  first_p, first_c, second, first_p, first_c, second,
                  pl.BlockSpec((P_HEADS, P_TN), lambda i, j: (0, i), pipeline_mode=once)],
        out_specs=pl.BlockSpec((D, P_TN), lambda i, j: (0, i)),
        out_shape=jax.ShapeDtypeStruct((D, t), F32),
        scratch_shapes=[pltpu.VMEM((P_TN // LANE, P_HALF, LANE), F32), pltpu.VMEM((P_TN // LANE, P_HALF, LANE), F32),
                        pltpu.VMEM((P_TN // LANE, P_HALF, LANE), BF16), pltpu.VMEM((P_TN // LANE, P_HALF, LANE), BF16)],
        compiler_params=_cparams(("parallel", "arbitrary")), name="peer_experts",
    )(ht, u, vt, s4, s4, s4, e4, e4, e4, thr)


def _peer(h, wq, keys, u, v):
    ht = h.T
    s, e, thr = _router(ht, wq.T.astype(BF16), keys.reshape(2 * P_HEADS, P_NKEYS, LANE).astype(BF16))
    yt = _experts(ht, u.astype(BF16), v.T.astype(BF16), s, e, thr)
    return yt.T


def _ab_weights(w_in, gate_b):
    aq, ak, av, ao, ag, bq, bk, bv = jnp.split(w_in, np.cumsum([512, 512, 1024, 1024, 32, 1024, 256])[:].tolist(), axis=1)
    qk = jnp.concatenate([aq.reshape(D, A_HEADS, A_DK), ak.reshape(D, A_HEADS, A_DK)], axis=2).reshape(D, 1024)
    pad = jnp.zeros((D, AB_N - COL_G - 32), w_in.dtype)
    w = jnp.concatenate([qk, av, ao, bq, bk, bv, ag, pad], axis=1).astype(BF16)
    bias = jnp.zeros((1, AB_N), F32).at[0, COL_G:COL_G + 32].set(gate_b.astype(F32))
    return w, bias


def _mixer_ab(h, w_in, gate_b, mh_g, sink, rope_tabs, win_bias):
    w, bias = _ab_weights(w_in, gate_b)
    p = _matmul(h, w, bias=bias)
    n_chunk = N_TOK // A_CHUNK
    k_only = p[:, COL_QK:COL_QK + 1024].reshape(n_chunk, A_CHUNK, A_HEADS, 2, A_DK)[:, :, :, 1, :]
    kt3 = k_only.reshape(n_chunk, A_CHUNK, A_HEADS * A_DK).transpose(0, 2, 1)
    gt3 = p[:, COL_G:COL_G + 4 * A_HEADS].reshape(n_chunk, A_CHUNK, 4 * A_HEADS).transpose(0, 2, 1)
    hf = _mlstm(p, kt3, gt3, 0)
    hb = _mlstm(p, kt3, gt3, 1)
    ya = _aout(hf, hb, p, mh_g)
    qr, kr = _rope(p, *rope_tabs)
    sink_b = jnp.repeat(sink.astype(F32), LANE).reshape(1, B_HEADS * LANE)
    yb_l = _attn_latent(qr, 0, 0, kr, 0, 0, p, COL_BV // LANE, N_CTX, p, COL_BK // LANE, p, COL_BV // LANE,
                        win_bias, sink_b, n_heads=B_HEADS, n_kv=B_KV, mq=B_WINDOW, wk=3 * B_WINDOW, back=B_WINDOW,
                        name="window_attn")
    yb_c = _attn_context(p, COL_BQ // (4 * LANE), p, COL_BK // LANE, p, COL_BV // LANE, sink_b,
                         n_heads=B_HEADS, n_kv=B_KV, name="window_attn_ctx")
    return jnp.concatenate([ya, jnp.concatenate([yb_c, yb_l], axis=0)], axis=1)


def _mixer_c(h, w_in, rpb):
    p = _matmul(h, w_in.astype(BF16), out_dtype=BF16)
    bias = _na_bias(rpb)
    mq = NA_ROWS * GRID_W
    y_l = _attn_latent(p, N_CTX // mq, 0, p, C_HEADS, N_CTX, p, 2 * C_HEADS, N_CTX, p, C_HEADS, p, 2 * C_HEADS,
                       bias, None, n_heads=C_HEADS, n_kv=C_HEADS, mq=mq, wk=NA_BAND * GRID_W,
                       back=(NA_KH // 2) * GRID_W, name="na_attn")
    y_c = _attn_context(p, 0, p, C_HEADS, p, 2 * C_HEADS, None, n_heads=C_HEADS, n_kv=C_HEADS, name="na_attn_ctx")
    return jnp.concatenate([y_c, y_l], axis=0)


def _mod_rows(m6, i_shift, i_scale, i_gate):
    z = jnp.zeros((D,), F32)
    pick = lambda r, i: m6[r, i] if i is not None else z
    return jnp.stack([pick(0, i_shift), pick(0, i_scale), pick(1, i_shift), pick(1, i_scale),
                      pick(0, i_gate), pick(1, i_gate), z, z])


def kernel(x, c, ctx, c_ctx, ada_w, ada_b, norm1_g, norm2_g, ab_w_in, ab_gate_b, ab_mh_g, ab_sink, ab_w_out,
           na_w_in, na_rpb, na_w_out, peer_wq, peer_keys, peer_u, peer_v, final_g):
    xs = jnp.concatenate([ctx[0], x[0]], axis=0).astype(F32)
    cc = jnp.zeros((16, D), F32).at[0].set(c[0]).at[1].set(c_ctx)
    mods = _adaln(cc, ada_w, ada_b)[:, :2].reshape(DEPTH, 2, 6, D)
    rope_tabs = _rope_tables()
    win_bias = _window_bias()

    h = _norm(xs, norm1_g[0], _mod_rows(mods[0], 0, 1, None))
    for l in range(DEPTH):
        m6 = mods[l]
        if l % 2 == 0:
            e = l // 2
            ymix = _mixer_ab(h, ab_w_in[e], ab_gate_b[e], ab_mh_g[e], ab_sink[e], rope_tabs, win_bias)
            w_out = ab_w_out[e]
        else:
            o = l // 2
            ymix = _mixer_c(h, na_w_in[o], na_rpb[o])
            w_out = na_w_out[o]
        gv = jnp.concatenate([m6[:, 2], jnp.zeros((6, D), F32)], axis=0)
        xs = _matmul(ymix, w_out.astype(BF16), resid=xs, gates=gv)
        h2 = _norm(xs, norm2_g[l], _mod_rows(m6, 3, 4, None))
        y = _peer(h2, peer_wq[l], peer_keys[l], peer_u[l], peer_v[l])
        if l + 1 < DEPTH:
            mv = _mod_rows(mods[l + 1], 0, 1, None).at[4].set(m6[0, 5]).at[5].set(m6[1, 5])
            xs, h = _norm(xs, norm1_g[l + 1], mv, y=y)
        else:
            mv = _mod_rows(m6, None, None, 5)
            _, out = _norm(xs, final_g, mv, y=y, out_dtype=F32, row_off=N_CTX // 256)
    return out[None]
```

```python
import functools

import numpy as np
import jax
import jax.numpy as jnp
from jax import lax
from jax.experimental import pallas as pl
from jax.experimental.pallas import tpu as pltpu

F32 = jnp.float32
BF16 = jnp.bfloat16

D = 2048
N_LAT = 8192
N_CTX = 256
N_TOK = N_CTX + N_LAT
DEPTH = 4
GRID_W = 64
GRID_H = N_LAT // GRID_W
EPS = 1e-6
LANE = 128
NEG = -1e30

A_HEADS = 8
A_DK = 64
A_DV = 128
A_CHUNK = 64
A_SPAN = 256
B_HEADS = 8
B_KV = 2
B_WINDOW = 128
ROPE_THETA = 10000.0
C_HEADS = 16
NA_KH = 8
NA_KW = 16
NA_ROWS = 4
NA_BAND = NA_ROWS + NA_KH - 1
NA_HEADS_PER_STEP = 2
P_HEADS = 8
P_NKEYS = 128
P_TOPK = 16
P_EXPERTS = P_NKEYS * P_NKEYS

AB_N = 5120
COL_QK, COL_AV, COL_AO, COL_BQ, COL_BK, COL_BV, COL_G = 0, 1024, 2048, 3072, 4096, 4352, 4608

VMEM_LIMIT = 52 * 1024 * 1024


def _cparams(sem):
    return pltpu.CompilerParams(dimension_semantics=sem, vmem_limit_bytes=VMEM_LIMIT)


def _dot(a, b):
    return jnp.dot(a, b, preferred_element_type=F32)


def _dot_nt(a, b):
    return lax.dot_general(a, b, (((1,), (1,)), ((), ())), preferred_element_type=F32)


def _ada_kernel(c_ref, w_ref, b_ref, o_ref):
    c = c_ref[...]
    s = c / (1.0 + jnp.exp(-c))
    w = w_ref[0]
    s_hi = s.astype(BF16)
    s_lo = (s - s_hi.astype(F32)).astype(BF16)
    w_hi = w.astype(BF16)
    w_lo = (w - w_hi.astype(F32)).astype(BF16)
    o_ref[0] = _dot(s_hi, w_hi) + _dot(s_lo, w_hi) + _dot(s_hi, w_lo) + b_ref[0]


def _adaln(cc, ada_w, ada_b):
    tn = 1024
    n = ada_w.shape[-1]
    return pl.pallas_call(
        _ada_kernel,
        grid=(DEPTH, n // tn),
        in_specs=[pl.BlockSpec((16, D), lambda l, j: (0, 0)),
                  pl.BlockSpec((1, D, tn), lambda l, j: (l, 0, j)),
                  pl.BlockSpec((1, 1, tn), lambda l, j: (l, 0, j))],
        out_specs=pl.BlockSpec((1, 16, tn), lambda l, j: (l, 0, j)),
        out_shape=jax.ShapeDtypeStruct((DEPTH, 16, n), F32),
        compiler_params=_cparams(("arbitrary", "arbitrary")),
        name="adaln",
    )(cc, ada_w, ada_b.reshape(DEPTH, 1, n))


def _norm_kernel(*refs, has_resid, y_t, h_t, n_ctx, tm, row_off):
    if has_resid:
        x_ref, y_ref, g_ref, mv_ref, xo_ref, h_ref = refs
    else:
        x_ref, g_ref, mv_ref, h_ref = refs
    row = (pl.program_id(0) + row_off) * tm + lax.broadcasted_iota(jnp.int32, (tm, 1), 0)
    is_ctx = row < n_ctx
    x = x_ref[...]
    if has_resid:
        gate = jnp.where(is_ctx, mv_ref[5:6, :], mv_ref[4:5, :])
        y = y_ref[...].T if y_t else y_ref[...]
        x = x + gate * y
        xo_ref[...] = x
    ms = jnp.mean(x * x, axis=-1, keepdims=True)
    yn = (x * lax.rsqrt(ms + EPS)) * g_ref[...]
    shift = jnp.where(is_ctx, mv_ref[2:3, :], mv_ref[0:1, :])
    scale = jnp.where(is_ctx, mv_ref[3:4, :], mv_ref[1:2, :])
    h = yn * (1.0 + scale) + shift
    h_ref[...] = (h.T if h_t else h).astype(h_ref.dtype)


def _norm(x, g, mv, *, y=None, y_t=False, h_t=False, out_dtype=BF16, row_off=0):
    tm = 256
    n_rows = x.shape[0] - row_off * tm
    blk = pl.BlockSpec((tm, D), lambda i: (i + row_off, 0))
    oblk = pl.BlockSpec((tm, D), lambda i: (i, 0))
    yblk = pl.BlockSpec((D, tm), lambda i: (0, i + row_off)) if y_t else blk
    hblk = pl.BlockSpec((D, tm), lambda i: (0, i)) if h_t else oblk
    vec = pl.BlockSpec((1, D), lambda i: (0, 0))
    mvs = pl.BlockSpec((8, D), lambda i: (0, 0))
    has_resid = y is not None
    kern = functools.partial(_norm_kernel, has_resid=has_resid, y_t=y_t, h_t=h_t, n_ctx=N_CTX, tm=tm, row_off=row_off)
    h_shape = jax.ShapeDtypeStruct((D, n_rows) if h_t else (n_rows, D), out_dtype)
    if has_resid:
        return pl.pallas_call(
            kern, grid=(n_rows // tm,),
            in_specs=[blk, yblk, vec, mvs], out_specs=[oblk, hblk],
            out_shape=[jax.ShapeDtypeStruct((n_rows, D), F32), h_shape],
            compiler_params=_cparams(("parallel",)), name="resid_norm",
        )(x, y, g.reshape(1, D), mv)
    return pl.pallas_call(
        kern, grid=(n_rows // tm,),
        in_specs=[blk, vec, mvs], out_specs=hblk, out_shape=h_shape,
        compiler_params=_cparams(("parallel",)), name="norm",
    )(x, g.reshape(1, D), mv)


MM_TM, MM_TN = 768, 1024


def _mm_kernel(*refs, has_bias, has_resid, n_ctx, tm):
    a_ref, w_ref = refs[0], refs[1]
    o_ref = refs[-1]
    acc = _dot(a_ref[...], w_ref[...])
    k = 2
    if has_bias:
        acc = acc + refs[k][...]
        k += 1
    if has_resid:
        x_ref, gv_ref = refs[k], refs[k + 1]
        row = pl.program_id(0) * tm + lax.broadcasted_iota(jnp.int32, (tm, 1), 0)
        gate = jnp.where(row < n_ctx, gv_ref[1:2, :], gv_ref[0:1, :])
        acc = x_ref[...] + gate * acc
    o_ref[...] = acc.astype(o_ref.dtype)


def _matmul(a, w, *, bias=None, resid=None, gates=None, out_dtype=F32, tm=MM_TM, tn=MM_TN):
    m, k = a.shape
    n = w.shape[1]
    in_specs = [pl.BlockSpec((tm, k), lambda i, j: (i, 0)), pl.BlockSpec((k, tn), lambda i, j: (0, j))]
    args = [a, w]
    if bias is not None:
        in_specs.append(pl.BlockSpec((1, tn), lambda i, j: (0, j)))
        args.append(bias)
    if resid is not None:
        in_specs += [pl.BlockSpec((tm, tn), lambda i, j: (i, j)), pl.BlockSpec((8, tn), lambda i, j: (0, j))]
        args += [resid, gates]
    kern = functools.partial(_mm_kernel, has_bias=bias is not None, has_resid=resid is not None, n_ctx=N_CTX, tm=tm)
    return pl.pallas_call(
        kern, grid=(m // tm, n // tn), in_specs=in_specs,
        out_specs=pl.BlockSpec((tm, tn), lambda i, j: (i, j)),
        out_shape=jax.ShapeDtypeStruct((m, n), out_dtype),
        compiler_params=_cparams(("parallel", "arbitrary")), name="matmul",
    )(*args)


def _rope_tile(x, cos, sin):
    lane = lax.broadcasted_iota(jnp.int32, x.shape, 1)
    partner = jnp.where((lane % 64) < 32, pltpu.roll(x, 96, axis=1), pltpu.roll(x, 32, axis=1))
    return x * cos + partner * sin


def _rope_kernel(q_ref, k_ref, cos_ref, sin_ref, qo_ref, ko_ref):
    cos, sin = cos_ref[...], sin_ref[...]
    for h in range(B_HEADS):
        sl = slice(h * LANE, (h + 1) * LANE)
        qo_ref[:, sl] = _rope_tile(q_ref[:, sl], cos, sin).astype(qo_ref.dtype)
    for h in range(B_KV):
        sl = slice(h * LANE, (h + 1) * LANE)
        ko_ref[:, sl] = _rope_tile(k_ref[:, sl], cos, sin).astype(ko_ref.dtype)


def _rope_tables():
    t = jnp.arange(N_LAT)
    freqs = ROPE_THETA ** (-jnp.arange(32, dtype=F32) / 32)
    ar = (t // GRID_W).astype(F32)[:, None] * freqs[None, :]
    ac = (t % GRID_W).astype(F32)[:, None] * freqs[None, :]
    cos = jnp.concatenate([jnp.cos(ar), jnp.cos(ar), jnp.cos(ac), jnp.cos(ac)], axis=1)
    sin = jnp.concatenate([-jnp.sin(ar), jnp.sin(ar), -jnp.sin(ac), jnp.sin(ac)], axis=1)
    return cos, sin


def _rope(p, cos, sin):
    tr = 256
    off = N_CTX // tr
    return pl.pallas_call(
        _rope_kernel, grid=(N_LAT // tr,),
        in_specs=[pl.BlockSpec((tr, 1024), lambda i: (i + off, COL_BQ // 1024)),
                  pl.BlockSpec((tr, 256), lambda i: (i + off, COL_BK // 256)),
                  pl.BlockSpec((tr, LANE), lambda i: (i, 0)),
                  pl.BlockSpec((tr, LANE), lambda i: (i, 0))],
        out_specs=[pl.BlockSpec((tr, 1024), lambda i: (i, 0)), pl.BlockSpec((tr, 256), lambda i: (i, 0))],
        out_shape=[jax.ShapeDtypeStruct((N_LAT, 1024), BF16), jax.ShapeDtypeStruct((N_LAT, 256), BF16)],
        compiler_params=_cparams(("parallel",)), name="rope",
    )(p, p, cos, sin)


def _attn_kernel(*refs, g, kvps, bias_per_head, mq, wk, back, n_keys, k_off, v_off, has_band, has_sink, scale):
    refs = list(refs)
    q_ref = refs.pop(0)
    if has_band:
        k_ref, v_ref = refs.pop(0), refs.pop(0)
    kc_ref, vc_ref = refs.pop(0), refs.pop(0)
    if has_band:
        b_ref = refs.pop(0)
    if has_sink:
        s_ref = refs.pop(0)
    o_ref = refs.pop(0)

    if has_band:
        i = pl.program_id(1)
        ub = pl.multiple_of(jnp.clip(i * mq - back, 0, n_keys - wk), 64)
    for hq in range(kvps * g):
        kk = hq // g
        ksl = slice(kk * LANE, (kk + 1) * LANE)
        kc = kc_ref[:, ksl].astype(BF16)
        vc = vc_ref[:, ksl].astype(BF16)
        if has_band:
            kb = k_ref[pl.ds(k_off + ub, wk), ksl].astype(BF16)
            vb = v_ref[pl.ds(v_off + ub, wk), ksl].astype(BF16)
            bias = b_ref[kk if bias_per_head else 0, 0]
        hh = hq
        sl = slice(hh * LANE, (hh + 1) * LANE)
        q = q_ref[:, sl].astype(BF16)
        s_ctx = _dot_nt(q, kc) * scale
        m = jnp.max(s_ctx, axis=-1, keepdims=True)
        if has_band:
            s_loc = _dot_nt(q, kb) * scale + bias
            m = jnp.maximum(m, jnp.max(s_loc, axis=-1, keepdims=True))
        if has_sink:
            snk = s_ref[:, hh * LANE:hh * LANE + 1]
            m = jnp.maximum(m, snk)
        p_ctx = jnp.exp(s_ctx - m)
        den = jnp.sum(p_ctx, axis=-1, keepdims=True)
        acc = _dot(p_ctx.astype(BF16), vc)
        if has_band:
            p_loc = jnp.exp(s_loc - m)
            den = den + jnp.sum(p_loc, axis=-1, keepdims=True)
            acc = acc + _dot(p_loc.astype(BF16), vb)
        if has_sink:
            den = den + jnp.exp(snk - m)
        o_ref[:, sl] = (acc / den).astype(o_ref.dtype)


def _pattern(i, n):
    return jnp.where(i == 0, 0, jnp.where(i == n - 1, 2, 1))


def _attn_latent(q, q_blk0, q_col0, k, k_col0, k_off, v, v_col0, v_off, kc, kc_col0, vc, vc_col0,
                 bias, sink, *, n_heads, n_kv, kvps, mq, wk, back, name):
    g = n_heads // n_kv
    nq = N_LAT // mq
    per_head_bias = bias.shape[0] > 1
    kw, qw = kvps * LANE, kvps * g * LANE
    in_specs = [
        pl.BlockSpec((mq, qw), lambda j, i: (i + q_blk0, q_col0 // (kvps * g) + j)),
        pl.BlockSpec((k.shape[0], kw), lambda j, i: (0, k_col0 // kvps + j)),
        pl.BlockSpec((v.shape[0], kw), lambda j, i: (0, v_col0 // kvps + j)),
        pl.BlockSpec((N_CTX, kw), lambda j, i: (0, kc_col0 // kvps + j)),
        pl.BlockSpec((N_CTX, kw), lambda j, i: (0, vc_col0 // kvps + j)),
        pl.BlockSpec((kvps if per_head_bias else 1, 1, mq, wk),
                     lambda j, i: (j if per_head_bias else 0, _pattern(i, nq), 0, 0)),
    ]
    args = [q, k, v, kc, vc, bias]
    if sink is not None:
        in_specs.append(pl.BlockSpec((1, qw), lambda j, i: (0, j)))
        args.append(sink)
    kern = functools.partial(_attn_kernel, g=g, kvps=kvps, bias_per_head=per_head_bias, mq=mq, wk=wk, back=back,
                             n_keys=N_LAT, k_off=k_off, v_off=v_off, has_band=True, has_sink=sink is not None,
                             scale=LANE ** -0.5)
    return pl.pallas_call(
        kern, grid=(n_kv // kvps, nq), in_specs=in_specs,
        out_specs=pl.BlockSpec((mq, qw), lambda j, i: (i, j)),
        out_shape=jax.ShapeDtypeStruct((N_LAT, n_heads * LANE), BF16),
        compiler_params=_cparams(("arbitrary", "arbitrary")), name=name,
    )(*args)


def _attn_context(q, q_col0, kc, kc_col0, vc, vc_col0, sink, *, n_heads, n_kv, name):
    g = n_heads // n_kv
    in_specs = [
        pl.BlockSpec((N_CTX, g * LANE), lambda j, i: (0, q_col0 + j)),
        pl.BlockSpec((N_CTX, LANE), lambda j, i: (0, kc_col0 + j)),
        pl.BlockSpec((N_CTX, LANE), lambda j, i: (0, vc_col0 + j)),
    ]
    args = [q, kc, vc]
    if sink is not None:
        in_specs.append(pl.BlockSpec((1, g * LANE), lambda j, i: (0, j)))
        args.append(sink)
    kern = functools.partial(_attn_kernel, g=g, kvps=1, bias_per_head=False, mq=N_CTX, wk=0, back=0, n_keys=0,
                             k_off=0, v_off=0, has_band=False, has_sink=sink is not None, scale=LANE ** -0.5)
    return pl.pallas_call(
        kern, grid=(n_kv, 1), in_specs=in_specs,
        out_specs=pl.BlockSpec((N_CTX, g * LANE), lambda j, i: (0, j)),
        out_shape=jax.ShapeDtypeStruct((N_CTX, n_heads * LANE), BF16),
        compiler_params=_cparams(("arbitrary", "arbitrary")), name=name,
    )(*args)


def _window_bias():
    t, w = N_LAT, B_WINDOW
    nb = t // w
    out = []
    for bi in (0, 1, nb - 1):
        ub = min(max(bi * w - w, 0), t - 3 * w)
        qpos = bi * w + np.arange(w)[:, None]
        kpos = ub + np.arange(3 * w)[None, :]
        out.append(np.where(np.abs(kpos - qpos) <= w, 0.0, NEG))
    return jnp.asarray(np.stack(out)[None], F32)


def _na_bias(rpb):
    col = np.arange(GRID_W)
    c0 = np.clip(col - NA_KW // 2, 0, GRID_W - NA_KW)
    col_ok = (col[None, :] >= c0[:, None]) & (col[None, :] < c0[:, None] + NA_KW)
    dc = np.clip(col[None, :] - col[:, None] + (NA_KW - 1), 0, 2 * NA_KW - 2)
    onehot = (dc[None] == np.arange(2 * NA_KW - 1)[:, None, None]).astype(np.float32)
    t = jnp.einsum("hrd,dck->hrck", rpb.astype(F32), jnp.asarray(onehot), precision=lax.Precision.HIGHEST)
    t = jnp.where(col_ok[None, None], t, NEG)
    masked = jnp.full((C_HEADS, GRID_W, GRID_W), NEG, F32)
    n_groups = GRID_H // NA_ROWS
    out = []
    for gi in (0, 1, n_groups - 1):
        r = gi * NA_ROWS + np.arange(NA_ROWS)
        ub = min(max(gi * NA_ROWS - NA_KH // 2, 0), GRID_H - NA_BAND)
        rs = np.clip(r - NA_KH // 2, 0, GRID_H - NA_KH)
        krow = ub + np.arange(NA_BAND)
        row_ok = (krow[None, :] >= rs[:, None]) & (krow[None, :] < rs[:, None] + NA_KH)
        dr = krow[None, :] - r[:, None] + (NA_KH - 1)
        rows = [jnp.stack([t[:, dr[ri, a]] if row_ok[ri, a] else masked for a in range(NA_BAND)], axis=2)
                for ri in range(NA_ROWS)]
        out.append(jnp.stack(rows, axis=1).reshape(C_HEADS, NA_ROWS * GRID_W, NA_BAND * GRID_W))
    return jnp.stack(out, axis=1)


def _log_sigmoid(x):
    return jnp.minimum(x, 0.0) - jnp.log1p(jnp.exp(-jnp.abs(x)))


def _mlstm_kernel(qk_ref, v_ref, g_ref, h_ref, c_ref, m_ref, *, d):
    L = A_CHUNK
    P2 = 2 * L

    @pl.when(pl.program_id(0) == 0)
    def _():
        c_ref[...] = jnp.zeros_like(c_ref)
        m_ref[...] = jnp.zeros_like(m_ref)

    ri = lax.broadcasted_iota(jnp.int32, (L, P2), 0)
    ci = lax.broadcasted_iota(jnp.int32, (L, P2), 1)
    lane = lax.broadcasted_iota(jnp.int32, (1, P2), 1)
    ones_col = (lax.broadcasted_iota(jnp.int32, (P2, LANE), 1) == 0).astype(F32)
    i_lane, f_lane = 2 * d * A_HEADS, (2 * d + 1) * A_HEADS
    order = (0, 1) if d == 0 else (1, 0)

    def pair(pidx, carry):
        pp = pidx if d == 0 else A_SPAN // P2 - 1 - pidx
        r0 = pl.multiple_of(pp * P2, P2)
        gc = g_ref[pl.ds(r0, P2), :]
        gt = gc.T
        c_state = [c_ref[h] for h in range(A_HEADS)]
        m_state = [m_ref[h, 0:1, 0:1] for h in range(A_HEADS)]
        h_out = []
        for h in range(A_HEADS):
            sl = slice(h * LANE, (h + 1) * LANE)
            qk = qk_ref[pl.ds(r0, P2), sl]
            kt = qk.T[A_DK:, :]
            v_aug = jnp.concatenate([v_ref[pl.ds(r0, P2), sl], ones_col], axis=1).astype(BF16)
            i_row = gt[i_lane + h:i_lane + h + 1, :]
            f_row = _log_sigmoid(gt[f_lane + h:f_lane + h + 1, :])
            for sub in order:
                rows = slice(sub * L, (sub + 1) * L)
                own = (lane >= sub * L) & (lane < (sub + 1) * L)
                cs = ci - sub * L
                seen = ((cs <= ri) & (cs >= 0)) if d == 0 else ((cs >= ri) & (cs < L))
                seen_t = ((ri <= cs) & (cs < L)) if d == 0 else ((ri >= cs) & (cs >= 0))
                q = (qk[rows, :A_DK] * (A_DK ** -0.5)).astype(BF16)
                f_col = _log_sigmoid(gc[rows, f_lane + h:f_lane + h + 1])
                cum_col = jnp.sum(jnp.where(seen, f_row, 0.0), axis=1, keepdims=True)
                cum_row = jnp.sum(jnp.where(seen_t, f_col, 0.0), axis=0, keepdims=True)
                total = jnp.sum(jnp.where(own, f_row, 0.0), axis=1, keepdims=True)
                m_old, c_old = m_state[h], c_state[h]
                dm = jnp.where(seen, cum_col - cum_row + i_row, NEG)
                inter = cum_col + m_old
                mt = jnp.maximum(inter, jnp.max(dm, axis=1, keepdims=True))
                sw = _dot(q, kt.astype(BF16)) * jnp.exp(dm - mt)
                a = jnp.exp(inter - mt)
                na = _dot(sw.astype(BF16), v_aug) + a * _dot(q, c_old.astype(BF16))
                den = jnp.maximum(jnp.abs(na[:, LANE:LANE + 1]), jnp.exp(-mt))
                h_out.append((sub, sl, na[:, :LANE] / den))
                wend = jnp.where(own, total - cum_row + i_row, NEG)
                m_new = jnp.maximum(total + m_old, jnp.max(wend, axis=1, keepdims=True))
                decay = jnp.exp(total + m_old - m_new)
                wv = jnp.exp(wend - m_new)
                c_state[h] = decay * c_old + _dot((kt * wv).astype(BF16), v_aug)
                m_state[h] = m_new
        for sub, sl, val in h_out:
            h_ref[pl.ds(r0 + sub * L, L), sl] = val
        for h in range(A_HEADS):
            c_ref[h] = c_state[h]
            m_ref[h] = jnp.broadcast_to(m_state[h], (8, LANE))
        return carry

    lax.fori_loop(0, A_SPAN // P2, pair, 0)


def _mlstm(p, d):
    n_span = N_TOK // A_SPAN
    if d == 0:
        span = lambda s: s
    else:
        span = lambda s: jnp.where(s == 0, 0, n_span - s)
    return pl.pallas_call(
        functools.partial(_mlstm_kernel, d=d), grid=(n_span,),
        in_specs=[pl.BlockSpec((A_SPAN, 1024), lambda s: (span(s), COL_QK // 1024)),
                  pl.BlockSpec((A_SPAN, 1024), lambda s: (span(s), COL_AV // 1024)),
                  pl.BlockSpec((A_SPAN, LANE), lambda s: (span(s), COL_G // LANE))],
        out_specs=pl.BlockSpec((A_SPAN, 1024), lambda s: (span(s), 0)),
        out_shape=jax.ShapeDtypeStruct((N_TOK, 1024), F32),
        scratch_shapes=[pltpu.VMEM((A_HEADS, A_DK, 2 * LANE), F32), pltpu.VMEM((A_HEADS, 8, LANE), F32)],
        compiler_params=_cparams(("arbitrary",)), name="mlstm_fwd" if d == 0 else "mlstm_bwd",
    )(p, p, p)


def _aout_kernel(hf_ref, hb_ref, o_ref, g_ref, y_ref):
    for h in range(A_HEADS):
        sl = slice(h * LANE, (h + 1) * LANE)
        x = hf_ref[:, sl] + hb_ref[:, sl]
        x = x * lax.rsqrt(jnp.mean(x * x, axis=-1, keepdims=True) + EPS)
        o = o_ref[:, sl]
        y_ref[:, sl] = ((x * g_ref[:, sl]) * (1.0 / (1.0 + jnp.exp(-o)))).astype(y_ref.dtype)


def _aout(hf, hb, p, mh_g):
    tm = 256
    blk = pl.BlockSpec((tm, 1024), lambda i: (i, 0))
    return pl.pallas_call(
        _aout_kernel, grid=(N_TOK // tm,),
        in_specs=[blk, blk, pl.BlockSpec((tm, 1024), lambda i: (i, COL_AO // 1024)),
                  pl.BlockSpec((1, 1024), lambda i: (0, 0))],
        out_specs=blk, out_shape=jax.ShapeDtypeStruct((N_TOK, 1024), BF16),
        compiler_params=_cparams(("parallel",)), name="mlstm_out",
    )(hf, hb, p, mh_g.reshape(1, 1024))


def _top_values(s, k):
    vals = []
    cur = s
    for _ in range(k):
        mx = jnp.max(cur, axis=0, keepdims=True)
        vals.append(mx)
        cur = jnp.where(cur == mx, NEG, cur)
    return vals


def _router_kernel(ht_ref, wqt_ref, keys_ref, s_ref, e_ref, thr_ref, *, tn):
    ht = ht_ref[...]
    row8 = lax.broadcasted_iota(jnp.int32, (8, tn), 0)
    row16 = lax.broadcasted_iota(jnp.int32, (16, tn), 0)
    for h in range(P_HEADS):
        tops = []
        for p in range(2):
            hp = 2 * h + p
            q = _dot(wqt_ref[hp * LANE:(hp + 1) * LANE, :], ht).astype(BF16)
            s = _dot(keys_ref[hp], q)
            s_ref[hp] = s
            tops.append(_top_values(s, P_TOPK))
        ta, tb = tops
        a_all = jnp.full((16, tn), NEG, F32)
        b_hi = jnp.full((8, tn), NEG, F32)
        for i in range(16):
            a_all = jnp.where(row16 == i, ta[i], a_all)
        for j in range(8, 16):
            b_hi = jnp.where(row8 == j - 8, tb[j], b_hi)
        a_lo = a_all[:8]
        parts = [a_all + tb[0], a_lo + tb[1]]
        for j in range(2, 8):
            parts.append(jnp.where(row8 < P_TOPK // (j + 1), a_lo + tb[j], NEG))
        parts.append(ta[0] + b_hi)
        cand = jnp.concatenate(parts, axis=0)
        best = _top_values(cand, P_TOPK)
        z = jnp.zeros_like(best[0])
        for c in best:
            z = z + jnp.exp(c - best[0])
        thr_ref[h:h + 1, :] = best[P_TOPK - 1]
        e_ref[2 * h] = jnp.exp(s_ref[2 * h] - ta[0]) / z
        e_ref[2 * h + 1] = jnp.exp(s_ref[2 * h + 1] - tb[0])


def _router(ht, wqt, keys):
    tn = 256
    t = ht.shape[1]
    big = pl.BlockSpec((2 * P_HEADS, P_NKEYS, tn), lambda i: (0, 0, i))
    shp = jax.ShapeDtypeStruct((2 * P_HEADS, P_NKEYS, t), F32)
    return pl.pallas_call(
        functools.partial(_router_kernel, tn=tn), grid=(t // tn,),
        in_specs=[pl.BlockSpec((D, tn), lambda i: (0, i)),
                  pl.BlockSpec((D, D), lambda i: (0, 0)),
                  pl.BlockSpec((2 * P_HEADS, P_NKEYS, LANE), lambda i: (0, 0, 0))],
        out_specs=[big, big, pl.BlockSpec((P_HEADS, tn), lambda i: (0, i))],
        out_shape=[shp, shp, jax.ShapeDtypeStruct((P_HEADS, t), F32)],
        compiler_params=_cparams(("parallel",)), name="peer_router",
    )(ht, wqt, keys)


P_EC = 1024
P_TN = 768


def _gelu(x):
    return 0.5 * x * (1.0 + lax.erf(x * (2.0 ** -0.5)))


P_HALF = P_EC // 2
P_STEPS = P_EXPERTS // P_EC + 1
P_GATE_ROWS = 32


def _expert_gates(z_ref, w_ref, s1_ref, e1_ref, row0, s2_ref, e2_ref, thr_ref):
    n_r = P_HALF // P_NKEYS
    for lb in range(P_TN // LANE):
        sl = slice(lb * LANE, (lb + 1) * LANE)
        for b0 in range(0, P_NKEYS, P_GATE_ROWS):
            gates = [jnp.zeros((P_GATE_ROWS, LANE), F32) for _ in range(n_r)]
            for h in range(P_HEADS):
                s2 = s2_ref[h, 0, b0:b0 + P_GATE_ROWS, sl]
                e2 = e2_ref[h, 0, b0:b0 + P_GATE_ROWS, sl]
                thr = thr_ref[h:h + 1, sl]
                for r in range(n_r):
                    pair = s1_ref[h, 0, row0 + r:row0 + r + 1, sl] + s2
                    w = e1_ref[h, 0, row0 + r:row0 + r + 1, sl] * e2
                    gates[r] = gates[r] + jnp.where(pair >= thr, w, 0.0)
            for r in range(n_r):
                rows = slice(r * P_NKEYS + b0, r * P_NKEYS + b0 + P_GATE_ROWS)
                w_ref[lb, rows, :] = (gates[r] * _gelu(z_ref[lb, rows, :])).astype(BF16)


def _expert_kernel(ht_ref, u_ref, vt_ref, s1p_ref, s1c_ref, s2_ref, e1p_ref, e1c_ref, e2_ref, thr_ref, y_ref,
                   z0, z1, w0, w1):
    j = pl.program_id(1)
    last = P_STEPS - 1
    half_rows = P_HALF // P_NKEYS

    n_lb = P_TN // LANE

    def stage_a(z, half):
        zf = _dot(u_ref[half * P_HALF:(half + 1) * P_HALF, :], ht_ref[...])
        for lb in range(n_lb):
            z[lb] = zf[:, lb * LANE:(lb + 1) * LANE]

    def stage_b(w, half, y_old):
        wf = jnp.concatenate([w[lb] for lb in range(n_lb)], axis=1)
        y_ref[...] = y_old + _dot(vt_ref[:, half * P_HALF:(half + 1) * P_HALF], wf)

    @pl.when(j == 0)
    def _():
        stage_a(z0, 0)
        stage_a(z1, 1)
        _expert_gates(z0, w0, s1c_ref, e1c_ref, 0, s2_ref, e2_ref, thr_ref)

    @pl.when((j > 0) & (j < last))
    def _():
        stage_a(z0, 0)
        _expert_gates(z1, w1, s1p_ref, e1p_ref, half_rows, s2_ref, e2_ref, thr_ref)
        stage_b(w0, 0, jnp.where(j == 1, 0.0, y_ref[...]))
        stage_a(z1, 1)
        _expert_gates(z0, w0, s1c_ref, e1c_ref, 0, s2_ref, e2_ref, thr_ref)
        stage_b(w1, 1, y_ref[...])

    @pl.when(j == last)
    def _():
        _expert_gates(z1, w1, s1p_ref, e1p_ref, half_rows, s2_ref, e2_ref, thr_ref)
        stage_b(w0, 0, y_ref[...])
        stage_b(w1, 1, y_ref[...])


def _experts(ht, u, vt, s, e, thr):
    t = ht.shape[1]
    ac = P_EC // P_NKEYS
    n_blk = P_EXPERTS // P_EC
    s4 = s.reshape(P_HEADS, 2, P_NKEYS, t)
    e4 = e.reshape(P_HEADS, 2, P_NKEYS, t)
    once = pl.Buffered(1)
    cur = lambda j: jnp.minimum(j, n_blk - 1)
    prev = lambda j: jnp.maximum(j - 1, 0)
    first_p = pl.BlockSpec((P_HEADS, 1, ac, P_TN), lambda i, j: (0, 0, prev(j), i))
    first_c = pl.BlockSpec((P_HEADS, 1, ac, P_TN), lambda i, j: (0, 0, cur(j), i))
    second = pl.BlockSpec((P_HEADS, 1, P_NKEYS, P_TN), lambda i, j: (0, 1, 0, i), pipeline_mode=once)
    return pl.pallas_call(
        _expert_kernel, grid=(t // P_TN, P_STEPS),
        in_specs=[pl.BlockSpec((D, P_TN), lambda i, j: (0, i), pipeline_mode=once),
                  pl.BlockSpec((P_EC, D), lambda i, j: (cur(j), 0)),
                  pl.BlockSpec((D, P_EC), lambda i, j: (0, prev(j))),
                  first_p, first_c, second, first_p, first_c, second,
                  pl.BlockSpec((P_HEADS, P_TN), lambda i, j: (0, i), pipeline_mode=once)],
        out_specs=pl.BlockSpec((D, P_TN), lambda i, j: (0, i)),
        out_shape=jax.ShapeDtypeStruct((D, t), F32),
        scratch_shapes=[pltpu.VMEM((P_TN // LANE, P_HALF, LANE), F32), pltpu.VMEM((P_TN // LANE, P_HALF, LANE), F32),
                        pltpu.VMEM((P_TN // LANE, P_HALF, LANE), BF16), pltpu.VMEM((P_TN // LANE, P_HALF, LANE), BF16)],
        compiler_params=_cparams(("parallel", "arbitrary")), name="peer_experts",
    )(ht, u, vt, s4, s4, s4, e4, e4, e4, thr)


def _peer(ht, wq, keys, u, v):
    s, e, thr = _router(ht, wq.T.astype(BF16), keys.reshape(2 * P_HEADS, P_NKEYS, LANE).astype(BF16))
    return _experts(ht, u.astype(BF16), v.T.astype(BF16), s, e, thr)


def _ab_weights(w_in, gate_b):
    aq, ak, av, ao, ag, bq, bk, bv = jnp.split(w_in, np.cumsum([512, 512, 1024, 1024, 32, 1024, 256])[:].tolist(), axis=1)
    qk = jnp.concatenate([aq.reshape(D, A_HEADS, A_DK), ak.reshape(D, A_HEADS, A_DK)], axis=2).reshape(D, 1024)
    pad = jnp.zeros((D, AB_N - COL_G - 32), w_in.dtype)
    w = jnp.concatenate([qk, av, ao, bq, bk, bv, ag, pad], axis=1).astype(BF16)
    bias = jnp.zeros((1, AB_N), F32).at[0, COL_G:COL_G + 32].set(gate_b.astype(F32))
    return w, bias


def _mixer_ab(h, w_in, gate_b, mh_g, sink, rope_tabs, win_bias):
    w, bias = _ab_weights(w_in, gate_b)
    p = _matmul(h, w, bias=bias)
    hf = _mlstm(p, 0)
    hb = _mlstm(p, 1)
    ya = _aout(hf, hb, p, mh_g)
    qr, kr = _rope(p, *rope_tabs)
    sink_b = jnp.repeat(sink.astype(F32), LANE).reshape(1, B_HEADS * LANE)
    yb_l = _attn_latent(qr, 0, 0, kr, 0, 0, p, COL_BV // LANE, N_CTX, p, COL_BK // LANE, p, COL_BV // LANE,
                        win_bias, sink_b, n_heads=B_HEADS, n_kv=B_KV, kvps=1, mq=B_WINDOW, wk=3 * B_WINDOW,
                        back=B_WINDOW, name="window_attn")
    yb_c = _attn_context(p, COL_BQ // (4 * LANE), p, COL_BK // LANE, p, COL_BV // LANE, sink_b,
                         n_heads=B_HEADS, n_kv=B_KV, name="window_attn_ctx")
    return jnp.concatenate([ya, jnp.concatenate([yb_c, yb_l], axis=0)], axis=1)


def _mixer_c(h, w_in, rpb):
    p = _matmul(h, w_in.astype(BF16), out_dtype=BF16)
    bias = _na_bias(rpb)
    mq = NA_ROWS * GRID_W
    y_l = _attn_latent(p, N_CTX // mq, 0, p, C_HEADS, N_CTX, p, 2 * C_HEADS, N_CTX, p, C_HEADS, p, 2 * C_HEADS,
                       bias, None, n_heads=C_HEADS, n_kv=C_HEADS, kvps=NA_HEADS_PER_STEP, mq=mq, wk=NA_BAND * GRID_W,
                       back=(NA_KH // 2) * GRID_W, name="na_attn")
    y_c = _attn_context(p, 0, p, C_HEADS, p, 2 * C_HEADS, None, n_heads=C_HEADS, n_kv=C_HEADS, name="na_attn_ctx")
    return jnp.concatenate([y_c, y_l], axis=0)


def _mod_rows(m6, i_shift, i_scale, i_gate):
    z = jnp.zeros((D,), F32)
    pick = lambda r, i: m6[r, i] if i is not None else z
    return jnp.stack([pick(0, i_shift), pick(0, i_scale), pick(1, i_shift), pick(1, i_scale),
                      pick(0, i_gate), pick(1, i_gate), z, z])


def kernel(x, c, ctx, c_ctx, ada_w, ada_b, norm1_g, norm2_g, ab_w_in, ab_gate_b, ab_mh_g, ab_sink, ab_w_out,
           na_w_in, na_rpb, na_w_out, peer_wq, peer_keys, peer_u, peer_v, final_g):
    xs = jnp.concatenate([ctx[0], x[0]], axis=0).astype(F32)
    cc = jnp.zeros((16, D), F32).at[0].set(c[0]).at[1].set(c_ctx)
    mods = _adaln(cc, ada_w, ada_b)[:, :2].reshape(DEPTH, 2, 6, D)
    rope_tabs = _rope_tables()
    win_bias = _window_bias()

    h = _norm(xs, norm1_g[0], _mod_rows(mods[0], 0, 1, None))
    for l in range(DEPTH):
        m6 = mods[l]
        if l % 2 == 0:
            e = l // 2
            ymix = _mixer_ab(h, ab_w_in[e], ab_gate_b[e], ab_mh_g[e], ab_sink[e], rope_tabs, win_bias)
            w_out = ab_w_out[e]
        else:
            o = l // 2
            ymix = _mixer_c(h, na_w_in[o], na_rpb[o])
            w_out = na_w_out[o]
        gv = jnp.concatenate([m6[:, 2], jnp.zeros((6, D), F32)], axis=0)
        xs = _matmul(ymix, w_out.astype(BF16), resid=xs, gates=gv)
        h2t = _norm(xs, norm2_g[l], _mod_rows(m6, 3, 4, None), h_t=True)
        yt = _peer(h2t, peer_wq[l], peer_keys[l], peer_u[l], peer_v[l])
        if l + 1 < DEPTH:
            mv = _mod_rows(mods[l + 1], 0, 1, None).at[4].set(m6[0, 5]).at[5].set(m6[1, 5])
            xs, h = _norm(xs, norm1_g[l + 1], mv, y=yt, y_t=True)
        else:
            mv = _mod_rows(m6, None, None, 5)
            _, out = _norm(xs, final_g, mv, y=yt, y_t=True, out_dtype=F32, row_off=N_CTX // 256)
    return out[None]
```

```python
import functools

import numpy as np
import jax
import jax.numpy as jnp
from jax import lax
from jax.experimental import pallas as pl
from jax.experimental.pallas import tpu as pltpu

F32 = jnp.float32
BF16 = jnp.bfloat16

D = 2048
N_LAT = 8192
N_CTX = 256
N_TOK = N_CTX + N_LAT
DEPTH = 4
GRID_W = 64
GRID_H = N_LAT // GRID_W
EPS = 1e-6
LANE = 128
NEG = -1e30

A_HEADS = 8
A_DK = 64
A_DV = 128
A_CHUNK = 64
A_SPAN = 256
B_HEADS = 8
B_KV = 2
B_WINDOW = 128
ROPE_THETA = 10000.0
C_HEADS = 16
NA_KH = 8
NA_KW = 16
NA_ROWS = 4
NA_BAND = NA_ROWS + NA_KH - 1
NA_HEADS_PER_STEP = 2
P_HEADS = 8
P_NKEYS = 128
P_TOPK = 16
P_EXPERTS = P_NKEYS * P_NKEYS

AB_N = 5120
COL_QK, COL_AV, COL_AO, COL_BQ, COL_BK, COL_BV, COL_G = 0, 1024, 2048, 3072, 4096, 4352, 4608

VMEM_LIMIT = 52 * 1024 * 1024


def _cparams(sem):
    return pltpu.CompilerParams(dimension_semantics=sem, vmem_limit_bytes=VMEM_LIMIT)


def _dot(a, b):
    return jnp.dot(a, b, preferred_element_type=F32)


def _dot_nt(a, b):
    return lax.dot_general(a, b, (((1,), (1,)), ((), ())), preferred_element_type=F32)


def _ada_kernel(c_ref, w_ref, b_ref, o_ref):
    c = c_ref[...]
    s = c / (1.0 + jnp.exp(-c))
    w = w_ref[0]
    s_hi = s.astype(BF16)
    s_lo = (s - s_hi.astype(F32)).astype(BF16)
    w_hi = w.astype(BF16)
    w_lo = (w - w_hi.astype(F32)).astype(BF16)
    o_ref[0] = _dot(s_hi, w_hi) + _dot(s_lo, w_hi) + _dot(s_hi, w_lo) + b_ref[0]


def _adaln(cc, ada_w, ada_b):
    tn = 1024
    n = ada_w.shape[-1]
    return pl.pallas_call(
        _ada_kernel,
        grid=(DEPTH, n // tn),
        in_specs=[pl.BlockSpec((16, D), lambda l, j: (0, 0)),
                  pl.BlockSpec((1, D, tn), lambda l, j: (l, 0, j)),
                  pl.BlockSpec((1, 1, tn), lambda l, j: (l, 0, j))],
        out_specs=pl.BlockSpec((1, 16, tn), lambda l, j: (l, 0, j)),
        out_shape=jax.ShapeDtypeStruct((DEPTH, 16, n), F32),
        compiler_params=_cparams(("arbitrary", "arbitrary")),
        name="adaln",
    )(cc, ada_w, ada_b.reshape(DEPTH, 1, n))


def _norm_kernel(*refs, has_resid, y_t, h_t, n_ctx, tm, row_off):
    if has_resid:
        x_ref, y_ref, g_ref, mv_ref, xo_ref, h_ref = refs
    else:
        x_ref, g_ref, mv_ref, h_ref = refs
    row = (pl.program_id(0) + row_off) * tm + lax.broadcasted_iota(jnp.int32, (tm, 1), 0)
    is_ctx = row < n_ctx
    x = x_ref[...]
    if has_resid:
        gate = jnp.where(is_ctx, mv_ref[5:6, :], mv_ref[4:5, :])
        y = y_ref[...].T if y_t else y_ref[...]
        x = x + gate * y
        xo_ref[...] = x
    ms = jnp.mean(x * x, axis=-1, keepdims=True)
    yn = (x * lax.rsqrt(ms + EPS)) * g_ref[...]
    shift = jnp.where(is_ctx, mv_ref[2:3, :], mv_ref[0:1, :])
    scale = jnp.where(is_ctx, mv_ref[3:4, :], mv_ref[1:2, :])
    h = yn * (1.0 + scale) + shift
    h_ref[...] = (h.T if h_t else h).astype(h_ref.dtype)


def _norm(x, g, mv, *, y=None, y_t=False, h_t=False, out_dtype=BF16, row_off=0):
    tm = 256
    n_rows = x.shape[0] - row_off * tm
    blk = pl.BlockSpec((tm, D), lambda i: (i + row_off, 0))
    oblk = pl.BlockSpec((tm, D), lambda i: (i, 0))
    yblk = pl.BlockSpec((D, tm), lambda i: (0, i + row_off)) if y_t else blk
    hblk = pl.BlockSpec((D, tm), lambda i: (0, i)) if h_t else oblk
    vec = pl.BlockSpec((1, D), lambda i: (0, 0))
    mvs = pl.BlockSpec((8, D), lambda i: (0, 0))
    has_resid = y is not None
    kern = functools.partial(_norm_kernel, has_resid=has_resid, y_t=y_t, h_t=h_t, n_ctx=N_CTX, tm=tm, row_off=row_off)
    h_shape = jax.ShapeDtypeStruct((D, n_rows) if h_t else (n_rows, D), out_dtype)
    if has_resid:
        return pl.pallas_call(
            kern, grid=(n_rows // tm,),
            in_specs=[blk, yblk, vec, mvs], out_specs=[oblk, hblk],
            out_shape=[jax.ShapeDtypeStruct((n_rows, D), F32), h_shape],
            compiler_params=_cparams(("parallel",)), name="resid_norm",
        )(x, y, g.reshape(1, D), mv)
    return pl.pallas_call(
        kern, grid=(n_rows // tm,),
        in_specs=[blk, vec, mvs], out_specs=hblk, out_shape=h_shape,
        compiler_params=_cparams(("parallel",)), name="norm",
    )(x, g.reshape(1, D), mv)


MM_TM, MM_TN = 768, 1024


def _mm_kernel(*refs, has_bias, has_resid, n_ctx, tm):
    a_ref, w_ref = refs[0], refs[1]
    o_ref = refs[-1]
    acc = _dot(a_ref[...], w_ref[...])
    k = 2
    if has_bias:
        acc = acc + refs[k][...]
        k += 1
    if has_resid:
        x_ref, gv_ref = refs[k], refs[k + 1]
        row = pl.program_id(0) * tm + lax.broadcasted_iota(jnp.int32, (tm, 1), 0)
        gate = jnp.where(row < n_ctx, gv_ref[1:2, :], gv_ref[0:1, :])
        acc = x_ref[...] + gate * acc
    o_ref[...] = acc.astype(o_ref.dtype)


def _matmul(a, w, *, bias=None, resid=None, gates=None, out_dtype=F32, tm=MM_TM, tn=MM_TN):
    m, k = a.shape
    n = w.shape[1]
    in_specs = [pl.BlockSpec((tm, k), lambda i, j: (i, 0)), pl.BlockSpec((k, tn), lambda i, j: (0, j))]
    args = [a, w]
    if bias is not None:
        in_specs.append(pl.BlockSpec((1, tn), lambda i, j: (0, j)))
        args.append(bias)
    if resid is not None:
        in_specs += [pl.BlockSpec((tm, tn), lambda i, j: (i, j)), pl.BlockSpec((8, tn), lambda i, j: (0, j))]
        args += [resid, gates]
    kern = functools.partial(_mm_kernel, has_bias=bias is not None, has_resid=resid is not None, n_ctx=N_CTX, tm=tm)
    return pl.pallas_call(
        kern, grid=(m // tm, n // tn), in_specs=in_specs,
        out_specs=pl.BlockSpec((tm, tn), lambda i, j: (i, j)),
        out_shape=jax.ShapeDtypeStruct((m, n), out_dtype),
        compiler_params=_cparams(("parallel", "arbitrary")), name="matmul",
    )(*args)


def _rope_tile(x, cos, sin):
    lane = lax.broadcasted_iota(jnp.int32, x.shape, 1)
    partner = jnp.where((lane % 64) < 32, pltpu.roll(x, 96, axis=1), pltpu.roll(x, 32, axis=1))
    return x * cos + partner * sin


def _rope_kernel(q_ref, k_ref, cos_ref, sin_ref, qo_ref, ko_ref):
    cos, sin = cos_ref[...], sin_ref[...]
    for h in range(B_HEADS):
        sl = slice(h * LANE, (h + 1) * LANE)
        qo_ref[:, sl] = _rope_tile(q_ref[:, sl], cos, sin).astype(qo_ref.dtype)
    for h in range(B_KV):
        sl = slice(h * LANE, (h + 1) * LANE)
        ko_ref[:, sl] = _rope_tile(k_ref[:, sl], cos, sin).astype(ko_ref.dtype)


def _rope_tables():
    t = jnp.arange(N_LAT)
    freqs = ROPE_THETA ** (-jnp.arange(32, dtype=F32) / 32)
    ar = (t // GRID_W).astype(F32)[:, None] * freqs[None, :]
    ac = (t % GRID_W).astype(F32)[:, None] * freqs[None, :]
    cos = jnp.concatenate([jnp.cos(ar), jnp.cos(ar), jnp.cos(ac), jnp.cos(ac)], axis=1)
    sin = jnp.concatenate([-jnp.sin(ar), jnp.sin(ar), -jnp.sin(ac), jnp.sin(ac)], axis=1)
    return cos, sin


def _rope(p, cos, sin):
    tr = 256
    off = N_CTX // tr
    return pl.pallas_call(
        _rope_kernel, grid=(N_LAT // tr,),
        in_specs=[pl.BlockSpec((tr, 1024), lambda i: (i + off, COL_BQ // 1024)),
                  pl.BlockSpec((tr, 256), lambda i: (i + off, COL_BK // 256)),
                  pl.BlockSpec((tr, LANE), lambda i: (i, 0)),
                  pl.BlockSpec((tr, LANE), lambda i: (i, 0))],
        out_specs=[pl.BlockSpec((tr, 1024), lambda i: (i, 0)), pl.BlockSpec((tr, 256), lambda i: (i, 0))],
        out_shape=[jax.ShapeDtypeStruct((N_LAT, 1024), BF16), jax.ShapeDtypeStruct((N_LAT, 256), BF16)],
        compiler_params=_cparams(("parallel",)), name="rope",
    )(p, p, cos, sin)


def _na_bias_block(b_ref, kk, i):
    lane = lax.broadcasted_iota(jnp.int32, (GRID_W, LANE), 1)
    r0 = i * NA_ROWS
    ub = jnp.clip(r0 - NA_KH // 2, 0, GRID_H - NA_BAND)
    rows = []
    for ri in range(NA_ROWS):
        r = r0 + ri
        rs = jnp.clip(r - NA_KH // 2, 0, GRID_H - NA_KH)
        idx = []
        for a in range(NA_BAND):
            krow = ub + a
            ok = (krow >= rs) & (krow < rs + NA_KH)
            idx.append(jnp.where(ok, krow - r + (NA_KH - 1), 2 * NA_KH - 1))
        pieces = [jnp.where(lane < GRID_W, b_ref[kk, idx[a]], b_ref[kk, idx[a + 1]]) for a in range(0, NA_BAND - 1, 2)]
        pieces.append(b_ref[kk, idx[NA_BAND - 1]][:, :GRID_W])
        rows.append(jnp.concatenate(pieces, axis=1))
    return jnp.concatenate(rows, axis=0)


def _attn_kernel(*refs, g, kvps, bias_per_head, na_table, mq, wk, back, n_keys, k_off, v_off, has_band, has_sink,
                 scale):
    refs = list(refs)
    q_ref = refs.pop(0)
    if has_band:
        k_ref, v_ref = refs.pop(0), refs.pop(0)
    kc_ref, vc_ref = refs.pop(0), refs.pop(0)
    if has_band:
        b_ref = refs.pop(0)
    if has_sink:
        s_ref = refs.pop(0)
    o_ref = refs.pop(0)

    if has_band:
        i = pl.program_id(1)
        ub = pl.multiple_of(jnp.clip(i * mq - back, 0, n_keys - wk), 64)
    for hq in range(kvps * g):
        kk = hq // g
        ksl = slice(kk * LANE, (kk + 1) * LANE)
        kc = kc_ref[:, ksl].astype(BF16)
        vc = vc_ref[:, ksl].astype(BF16)
        if has_band:
            kb = k_ref[pl.ds(k_off + ub, wk), ksl].astype(BF16)
            vb = v_ref[pl.ds(v_off + ub, wk), ksl].astype(BF16)
            bias = _na_bias_block(b_ref, kk, i) if na_table else b_ref[kk if bias_per_head else 0, 0]
        hh = hq
        sl = slice(hh * LANE, (hh + 1) * LANE)
        q = q_ref[:, sl].astype(BF16)
        s_ctx = _dot_nt(q, kc) * scale
        m = jnp.max(s_ctx, axis=-1, keepdims=True)
        if has_band:
            s_loc = _dot_nt(q, kb) * scale + bias
            m = jnp.maximum(m, jnp.max(s_loc, axis=-1, keepdims=True))
        if has_sink:
            snk = s_ref[:, hh * LANE:hh * LANE + 1]
            m = jnp.maximum(m, snk)
        p_ctx = jnp.exp(s_ctx - m)
        den = jnp.sum(p_ctx, axis=-1, keepdims=True)
        acc = _dot(p_ctx.astype(BF16), vc)
        if has_band:
            p_loc = jnp.exp(s_loc - m)
            den = den + jnp.sum(p_loc, axis=-1, keepdims=True)
            acc = acc + _dot(p_loc.astype(BF16), vb)
        if has_sink:
            den = den + jnp.exp(snk - m)
        o_ref[:, sl] = (acc / den).astype(o_ref.dtype)


def _pattern(i, n):
    return jnp.where(i == 0, 0, jnp.where(i == n - 1, 2, 1))


def _attn_latent(q, q_blk0, q_col0, k, k_col0, k_off, v, v_col0, v_off, kc, kc_col0, vc, vc_col0,
                 bias, sink, *, n_heads, n_kv, kvps, mq, wk, back, name):
    g = n_heads // n_kv
    nq = N_LAT // mq
    per_head_bias = bias.shape[0] > 1
    na_table = bias.shape[1] == 2 * NA_KH
    kw, qw = kvps * LANE, kvps * g * LANE
    if na_table:
        bias_spec = pl.BlockSpec((kvps,) + bias.shape[1:], lambda j, i: (j, 0, 0, 0))
    else:
        bias_spec = pl.BlockSpec((kvps if per_head_bias else 1, 1, mq, wk),
                                 lambda j, i: (j if per_head_bias else 0, _pattern(i, nq), 0, 0))
    in_specs = [
        pl.BlockSpec((mq, qw), lambda j, i: (i + q_blk0, q_col0 // (kvps * g) + j)),
        pl.BlockSpec((k.shape[0], kw), lambda j, i: (0, k_col0 // kvps + j)),
        pl.BlockSpec((v.shape[0], kw), lambda j, i: (0, v_col0 // kvps + j)),
        pl.BlockSpec((N_CTX, kw), lambda j, i: (0, kc_col0 // kvps + j)),
        pl.BlockSpec((N_CTX, kw), lambda j, i: (0, vc_col0 // kvps + j)),
        bias_spec,
    ]
    args = [q, k, v, kc, vc, bias]
    if sink is not None:
        in_specs.append(pl.BlockSpec((1, qw), lambda j, i: (0, j)))
        args.append(sink)
    kern = functools.partial(_attn_kernel, g=g, kvps=kvps, bias_per_head=per_head_bias, na_table=na_table,
                             mq=mq, wk=wk, back=back,
                             n_keys=N_LAT, k_off=k_off, v_off=v_off, has_band=True, has_sink=sink is not None,
                             scale=LANE ** -0.5)
    return pl.pallas_call(
        kern, grid=(n_kv // kvps, nq), in_specs=in_specs,
        out_specs=pl.BlockSpec((mq, qw), lambda j, i: (i, j)),
        out_shape=jax.ShapeDtypeStruct((N_LAT, n_heads * LANE), BF16),
        compiler_params=_cparams(("arbitrary", "arbitrary")), name=name,
    )(*args)


def _attn_context(q, q_col0, kc, kc_col0, vc, vc_col0, sink, *, n_heads, n_kv, name):
    g = n_heads // n_kv
    in_specs = [
        pl.BlockSpec((N_CTX, g * LANE), lambda j, i: (0, q_col0 + j)),
        pl.BlockSpec((N_CTX, LANE), lambda j, i: (0, kc_col0 + j)),
        pl.BlockSpec((N_CTX, LANE), lambda j, i: (0, vc_col0 + j)),
    ]
    args = [q, kc, vc]
    if sink is not None:
        in_specs.append(pl.BlockSpec((1, g * LANE), lambda j, i: (0, j)))
        args.append(sink)
    kern = functools.partial(_attn_kernel, g=g, kvps=1, bias_per_head=False, na_table=False, mq=N_CTX, wk=0, back=0, n_keys=0,
                             k_off=0, v_off=0, has_band=False, has_sink=sink is not None, scale=LANE ** -0.5)
    return pl.pallas_call(
        kern, grid=(n_kv, 1), in_specs=in_specs,
        out_specs=pl.BlockSpec((N_CTX, g * LANE), lambda j, i: (0, j)),
        out_shape=jax.ShapeDtypeStruct((N_CTX, n_heads * LANE), BF16),
        compiler_params=_cparams(("arbitrary", "arbitrary")), name=name,
    )(*args)


def _window_bias():
    t, w = N_LAT, B_WINDOW
    nb = t // w
    out = []
    for bi in (0, 1, nb - 1):
        ub = min(max(bi * w - w, 0), t - 3 * w)
        qpos = bi * w + np.arange(w)[:, None]
        kpos = ub + np.arange(3 * w)[None, :]
        out.append(np.where(np.abs(kpos - qpos) <= w, 0.0, NEG))
    return jnp.asarray(np.stack(out)[None], F32)


def _na_table(rpb):
    col = np.arange(GRID_W)
    c0 = np.clip(col - NA_KW // 2, 0, GRID_W - NA_KW)
    col_ok = (col[None, :] >= c0[:, None]) & (col[None, :] < c0[:, None] + NA_KW)
    dc = np.clip(col[None, :] - col[:, None] + (NA_KW - 1), 0, 2 * NA_KW - 2)
    onehot = (dc[None] == np.arange(2 * NA_KW - 1)[:, None, None]).astype(np.float32)
    t = jnp.einsum("hrd,dck->hrck", rpb.astype(F32), jnp.asarray(onehot), precision=lax.Precision.HIGHEST)
    t = jnp.where(col_ok[None, None], t, NEG)
    t = jnp.concatenate([t, jnp.full((C_HEADS, 1, GRID_W, GRID_W), NEG, F32)], axis=1)
    return jnp.concatenate([t, t], axis=-1)


def _log_sigmoid(x):
    return jnp.minimum(x, 0.0) - jnp.log1p(jnp.exp(-jnp.abs(x)))


def _mlstm_kernel(qk_ref, v_ref, g_ref, h_ref, c_ref, m_ref, *, d):
    L = A_CHUNK
    P2 = 2 * L

    @pl.when(pl.program_id(0) == 0)
    def _():
        c_ref[...] = jnp.zeros_like(c_ref)
        m_ref[...] = jnp.zeros_like(m_ref)

    ri = lax.broadcasted_iota(jnp.int32, (L, P2), 0)
    ci = lax.broadcasted_iota(jnp.int32, (L, P2), 1)
    lane = lax.broadcasted_iota(jnp.int32, (1, P2), 1)
    ones_col = (lax.broadcasted_iota(jnp.int32, (P2, LANE), 1) == 0).astype(F32)
    i_lane, f_lane = 2 * d * A_HEADS, (2 * d + 1) * A_HEADS
    order = (0, 1) if d == 0 else (1, 0)

    def pair(pidx, carry):
        pp = pidx if d == 0 else A_SPAN // P2 - 1 - pidx
        r0 = pl.multiple_of(pp * P2, P2)
        gc = g_ref[pl.ds(r0, P2), :]
        gt = gc.T
        c_state = [c_ref[h] for h in range(A_HEADS)]
        m_state = [m_ref[h, 0:1, 0:1] for h in range(A_HEADS)]
        h_out = []
        for h in range(A_HEADS):
            sl = slice(h * LANE, (h + 1) * LANE)
            qk = qk_ref[pl.ds(r0, P2), sl]
            kt = qk.T[A_DK:, :]
            v_aug = jnp.concatenate([v_ref[pl.ds(r0, P2), sl], ones_col], axis=1).astype(BF16)
            i_row = gt[i_lane + h:i_lane + h + 1, :]
            f_row = _log_sigmoid(gt[f_lane + h:f_lane + h + 1, :])
            for sub in order:
                rows = slice(sub * L, (sub + 1) * L)
                own = (lane >= sub * L) & (lane < (sub + 1) * L)
                cs = ci - sub * L
                seen = ((cs <= ri) & (cs >= 0)) if d == 0 else ((cs >= ri) & (cs < L))
                seen_t = ((ri <= cs) & (cs < L)) if d == 0 else ((ri >= cs) & (cs >= 0))
                q = (qk[rows, :A_DK] * (A_DK ** -0.5)).astype(BF16)
                f_col = _log_sigmoid(gc[rows, f_lane + h:f_lane + h + 1])
                cum_col = jnp.sum(jnp.where(seen, f_row, 0.0), axis=1, keepdims=True)
                cum_row = jnp.sum(jnp.where(seen_t, f_col, 0.0), axis=0, keepdims=True)
                total = jnp.sum(jnp.where(own, f_row, 0.0), axis=1, keepdims=True)
                m_old, c_old = m_state[h], c_state[h]
                dm = jnp.where(seen, cum_col - cum_row + i_row, NEG)
                inter = cum_col + m_old
                mt = jnp.maximum(inter, jnp.max(dm, axis=1, keepdims=True))
                sw = _dot(q, kt.astype(BF16)) * jnp.exp(dm - mt)
                a = jnp.exp(inter - mt)
                na = _dot(sw.astype(BF16), v_aug) + a * _dot(q, c_old.astype(BF16))
                den = jnp.maximum(jnp.abs(na[:, LANE:LANE + 1]), jnp.exp(-mt))
                h_out.append((sub, sl, na[:, :LANE] / den))
                wend = jnp.where(own, total - cum_row + i_row, NEG)
                m_new = jnp.maximum(total + m_old, jnp.max(wend, axis=1, keepdims=True))
                decay = jnp.exp(total + m_old - m_new)
                wv = jnp.exp(wend - m_new)
                c_state[h] = decay * c_old + _dot((kt * wv).astype(BF16), v_aug)
                m_state[h] = m_new
        for sub, sl, val in h_out:
            h_ref[pl.ds(r0 + sub * L, L), sl] = val
        for h in range(A_HEADS):
            c_ref[h] = c_state[h]
            m_ref[h] = jnp.broadcast_to(m_state[h], (8, LANE))
        return carry

    lax.fori_loop(0, A_SPAN // P2, pair, 0)


def _mlstm(p, d):
    n_span = N_TOK // A_SPAN
    if d == 0:
        span = lambda s: s
    else:
        span = lambda s: jnp.where(s == 0, 0, n_span - s)
    return pl.pallas_call(
        functools.partial(_mlstm_kernel, d=d), grid=(n_span,),
        in_specs=[pl.BlockSpec((A_SPAN, 1024), lambda s: (span(s), COL_QK // 1024)),
                  pl.BlockSpec((A_SPAN, 1024), lambda s: (span(s), COL_AV // 1024)),
                  pl.BlockSpec((A_SPAN, LANE), lambda s: (span(s), COL_G // LANE))],
        out_specs=pl.BlockSpec((A_SPAN, 1024), lambda s: (span(s), 0)),
        out_shape=jax.ShapeDtypeStruct((N_TOK, 1024), F32),
        scratch_shapes=[pltpu.VMEM((A_HEADS, A_DK, 2 * LANE), F32), pltpu.VMEM((A_HEADS, 8, LANE), F32)],
        compiler_params=_cparams(("arbitrary",)), name="mlstm_fwd" if d == 0 else "mlstm_bwd",
    )(p, p, p)


def _aout_kernel(hf_ref, hb_ref, o_ref, g_ref, y_ref):
    for h in range(A_HEADS):
        sl = slice(h * LANE, (h + 1) * LANE)
        x = hf_ref[:, sl] + hb_ref[:, sl]
        x = x * lax.rsqrt(jnp.mean(x * x, axis=-1, keepdims=True) + EPS)
        o = o_ref[:, sl]
        y_ref[:, sl] = ((x * g_ref[:, sl]) * (1.0 / (1.0 + jnp.exp(-o)))).astype(y_ref.dtype)


def _aout(hf, hb, p, mh_g):
    tm = 256
    blk = pl.BlockSpec((tm, 1024), lambda i: (i, 0))
    return pl.pallas_call(
        _aout_kernel, grid=(N_TOK // tm,),
        in_specs=[blk, blk, pl.BlockSpec((tm, 1024), lambda i: (i, COL_AO // 1024)),
                  pl.BlockSpec((1, 1024), lambda i: (0, 0))],
        out_specs=blk, out_shape=jax.ShapeDtypeStruct((N_TOK, 1024), BF16),
        compiler_params=_cparams(("parallel",)), name="mlstm_out",
    )(hf, hb, p, mh_g.reshape(1, 1024))


def _top_values(s, k):
    vals = []
    cur = s
    for _ in range(k):
        mx = jnp.max(cur, axis=0, keepdims=True)
        vals.append(mx)
        cur = jnp.where(cur == mx, NEG, cur)
    return vals


def _sort_network(n):
    def merge(lo, hi, r):
        step = r * 2
        if step < hi - lo:
            yield from merge(lo, hi, step)
            yield from merge(lo + r, hi, step)
            yield from ((i, i + r) for i in range(lo + r, hi - r, step))
        else:
            yield (lo, lo + r)

    def sort(lo, hi):
        if hi - lo >= 1:
            mid = lo + (hi - lo) // 2
            yield from sort(lo, mid)
            yield from sort(mid + 1, hi)
            yield from merge(lo, hi, 1)

    return list(sort(0, n - 1))


def _exchange(xs, i, j):
    xs[i], xs[j] = jnp.maximum(xs[i], xs[j]), jnp.minimum(xs[i], xs[j])


def _top16_sorted(s):
    n = P_TOPK
    xs = [s[n_ * 8:(n_ + 1) * 8, :] for n_ in range(n)]
    for i, j in _sort_network(n):
        _exchange(xs, i, j)
    for shift in (4, 2, 1):
        other = [pltpu.roll(x, shift, axis=0) for x in xs]
        xs = [jnp.maximum(xs[k], other[n - 1 - k]) for k in range(n)]
        for dist in (8, 4, 2, 1):
            for i in range(n):
                if i & dist == 0:
                    _exchange(xs, i, i + dist)
    return xs


def _router_kernel(ht_ref, wqt_ref, keys_ref, s_ref, e_ref, thr_ref, *, tn):
    ht = ht_ref[...]
    row8 = lax.broadcasted_iota(jnp.int32, (8, tn), 0)
    for h in range(P_HEADS):
        tops = []
        for p in range(2):
            hp = 2 * h + p
            q = _dot(wqt_ref[hp * LANE:(hp + 1) * LANE, :], ht).astype(BF16)
            s = _dot(keys_ref[hp], q)
            s_ref[hp] = s
            tops.append(_top16_sorted(s))
        ta, tb = tops
        a_lo, a_hi, b_hi = (jnp.full((8, tn), NEG, F32) for _ in range(3))
        for i in range(8):
            a_lo = jnp.where(row8 == i, ta[i], a_lo)
            a_hi = jnp.where(row8 == i, ta[i + 8], a_hi)
            b_hi = jnp.where(row8 == i, tb[i + 8], b_hi)
        parts = [a_lo + tb[0], a_hi + tb[0], a_lo + tb[1]]
        for j in range(2, 8):
            parts.append(jnp.where(row8 < P_TOPK // (j + 1), a_lo + tb[j], NEG))
        parts.append(ta[0] + b_hi)
        cand = jnp.concatenate(parts, axis=0)
        best = _top_values(cand, P_TOPK)
        z = jnp.zeros_like(best[0])
        for c in best:
            z = z + jnp.exp(c - best[0])
        thr_ref[h:h + 1, :] = best[P_TOPK - 1]
        e_ref[2 * h] = jnp.exp(s_ref[2 * h] - ta[0][0:1, :]) / z
        e_ref[2 * h + 1] = jnp.exp(s_ref[2 * h + 1] - tb[0][0:1, :])


def _router(ht, wqt, keys):
    tn = 256
    t = ht.shape[1]
    big = pl.BlockSpec((2 * P_HEADS, P_NKEYS, tn), lambda i: (0, 0, i))
    shp = jax.ShapeDtypeStruct((2 * P_HEADS, P_NKEYS, t), F32)
    return pl.pallas_call(
        functools.partial(_router_kernel, tn=tn), grid=(t // tn,),
        in_specs=[pl.BlockSpec((D, tn), lambda i: (0, i)),
                  pl.BlockSpec((D, D), lambda i: (0, 0)),
                  pl.BlockSpec((2 * P_HEADS, P_NKEYS, LANE), lambda i: (0, 0, 0))],
        out_specs=[big, big, pl.BlockSpec((P_HEADS, tn), lambda i: (0, i))],
        out_shape=[shp, shp, jax.ShapeDtypeStruct((P_HEADS, t), F32)],
        compiler_params=_cparams(("parallel",)), name="peer_router",
    )(ht, wqt, keys)


P_EC = 1024
P_TN = 768


def _gelu(x):
    return 0.5 * x * (1.0 + lax.erf(x * (2.0 ** -0.5)))


P_HALF = P_EC // 2
P_STEPS = P_EXPERTS // P_EC + 1
P_GATE_ROWS = 32


def _expert_gates(z_ref, w_ref, s1_ref, e1_ref, row0, s2_ref, e2_ref, thr_ref):
    n_r = P_HALF // P_NKEYS
    for lb in range(P_TN // LANE):
        sl = slice(lb * LANE, (lb + 1) * LANE)
        for b0 in range(0, P_NKEYS, P_GATE_ROWS):
            gates = [jnp.zeros((P_GATE_ROWS, LANE), F32) for _ in range(n_r)]
            for h in range(P_HEADS):
                s2 = s2_ref[h, 0, b0:b0 + P_GATE_ROWS, sl]
                e2 = e2_ref[h, 0, b0:b0 + P_GATE_ROWS, sl]
                thr = thr_ref[h:h + 1, sl]
                for r in range(n_r):
                    pair = s1_ref[h, 0, row0 + r:row0 + r + 1, sl] + s2
                    w = e1_ref[h, 0, row0 + r:row0 + r + 1, sl] * e2
                    gates[r] = gates[r] + jnp.where(pair >= thr, w, 0.0)
            for r in range(n_r):
                rows = slice(r * P_NKEYS + b0, r * P_NKEYS + b0 + P_GATE_ROWS)
                w_ref[lb, rows, :] = (gates[r] * _gelu(z_ref[lb, rows, :])).astype(BF16)


def _expert_kernel(ht_ref, u_ref, vt_ref, s1p_ref, s1c_ref, s2_ref, e1p_ref, e1c_ref, e2_ref, thr_ref, y_ref,
                   z0, z1, w0, w1):
    j = pl.program_id(1)
    last = P_STEPS - 1
    half_rows = P_HALF // P_NKEYS

    n_lb = P_TN // LANE

    def stage_a(z, half):
        zf = _dot(u_ref[half * P_HALF:(half + 1) * P_HALF, :], ht_ref[...])
        for lb in range(n_lb):
            z[lb] = zf[:, lb * LANE:(lb + 1) * LANE]

    def stage_b(w, half, y_old):
        wf = jnp.concatenate([w[lb] for lb in range(n_lb)], axis=1)
        y_ref[...] = y_old + _dot(vt_ref[:, half * P_HALF:(half + 1) * P_HALF], wf)

    @pl.when(j == 0)
    def _():
        stage_a(z0, 0)
        stage_a(z1, 1)
        _expert_gates(z0, w0, s1c_ref, e1c_ref, 0, s2_ref, e2_ref, thr_ref)

    @pl.when((j > 0) & (j < last))
    def _():
        stage_a(z0, 0)
        _expert_gates(z1, w1, s1p_ref, e1p_ref, half_rows, s2_ref, e2_ref, thr_ref)
        stage_b(w0, 0, jnp.where(j == 1, 0.0, y_ref[...]))
        stage_a(z1, 1)
        _expert_gates(z0, w0, s1c_ref, e1c_ref, 0, s2_ref, e2_ref, thr_ref)
        stage_b(w1, 1, y_ref[...])

    @pl.when(j == last)
    def _():
        _expert_gates(z1, w1, s1p_ref, e1p_ref, half_rows, s2_ref, e2_ref, thr_ref)
        stage_b(w0, 0, y_ref[...])
        stage_b(w1, 1, y_ref[...])


def _experts(ht, u, vt, s, e, thr):
    t = ht.shape[1]
    ac = P_EC // P_NKEYS
    n_blk = P_EXPERTS // P_EC
    s4 = s.reshape(P_HEADS, 2, P_NKEYS, t)
    e4 = e.reshape(P_HEADS, 2, P_NKEYS, t)
    once = pl.Buffered(1)
    cur = lambda j: jnp.minimum(j, n_blk - 1)
    prev = lambda j: jnp.maximum(j - 1, 0)
    first_p = pl.BlockSpec((P_HEADS, 1, ac, P_TN), lambda i, j: (0, 0, prev(j), i))
    first_c = pl.BlockSpec((P_HEADS, 1, ac, P_TN), lambda i, j: (0, 0, cur(j), i))
    second = pl.BlockSpec((P_HEADS, 1, P_NKEYS, P_TN), lambda i, j: (0, 1, 0, i), pipeline_mode=once)
    return pl.pallas_call(
        _expert_kernel, grid=(t // P_TN, P_STEPS),
        in_specs=[pl.BlockSpec((D, P_TN), lambda i, j: (0, i), pipeline_mode=once),
                  pl.BlockSpec((P_EC, D), lambda i, j: (cur(j), 0)),
                  pl.BlockSpec((D, P_EC), lambda i, j: (0, prev(j))),
                  first_p, first_c, second, first_p, first_c, second,
                  pl.BlockSpec((P_HEADS, P_TN), lambda i, j: (0, i), pipeline_mode=once)],
        out_specs=pl.BlockSpec((D, P_TN), lambda i, j: (0, i)),
        out_shape=jax.ShapeDtypeStruct((D, t), F32),
        scratch_shapes=[pltpu.VMEM((P_TN // LANE, P_HALF, LANE), F32), pltpu.VMEM((P_TN // LANE, P_HALF, LANE), F32),
                        pltpu.VMEM((P_TN // LANE, P_HALF, LANE), BF16), pltpu.VMEM((P_TN // LANE, P_HALF, LANE), BF16)],
        compiler_params=_cparams(("parallel", "arbitrary")), name="peer_experts",
    )(ht, u, vt, s4, s4, s4, e4, e4, e4, thr)


def _peer(ht, wq, keys, u, v):
    s, e, thr = _router(ht, wq.T.astype(BF16), keys.reshape(2 * P_HEADS, P_NKEYS, LANE).astype(BF16))
    return _experts(ht, u.astype(BF16), v.T.astype(BF16), s, e, thr)


def _ab_weights(w_in, gate_b):
    aq, ak, av, ao, ag, bq, bk, bv = jnp.split(w_in, np.cumsum([512, 512, 1024, 1024, 32, 1024, 256])[:].tolist(), axis=1)
    qk_t = jnp.concatenate([aq, ak], axis=1).astype(BF16).T.reshape(2, A_HEADS, A_DK, D)
    qk = qk_t.transpose(1, 0, 2, 3).reshape(2 * A_HEADS * A_DK, D).T
    pad = jnp.zeros((D, AB_N - COL_G - 32), BF16)
    w = jnp.concatenate([qk] + [a.astype(BF16) for a in (av, ao, bq, bk, bv, ag)] + [pad], axis=1)
    bias = jnp.zeros((1, AB_N), F32).at[0, COL_G:COL_G + 32].set(gate_b.astype(F32))
    return w, bias


def _mixer_ab(h, w_in, gate_b, mh_g, sink, rope_tabs, win_bias):
    w, bias = _ab_weights(w_in, gate_b)
    p = _matmul(h, w, bias=bias)
    hf = _mlstm(p, 0)
    hb = _mlstm(p, 1)
    ya = _aout(hf, hb, p, mh_g)
    qr, kr = _rope(p, *rope_tabs)
    sink_b = jnp.repeat(sink.astype(F32), LANE).reshape(1, B_HEADS * LANE)
    yb_l = _attn_latent(qr, 0, 0, kr, 0, 0, p, COL_BV // LANE, N_CTX, p, COL_BK // LANE, p, COL_BV // LANE,
                        win_bias, sink_b, n_heads=B_HEADS, n_kv=B_KV, kvps=1, mq=B_WINDOW, wk=3 * B_WINDOW,
                        back=B_WINDOW, name="window_attn")
    yb_c = _attn_context(p, COL_BQ // (4 * LANE), p, COL_BK // LANE, p, COL_BV // LANE, sink_b,
                         n_heads=B_HEADS, n_kv=B_KV, name="window_attn_ctx")
    return jnp.concatenate([ya, jnp.concatenate([yb_c, yb_l], axis=0)], axis=1)


def _mixer_c(h, w_in, rpb):
    p = _matmul(h, w_in.astype(BF16), out_dtype=BF16)
    bias = _na_table(rpb)
    mq = NA_ROWS * GRID_W
    y_l = _attn_latent(p, N_CTX // mq, 0, p, C_HEADS, N_CTX, p, 2 * C_HEADS, N_CTX, p, C_HEADS, p, 2 * C_HEADS,
                       bias, None, n_heads=C_HEADS, n_kv=C_HEADS, kvps=NA_HEADS_PER_STEP, mq=mq, wk=NA_BAND * GRID_W,
                       back=(NA_KH // 2) * GRID_W, name="na_attn")
    y_c = _attn_context(p, 0, p, C_HEADS, p, 2 * C_HEADS, None, n_heads=C_HEADS, n_kv=C_HEADS, name="na_attn_ctx")
    return jnp.concatenate([y_c, y_l], axis=0)


def _mod_rows(m6, i_shift, i_scale, i_gate):
    z = jnp.zeros((D,), F32)
    pick = lambda r, i: m6[r, i] if i is not None else z
    return jnp.stack([pick(0, i_shift), pick(0, i_scale), pick(1, i_shift), pick(1, i_scale),
                      pick(0, i_gate), pick(1, i_gate), z, z])


def kernel(x, c, ctx, c_ctx, ada_w, ada_b, norm1_g, norm2_g, ab_w_in, ab_gate_b, ab_mh_g, ab_sink, ab_w_out,
           na_w_in, na_rpb, na_w_out, peer_wq, peer_keys, peer_u, peer_v, final_g):
    xs = jnp.concatenate([ctx[0], x[0]], axis=0).astype(F32)
    cc = jnp.zeros((16, D), F32).at[0].set(c[0]).at[1].set(c_ctx)
    mods = _adaln(cc, ada_w, ada_b)[:, :2].reshape(DEPTH, 2, 6, D)
    rope_tabs = _rope_tables()
    win_bias = _window_bias()

    h = _norm(xs, norm1_g[0], _mod_rows(mods[0], 0, 1, None))
    for l in range(DEPTH):
        m6 = mods[l]
        if l % 2 == 0:
            e = l // 2
            ymix = _mixer_ab(h, ab_w_in[e], ab_gate_b[e], ab_mh_g[e], ab_sink[e], rope_tabs, win_bias)
            w_out = ab_w_out[e]
        else:
            o = l // 2
            ymix = _mixer_c(h, na_w_in[o], na_rpb[o])
            w_out = na_w_out[o]
        gv = jnp.concatenate([m6[:, 2], jnp.zeros((6, D), F32)], axis=0)
        xs = _matmul(ymix, w_out.astype(BF16), resid=xs, gates=gv)
        h2t = _norm(xs, norm2_g[l], _mod_rows(m6, 3, 4, None), h_t=True)
        yt = _peer(h2t, peer_wq[l], peer_keys[l], peer_u[l], peer_v[l])
        if l + 1 < DEPTH:
            mv = _mod_rows(mods[l + 1], 0, 1, None).at[4].set(m6[0, 5]).at[5].set(m6[1, 5])
            xs, h = _norm(xs, norm1_g[l + 1], mv, y=yt, y_t=True)
        else:
            mv = _mod_rows(m6, None, None, 5)
            _, out = _norm(xs, final_g, mv, y=yt, y_t=True, out_dtype=F32, row_off=N_CTX // 256)
    return out[None]
```

```python
import functools

import numpy as np
import jax
import jax.numpy as jnp
from jax import lax
from jax.experimental import pallas as pl
from jax.experimental.pallas import tpu as pltpu

F32 = jnp.float32
BF16 = jnp.bfloat16

D = 2048
N_LAT = 8192
N_CTX = 256
N_TOK = N_CTX + N_LAT
DEPTH = 4
GRID_W = 64
GRID_H = N_LAT // GRID_W
EPS = 1e-6
LANE = 128
NEG = -1e30

A_HEADS = 8
A_DK = 64
A_DV = 128
A_CHUNK = 64
A_SPAN = 256
B_HEADS = 8
B_KV = 2
B_WINDOW = 128
ROPE_THETA = 10000.0
C_HEADS = 16
NA_KH = 8
NA_KW = 16
NA_ROWS = 4
NA_BAND = NA_ROWS + NA_KH - 1
NA_HEADS_PER_STEP = 2
P_HEADS = 8
P_NKEYS = 128
P_TOPK = 16
P_EXPERTS = P_NKEYS * P_NKEYS

AB_N = 5120
COL_QK, COL_AV, COL_AO, COL_BQ, COL_BK, COL_BV, COL_G = 0, 1024, 2048, 3072, 4096, 4352, 4608

VMEM_LIMIT = 52 * 1024 * 1024


def _cparams(sem):
    return pltpu.CompilerParams(dimension_semantics=sem, vmem_limit_bytes=VMEM_LIMIT)


def _dot(a, b):
    return jnp.dot(a, b, preferred_element_type=F32)


def _dot_nt(a, b):
    return lax.dot_general(a, b, (((1,), (1,)), ((), ())), preferred_element_type=F32)


def _ada_kernel(c_ref, w_ref, b_ref, o_ref):
    c = c_ref[...]
    s = c / (1.0 + jnp.exp(-c))
    w = w_ref[0]
    s_hi = s.astype(BF16)
    s_lo = (s - s_hi.astype(F32)).astype(BF16)
    w_hi = w.astype(BF16)
    w_lo = (w - w_hi.astype(F32)).astype(BF16)
    o_ref[0] = _dot(s_hi, w_hi) + _dot(s_lo, w_hi) + _dot(s_hi, w_lo) + b_ref[0]


def _adaln(cc, ada_w, ada_b):
    tn = 1024
    n = ada_w.shape[-1]
    return pl.pallas_call(
        _ada_kernel,
        grid=(DEPTH, n // tn),
        in_specs=[pl.BlockSpec((16, D), lambda l, j: (0, 0)),
                  pl.BlockSpec((1, D, tn), lambda l, j: (l, 0, j)),
                  pl.BlockSpec((1, 1, tn), lambda l, j: (l, 0, j))],
        out_specs=pl.BlockSpec((1, 16, tn), lambda l, j: (l, 0, j)),
        out_shape=jax.ShapeDtypeStruct((DEPTH, 16, n), F32),
        compiler_params=_cparams(("arbitrary", "arbitrary")),
        name="adaln",
    )(cc, ada_w, ada_b.reshape(DEPTH, 1, n))


def _norm_kernel(*refs, has_resid, y_t, h_t, n_ctx, tm, row_off):
    if has_resid:
        x_ref, y_ref, g_ref, mv_ref, xo_ref, h_ref = refs
    else:
        x_ref, g_ref, mv_ref, h_ref = refs
    row = (pl.program_id(0) + row_off) * tm + lax.broadcasted_iota(jnp.int32, (tm, 1), 0)
    is_ctx = row < n_ctx
    x = x_ref[...]
    if has_resid:
        gate = jnp.where(is_ctx, mv_ref[5:6, :], mv_ref[4:5, :])
        y = y_ref[...].T if y_t else y_ref[...]
        x = x + gate * y
        xo_ref[...] = x
    ms = jnp.mean(x * x, axis=-1, keepdims=True)
    yn = (x * lax.rsqrt(ms + EPS)) * g_ref[...]
    shift = jnp.where(is_ctx, mv_ref[2:3, :], mv_ref[0:1, :])
    scale = jnp.where(is_ctx, mv_ref[3:4, :], mv_ref[1:2, :])
    h = yn * (1.0 + scale) + shift
    h_ref[...] = (h.T if h_t else h).astype(h_ref.dtype)


def _norm(x, g, mv, *, y=None, y_t=False, h_t=False, out_dtype=BF16, row_off=0):
    tm = 256
    n_rows = x.shape[0] - row_off * tm
    blk = pl.BlockSpec((tm, D), lambda i: (i + row_off, 0))
    oblk = pl.BlockSpec((tm, D), lambda i: (i, 0))
    yblk = pl.BlockSpec((D, tm), lambda i: (0, i + row_off)) if y_t else blk
    hblk = pl.BlockSpec((D, tm), lambda i: (0, i)) if h_t else oblk
    vec = pl.BlockSpec((1, D), lambda i: (0, 0))
    mvs = pl.BlockSpec((8, D), lambda i: (0, 0))
    has_resid = y is not None
    kern = functools.partial(_norm_kernel, has_resid=has_resid, y_t=y_t, h_t=h_t, n_ctx=N_CTX, tm=tm, row_off=row_off)
    h_shape = jax.ShapeDtypeStruct((D, n_rows) if h_t else (n_rows, D), out_dtype)
    if has_resid:
        return pl.pallas_call(
            kern, grid=(n_rows // tm,),
            in_specs=[blk, yblk, vec, mvs], out_specs=[oblk, hblk],
            out_shape=[jax.ShapeDtypeStruct((n_rows, D), F32), h_shape],
            compiler_params=_cparams(("parallel",)), name="resid_norm",
        )(x, y, g.reshape(1, D), mv)
    return pl.pallas_call(
        kern, grid=(n_rows // tm,),
        in_specs=[blk, vec, mvs], out_specs=hblk, out_shape=h_shape,
        compiler_params=_cparams(("parallel",)), name="norm",
    )(x, g.reshape(1, D), mv)


MM_TM, MM_TN = 768, 1024


def _mm_kernel(*refs, has_bias, has_resid, n_ctx, tm):
    a_ref, w_ref = refs[0], refs[1]
    o_ref = refs[-1]
    acc = _dot(a_ref[...], w_ref[...])
    k = 2
    if has_bias:
        acc = acc + refs[k][...]
        k += 1
    if has_resid:
        x_ref, gv_ref = refs[k], refs[k + 1]
        row = pl.program_id(0) * tm + lax.broadcasted_iota(jnp.int32, (tm, 1), 0)
        gate = jnp.where(row < n_ctx, gv_ref[1:2, :], gv_ref[0:1, :])
        acc = x_ref[...] + gate * acc
    o_ref[...] = acc.astype(o_ref.dtype)


def _matmul(a, w, *, bias=None, resid=None, gates=None, out_dtype=F32, tm=MM_TM, tn=MM_TN):
    m, k = a.shape
    n = w.shape[1]
    in_specs = [pl.BlockSpec((tm, k), lambda i, j: (i, 0)), pl.BlockSpec((k, tn), lambda i, j: (0, j))]
    args = [a, w]
    if bias is not None:
        in_specs.append(pl.BlockSpec((1, tn), lambda i, j: (0, j)))
        args.append(bias)
    if resid is not None:
        in_specs += [pl.BlockSpec((tm, tn), lambda i, j: (i, j)), pl.BlockSpec((8, tn), lambda i, j: (0, j))]
        args += [resid, gates]
    kern = functools.partial(_mm_kernel, has_bias=bias is not None, has_resid=resid is not None, n_ctx=N_CTX, tm=tm)
    return pl.pallas_call(
        kern, grid=(m // tm, n // tn), in_specs=in_specs,
        out_specs=pl.BlockSpec((tm, tn), lambda i, j: (i, j)),
        out_shape=jax.ShapeDtypeStruct((m, n), out_dtype),
        compiler_params=_cparams(("parallel", "arbitrary")), name="matmul",
    )(*args)


def _rope_tile(x, cos, sin):
    lane = lax.broadcasted_iota(jnp.int32, x.shape, 1)
    partner = jnp.where((lane % 64) < 32, pltpu.roll(x, 96, axis=1), pltpu.roll(x, 32, axis=1))
    return x * cos + partner * sin


def _rope_kernel(q_ref, k_ref, cos_ref, sin_ref, qo_ref, ko_ref):
    cos, sin = cos_ref[...], sin_ref[...]
    for h in range(B_HEADS):
        sl = slice(h * LANE, (h + 1) * LANE)
        qo_ref[:, sl] = _rope_tile(q_ref[:, sl], cos, sin).astype(qo_ref.dtype)
    for h in range(B_KV):
        sl = slice(h * LANE, (h + 1) * LANE)
        ko_ref[:, sl] = _rope_tile(k_ref[:, sl], cos, sin).astype(ko_ref.dtype)


def _rope_tables():
    t = jnp.arange(N_LAT)
    freqs = ROPE_THETA ** (-jnp.arange(32, dtype=F32) / 32)
    ar = (t // GRID_W).astype(F32)[:, None] * freqs[None, :]
    ac = (t % GRID_W).astype(F32)[:, None] * freqs[None, :]
    cos = jnp.concatenate([jnp.cos(ar), jnp.cos(ar), jnp.cos(ac), jnp.cos(ac)], axis=1)
    sin = jnp.concatenate([-jnp.sin(ar), jnp.sin(ar), -jnp.sin(ac), jnp.sin(ac)], axis=1)
    return cos, sin


def _rope(p, cos, sin):
    tr = 256
    off = N_CTX // tr
    return pl.pallas_call(
        _rope_kernel, grid=(N_LAT // tr,),
        in_specs=[pl.BlockSpec((tr, 1024), lambda i: (i + off, COL_BQ // 1024)),
                  pl.BlockSpec((tr, 256), lambda i: (i + off, COL_BK // 256)),
                  pl.BlockSpec((tr, LANE), lambda i: (i, 0)),
                  pl.BlockSpec((tr, LANE), lambda i: (i, 0))],
        out_specs=[pl.BlockSpec((tr, 1024), lambda i: (i, 0)), pl.BlockSpec((tr, 256), lambda i: (i, 0))],
        out_shape=[jax.ShapeDtypeStruct((N_LAT, 1024), BF16), jax.ShapeDtypeStruct((N_LAT, 256), BF16)],
        compiler_params=_cparams(("parallel",)), name="rope",
    )(p, p, cos, sin)


def _na_bias_block(b_ref, kk, i):
    lane = lax.broadcasted_iota(jnp.int32, (GRID_W, LANE), 1)
    r0 = i * NA_ROWS
    ub = jnp.clip(r0 - NA_KH // 2, 0, GRID_H - NA_BAND)
    rows = []
    for ri in range(NA_ROWS):
        r = r0 + ri
        rs = jnp.clip(r - NA_KH // 2, 0, GRID_H - NA_KH)
        idx = []
        for a in range(NA_BAND):
            krow = ub + a
            ok = (krow >= rs) & (krow < rs + NA_KH)
            idx.append(jnp.where(ok, krow - r + (NA_KH - 1), 2 * NA_KH - 1))
        pieces = [jnp.where(lane < GRID_W, b_ref[kk, idx[a]], b_ref[kk, idx[a + 1]]) for a in range(0, NA_BAND - 1, 2)]
        pieces.append(b_ref[kk, idx[NA_BAND - 1]][:, :GRID_W])
        rows.append(jnp.concatenate(pieces, axis=1))
    return jnp.concatenate(rows, axis=0)


def _attn_kernel(*refs, g, kvps, bias_per_head, na_table, mq, wk, back, n_keys, k_off, v_off, has_band, has_sink,
                 scale):
    refs = list(refs)
    q_ref = refs.pop(0)
    if has_band:
        k_ref, v_ref = refs.pop(0), refs.pop(0)
    kc_ref, vc_ref = refs.pop(0), refs.pop(0)
    if has_band:
        b_ref = refs.pop(0)
    if has_sink:
        s_ref = refs.pop(0)
    o_ref = refs.pop(0)

    if has_band:
        i = pl.program_id(1)
        ub = pl.multiple_of(jnp.clip(i * mq - back, 0, n_keys - wk), 64)
    for hq in range(kvps * g):
        kk = hq // g
        ksl = slice(kk * LANE, (kk + 1) * LANE)
        kc = kc_ref[:, ksl].astype(BF16)
        vc = vc_ref[:, ksl].astype(BF16)
        if has_band:
            kb = k_ref[pl.ds(k_off + ub, wk), ksl].astype(BF16)
            vb = v_ref[pl.ds(v_off + ub, wk), ksl].astype(BF16)
            bias = _na_bias_block(b_ref, kk, i) if na_table else b_ref[kk if bias_per_head else 0, 0]
        hh = hq
        sl = slice(hh * LANE, (hh + 1) * LANE)
        q = q_ref[:, sl].astype(BF16)
        s_ctx = _dot_nt(q, kc) * scale
        m = jnp.max(s_ctx, axis=-1, keepdims=True)
        if has_band:
            s_loc = _dot_nt(q, kb) * scale + bias
            m = jnp.maximum(m, jnp.max(s_loc, axis=-1, keepdims=True))
        if has_sink:
            snk = s_ref[:, hh * LANE:hh * LANE + 1]
            m = jnp.maximum(m, snk)
        p_ctx = jnp.exp(s_ctx - m)
        den = jnp.sum(p_ctx, axis=-1, keepdims=True)
        acc = _dot(p_ctx.astype(BF16), vc)
        if has_band:
            p_loc = jnp.exp(s_loc - m)
            den = den + jnp.sum(p_loc, axis=-1, keepdims=True)
            acc = acc + _dot(p_loc.astype(BF16), vb)
        if has_sink:
            den = den + jnp.exp(snk - m)
        o_ref[:, sl] = (acc / den).astype(o_ref.dtype)


def _pattern(i, n):
    return jnp.where(i == 0, 0, jnp.where(i == n - 1, 2, 1))


def _attn_latent(q, q_blk0, q_col0, k, k_col0, k_off, v, v_col0, v_off, kc, kc_col0, vc, vc_col0,
                 bias, sink, *, n_heads, n_kv, kvps, mq, wk, back, name):
    g = n_heads // n_kv
    nq = N_LAT // mq
    per_head_bias = bias.shape[0] > 1
    na_table = bias.shape[1] == 2 * NA_KH
    kw, qw = kvps * LANE, kvps * g * LANE
    if na_table:
        bias_spec = pl.BlockSpec((kvps,) + bias.shape[1:], lambda j, i: (j, 0, 0, 0))
    else:
        bias_spec = pl.BlockSpec((kvps if per_head_bias else 1, 1, mq, wk),
                                 lambda j, i: (j if per_head_bias else 0, _pattern(i, nq), 0, 0))
    in_specs = [
        pl.BlockSpec((mq, qw), lambda j, i: (i + q_blk0, q_col0 // (kvps * g) + j)),
        pl.BlockSpec((k.shape[0], kw), lambda j, i: (0, k_col0 // kvps + j)),
        pl.BlockSpec((v.shape[0], kw), lambda j, i: (0, v_col0 // kvps + j)),
        pl.BlockSpec((N_CTX, kw), lambda j, i: (0, kc_col0 // kvps + j)),
        pl.BlockSpec((N_CTX, kw), lambda j, i: (0, vc_col0 // kvps + j)),
        bias_spec,
    ]
    args = [q, k, v, kc, vc, bias]
    if sink is not None:
        in_specs.append(pl.BlockSpec((1, qw), lambda j, i: (0, j)))
        args.append(sink)
    kern = functools.partial(_attn_kernel, g=g, kvps=kvps, bias_per_head=per_head_bias, na_table=na_table,
                             mq=mq, wk=wk, back=back,
                             n_keys=N_LAT, k_off=k_off, v_off=v_off, has_band=True, has_sink=sink is not None,
                             scale=LANE ** -0.5)
    return pl.pallas_call(
        kern, grid=(n_kv // kvps, nq), in_specs=in_specs,
        out_specs=pl.BlockSpec((mq, qw), lambda j, i: (i, j)),
        out_shape=jax.ShapeDtypeStruct((N_LAT, n_heads * LANE), BF16),
        compiler_params=_cparams(("arbitrary", "arbitrary")), name=name,
    )(*args)


def _attn_context(q, q_col0, kc, kc_col0, vc, vc_col0, sink, *, n_heads, n_kv, name):
    g = n_heads // n_kv
    in_specs = [
        pl.BlockSpec((N_CTX, g * LANE), lambda j, i: (0, q_col0 + j)),
        pl.BlockSpec((N_CTX, LANE), lambda j, i: (0, kc_col0 + j)),
        pl.BlockSpec((N_CTX, LANE), lambda j, i: (0, vc_col0 + j)),
    ]
    args = [q, kc, vc]
    if sink is not None:
        in_specs.append(pl.BlockSpec((1, g * LANE), lambda j, i: (0, j)))
        args.append(sink)
    kern = functools.partial(_attn_kernel, g=g, kvps=1, bias_per_head=False, na_table=False, mq=N_CTX, wk=0, back=0, n_keys=0,
                             k_off=0, v_off=0, has_band=False, has_sink=sink is not None, scale=LANE ** -0.5)
    return pl.pallas_call(
        kern, grid=(n_kv, 1), in_specs=in_specs,
        out_specs=pl.BlockSpec((N_CTX, g * LANE), lambda j, i: (0, j)),
        out_shape=jax.ShapeDtypeStruct((N_CTX, n_heads * LANE), BF16),
        compiler_params=_cparams(("arbitrary", "arbitrary")), name=name,
    )(*args)


def _window_bias():
    t, w = N_LAT, B_WINDOW
    nb = t // w
    out = []
    for bi in (0, 1, nb - 1):
        ub = min(max(bi * w - w, 0), t - 3 * w)
        qpos = bi * w + np.arange(w)[:, None]
        kpos = ub + np.arange(3 * w)[None, :]
        out.append(np.where(np.abs(kpos - qpos) <= w, 0.0, NEG))
    return jnp.asarray(np.stack(out)[None], F32)


def _na_table(rpb):
    col = np.arange(GRID_W)
    c0 = np.clip(col - NA_KW // 2, 0, GRID_W - NA_KW)
    col_ok = (col[None, :] >= c0[:, None]) & (col[None, :] < c0[:, None] + NA_KW)
    dc = np.clip(col[None, :] - col[:, None] + (NA_KW - 1), 0, 2 * NA_KW - 2)
    onehot = (dc[None] == np.arange(2 * NA_KW - 1)[:, None, None]).astype(np.float32)
    t = jnp.einsum("hrd,dck->hrck", rpb.astype(F32), jnp.asarray(onehot), precision=lax.Precision.HIGHEST)
    t = jnp.where(col_ok[None, None], t, NEG)
    t = jnp.concatenate([t, jnp.full((C_HEADS, 1, GRID_W, GRID_W), NEG, F32)], axis=1)
    return jnp.concatenate([t, t], axis=-1)


def _log_sigmoid(x):
    return jnp.minimum(x, 0.0) - jnp.log1p(jnp.exp(-jnp.abs(x)))


def _mlstm_kernel(qk_ref, v_ref, g_ref, h_ref, c_ref, m_ref, *, d):
    L = A_CHUNK
    P2 = 2 * L

    @pl.when(pl.program_id(0) == 0)
    def _():
        c_ref[...] = jnp.zeros_like(c_ref)
        m_ref[...] = jnp.zeros_like(m_ref)

    ri = lax.broadcasted_iota(jnp.int32, (L, P2), 0)
    ci = lax.broadcasted_iota(jnp.int32, (L, P2), 1)
    lane = lax.broadcasted_iota(jnp.int32, (1, P2), 1)
    ones_col = (lax.broadcasted_iota(jnp.int32, (P2, LANE), 1) == 0).astype(F32)
    i_lane, f_lane = 2 * d * A_HEADS, (2 * d + 1) * A_HEADS
    order = (0, 1) if d == 0 else (1, 0)

    def pair(pidx, carry):
        pp = pidx if d == 0 else A_SPAN // P2 - 1 - pidx
        r0 = pl.multiple_of(pp * P2, P2)
        gc = g_ref[pl.ds(r0, P2), :]
        gt = gc.T
        c_state = [c_ref[h] for h in range(A_HEADS)]
        m_state = [m_ref[h, 0:1, 0:1] for h in range(A_HEADS)]
        h_out = []
        for h in range(A_HEADS):
            sl = slice(h * LANE, (h + 1) * LANE)
            qk = qk_ref[pl.ds(r0, P2), sl]
            kt = qk.T[A_DK:, :]
            v_aug = jnp.concatenate([v_ref[pl.ds(r0, P2), sl], ones_col], axis=1).astype(BF16)
            i_row = gt[i_lane + h:i_lane + h + 1, :]
            f_row = _log_sigmoid(gt[f_lane + h:f_lane + h + 1, :])
            for sub in order:
                rows = slice(sub * L, (sub + 1) * L)
                own = (lane >= sub * L) & (lane < (sub + 1) * L)
                cs = ci - sub * L
                seen = ((cs <= ri) & (cs >= 0)) if d == 0 else ((cs >= ri) & (cs < L))
                seen_t = ((ri <= cs) & (cs < L)) if d == 0 else ((ri >= cs) & (cs >= 0))
                q = (qk[rows, :A_DK] * (A_DK ** -0.5)).astype(BF16)
                f_col = _log_sigmoid(gc[rows, f_lane + h:f_lane + h + 1])
                cum_col = jnp.sum(jnp.where(seen, f_row, 0.0), axis=1, keepdims=True)
                cum_row = jnp.sum(jnp.where(seen_t, f_col, 0.0), axis=0, keepdims=True)
                total = jnp.sum(jnp.where(own, f_row, 0.0), axis=1, keepdims=True)
                m_old, c_old = m_state[h], c_state[h]
                dm = jnp.where(seen, cum_col - cum_row + i_row, NEG)
                inter = cum_col + m_old
                mt = jnp.maximum(inter, jnp.max(dm, axis=1, keepdims=True))
                sw = _dot(q, kt.astype(BF16)) * jnp.exp(dm - mt)
                a = jnp.exp(inter - mt)
                na = _dot(sw.astype(BF16), v_aug) + a * _dot(q, c_old.astype(BF16))
                den = jnp.maximum(jnp.abs(na[:, LANE:LANE + 1]), jnp.exp(-mt))
                h_out.append((sub, sl, na[:, :LANE] / den))
                wend = jnp.where(own, total - cum_row + i_row, NEG)
                m_new = jnp.maximum(total + m_old, jnp.max(wend, axis=1, keepdims=True))
                decay = jnp.exp(total + m_old - m_new)
                wv = jnp.exp(wend - m_new)
                c_state[h] = decay * c_old + _dot((kt * wv).astype(BF16), v_aug)
                m_state[h] = m_new
        for sub, sl, val in h_out:
            h_ref[pl.ds(r0 + sub * L, L), sl] = val
        for h in range(A_HEADS):
            c_ref[h] = c_state[h]
            m_ref[h] = jnp.broadcast_to(m_state[h], (8, LANE))
        return carry

    lax.fori_loop(0, A_SPAN // P2, pair, 0)


def _mlstm(p, d):
    n_span = N_TOK // A_SPAN
    if d == 0:
        span = lambda s: s
    else:
        span = lambda s: jnp.where(s == 0, 0, n_span - s)
    return pl.pallas_call(
        functools.partial(_mlstm_kernel, d=d), grid=(n_span,),
        in_specs=[pl.BlockSpec((A_SPAN, 1024), lambda s: (span(s), COL_QK // 1024)),
                  pl.BlockSpec((A_SPAN, 1024), lambda s: (span(s), COL_AV // 1024)),
                  pl.BlockSpec((A_SPAN, LANE), lambda s: (span(s), COL_G // LANE))],
        out_specs=pl.BlockSpec((A_SPAN, 1024), lambda s: (span(s), 0)),
        out_shape=jax.ShapeDtypeStruct((N_TOK, 1024), F32),
        scratch_shapes=[pltpu.VMEM((A_HEADS, A_DK, 2 * LANE), F32), pltpu.VMEM((A_HEADS, 8, LANE), F32)],
        compiler_params=_cparams(("arbitrary",)), name="mlstm_fwd" if d == 0 else "mlstm_bwd",
    )(p, p, p)


def _aout_kernel(hf_ref, hb_ref, o_ref, g_ref, y_ref):
    for h in range(A_HEADS):
        sl = slice(h * LANE, (h + 1) * LANE)
        x = hf_ref[:, sl] + hb_ref[:, sl]
        x = x * lax.rsqrt(jnp.mean(x * x, axis=-1, keepdims=True) + EPS)
        o = o_ref[:, sl]
        y_ref[:, sl] = ((x * g_ref[:, sl]) * (1.0 / (1.0 + jnp.exp(-o)))).astype(y_ref.dtype)


def _aout(hf, hb, p, mh_g):
    tm = 256
    blk = pl.BlockSpec((tm, 1024), lambda i: (i, 0))
    return pl.pallas_call(
        _aout_kernel, grid=(N_TOK // tm,),
        in_specs=[blk, blk, pl.BlockSpec((tm, 1024), lambda i: (i, COL_AO // 1024)),
                  pl.BlockSpec((1, 1024), lambda i: (0, 0))],
        out_specs=blk, out_shape=jax.ShapeDtypeStruct((N_TOK, 1024), BF16),
        compiler_params=_cparams(("parallel",)), name="mlstm_out",
    )(hf, hb, p, mh_g.reshape(1, 1024))


def _top_values(s, k):
    vals = []
    cur = s
    for _ in range(k):
        mx = jnp.max(cur, axis=0, keepdims=True)
        vals.append(mx)
        cur = jnp.where(cur == mx, NEG, cur)
    return vals


def _sort_network(n):
    def merge(lo, hi, r):
        step = r * 2
        if step < hi - lo:
            yield from merge(lo, hi, step)
            yield from merge(lo + r, hi, step)
            yield from ((i, i + r) for i in range(lo + r, hi - r, step))
        else:
            yield (lo, lo + r)

    def sort(lo, hi):
        if hi - lo >= 1:
            mid = lo + (hi - lo) // 2
            yield from sort(lo, mid)
            yield from sort(mid + 1, hi)
            yield from merge(lo, hi, 1)

    return list(sort(0, n - 1))


def _exchange(xs, i, j):
    xs[i], xs[j] = jnp.maximum(xs[i], xs[j]), jnp.minimum(xs[i], xs[j])


def _top16_sorted(s):
    n = P_TOPK
    xs = [s[n_ * 8:(n_ + 1) * 8, :] for n_ in range(n)]
    for i, j in _sort_network(n):
        _exchange(xs, i, j)
    for shift in (4, 2, 1):
        other = [pltpu.roll(x, shift, axis=0) for x in xs]
        xs = [jnp.maximum(xs[k], other[n - 1 - k]) for k in range(n)]
        for dist in (8, 4, 2, 1):
            for i in range(n):
                if i & dist == 0:
                    _exchange(xs, i, i + dist)
    return xs


def _router_kernel(ht_ref, wqt_ref, keys_ref, s_ref, e_ref, thr_ref, q_scr, *, tn):
    q_scr[...] = _dot(wqt_ref[...], ht_ref[...]).astype(BF16)
    row8 = lax.broadcasted_iota(jnp.int32, (8, tn), 0)
    for h in range(P_HEADS):
        tops, scores = [], []
        for p in range(2):
            hp = 2 * h + p
            s = _dot(keys_ref[hp], q_scr[hp * LANE:(hp + 1) * LANE, :])
            s_ref[hp] = s
            scores.append(s)
            tops.append(_top16_sorted(s))
        ta, tb = tops
        a_lo, a_hi, b_hi = (jnp.full((8, tn), NEG, F32) for _ in range(3))
        for i in range(8):
            a_lo = jnp.where(row8 == i, ta[i], a_lo)
            a_hi = jnp.where(row8 == i, ta[i + 8], a_hi)
            b_hi = jnp.where(row8 == i, tb[i + 8], b_hi)
        parts = [a_lo + tb[0], a_hi + tb[0], a_lo + tb[1]]
        for j in range(2, 8):
            parts.append(jnp.where(row8 < P_TOPK // (j + 1), a_lo + tb[j], NEG))
        parts.append(ta[0] + b_hi)
        cand = jnp.concatenate(parts, axis=0)
        best = _top_values(cand, P_TOPK)
        z = jnp.zeros_like(best[0])
        for c in best:
            z = z + jnp.exp(c - best[0])
        thr_ref[h:h + 1, :] = best[P_TOPK - 1]
        e_ref[2 * h] = jnp.exp(scores[0] - ta[0][0:1, :]) / z
        e_ref[2 * h + 1] = jnp.exp(scores[1] - tb[0][0:1, :])


def _router(ht, wqt, keys):
    tn = 256
    t = ht.shape[1]
    big = pl.BlockSpec((2 * P_HEADS, P_NKEYS, tn), lambda i: (0, 0, i))
    shp = jax.ShapeDtypeStruct((2 * P_HEADS, P_NKEYS, t), F32)
    return pl.pallas_call(
        functools.partial(_router_kernel, tn=tn), grid=(t // tn,),
        in_specs=[pl.BlockSpec((D, tn), lambda i: (0, i)),
                  pl.BlockSpec((D, D), lambda i: (0, 0)),
                  pl.BlockSpec((2 * P_HEADS, P_NKEYS, LANE), lambda i: (0, 0, 0))],
        out_specs=[big, big, pl.BlockSpec((P_HEADS, tn), lambda i: (0, i))],
        out_shape=[shp, shp, jax.ShapeDtypeStruct((P_HEADS, t), F32)],
        scratch_shapes=[pltpu.VMEM((D, tn), BF16)],
        compiler_params=_cparams(("parallel",)), name="peer_router",
    )(ht, wqt, keys)


P_EC = 1024
P_TN = 768


def _gelu(x):
    return 0.5 * x * (1.0 + lax.erf(x * (2.0 ** -0.5)))


P_HALF = P_EC // 2
P_STEPS = P_EXPERTS // P_EC + 1
P_GATE_ROWS = 32


def _expert_gates(z_ref, w_ref, s1_ref, e1_ref, row0, s2_ref, e2_ref, thr_ref):
    n_r = P_HALF // P_NKEYS
    for lb in range(P_TN // LANE):
        sl = slice(lb * LANE, (lb + 1) * LANE)
        for b0 in range(0, P_NKEYS, P_GATE_ROWS):
            gates = [jnp.zeros((P_GATE_ROWS, LANE), F32) for _ in range(n_r)]
            for h in range(P_HEADS):
                s2 = s2_ref[h, 0, b0:b0 + P_GATE_ROWS, sl]
                e2 = e2_ref[h, 0, b0:b0 + P_GATE_ROWS, sl]
                thr = thr_ref[h:h + 1, sl]
                for r in range(n_r):
                    pair = s1_ref[h, 0, row0 + r:row0 + r + 1, sl] + s2
                    w = e1_ref[h, 0, row0 + r:row0 + r + 1, sl] * e2
                    gates[r] = gates[r] + jnp.where(pair >= thr, w, 0.0)
            for r in range(n_r):
                rows = slice(r * P_NKEYS + b0, r * P_NKEYS + b0 + P_GATE_ROWS)
                w_ref[lb, rows, :] = (gates[r] * _gelu(z_ref[lb, rows, :])).astype(BF16)


def _expert_kernel(ht_ref, u_ref, vt_ref, s1p_ref, s1c_ref, s2_ref, e1p_ref, e1c_ref, e2_ref, thr_ref, y_ref,
                   z0, z1, w0, w1):
    j = pl.program_id(1)
    last = P_STEPS - 1
    half_rows = P_HALF // P_NKEYS

    n_lb = P_TN // LANE

    def stage_a(z, half):
        zf = _dot(u_ref[half * P_HALF:(half + 1) * P_HALF, :], ht_ref[...])
        for lb in range(n_lb):
            z[lb] = zf[:, lb * LANE:(lb + 1) * LANE]

    def stage_b(w, half, y_old):
        wf = jnp.concatenate([w[lb] for lb in range(n_lb)], axis=1)
        y_ref[...] = y_old + _dot(vt_ref[:, half * P_HALF:(half + 1) * P_HALF], wf)

    @pl.when(j == 0)
    def _():
        stage_a(z0, 0)
        stage_a(z1, 1)
        _expert_gates(z0, w0, s1c_ref, e1c_ref, 0, s2_ref, e2_ref, thr_ref)

    @pl.when((j > 0) & (j < last))
    def _():
        stage_a(z0, 0)
        _expert_gates(z1, w1, s1p_ref, e1p_ref, half_rows, s2_ref, e2_ref, thr_ref)
        stage_b(w0, 0, jnp.where(j == 1, 0.0, y_ref[...]))
        stage_a(z1, 1)
        _expert_gates(z0, w0, s1c_ref, e1c_ref, 0, s2_ref, e2_ref, thr_ref)
        stage_b(w1, 1, y_ref[...])

    @pl.when(j == last)
    def _():
        _expert_gates(z1, w1, s1p_ref, e1p_ref, half_rows, s2_ref, e2_ref, thr_ref)
        stage_b(w0, 0, y_ref[...])
        stage_b(w1, 1, y_ref[...])


def _experts(ht, u, vt, layer, s, e, thr):
    t = ht.shape[1]
    ac = P_EC // P_NKEYS
    n_blk = P_EXPERTS // P_EC
    s4 = s.reshape(P_HEADS, 2, P_NKEYS, t)
    e4 = e.reshape(P_HEADS, 2, P_NKEYS, t)
    once = pl.Buffered(1)
    cur = lambda j: jnp.minimum(j, n_blk - 1)
    prev = lambda j: jnp.maximum(j - 1, 0)
    first_p = pl.BlockSpec((P_HEADS, 1, ac, P_TN), lambda i, j: (0, 0, prev(j), i))
    first_c = pl.BlockSpec((P_HEADS, 1, ac, P_TN), lambda i, j: (0, 0, cur(j), i))
    second = pl.BlockSpec((P_HEADS, 1, P_NKEYS, P_TN), lambda i, j: (0, 1, 0, i), pipeline_mode=once)
    return pl.pallas_call(
        _expert_kernel, grid=(t // P_TN, P_STEPS),
        in_specs=[pl.BlockSpec((D, P_TN), lambda i, j: (0, i), pipeline_mode=once),
                  pl.BlockSpec((None, P_EC, D), lambda i, j: (layer, cur(j), 0)),
                  pl.BlockSpec((None, D, P_EC), lambda i, j: (layer, 0, prev(j))),
                  first_p, first_c, second, first_p, first_c, second,
                  pl.BlockSpec((P_HEADS, P_TN), lambda i, j: (0, i), pipeline_mode=once)],
        out_specs=pl.BlockSpec((D, P_TN), lambda i, j: (0, i)),
        out_shape=jax.ShapeDtypeStruct((D, t), F32),
        scratch_shapes=[pltpu.VMEM((P_TN // LANE, P_HALF, LANE), F32), pltpu.VMEM((P_TN // LANE, P_HALF, LANE), F32),
                        pltpu.VMEM((P_TN // LANE, P_HALF, LANE), BF16), pltpu.VMEM((P_TN // LANE, P_HALF, LANE), BF16)],
        compiler_params=_cparams(("parallel", "arbitrary")), name="peer_experts",
    )(ht, u, vt, s4, s4, s4, e4, e4, e4, thr)


def _peer(ht, wq, keys, u_all, vt_all, layer):
    s, e, thr = _router(ht, wq.T.astype(BF16), keys.reshape(2 * P_HEADS, P_NKEYS, LANE).astype(BF16))
    return _experts(ht, u_all, vt_all, layer, s, e, thr)


def _ab_weights(w_in, gate_b):
    aq, ak, av, ao, ag, bq, bk, bv = jnp.split(w_in, np.cumsum([512, 512, 1024, 1024, 32, 1024, 256])[:].tolist(), axis=1)
    qk_t = jnp.concatenate([aq, ak], axis=1).astype(BF16).T.reshape(2, A_HEADS, A_DK, D)
    qk = qk_t.transpose(1, 0, 2, 3).reshape(2 * A_HEADS * A_DK, D).T
    pad = jnp.zeros((D, AB_N - COL_G - 32), BF16)
    w = jnp.concatenate([qk] + [a.astype(BF16) for a in (av, ao, bq, bk, bv, ag)] + [pad], axis=1)
    bias = jnp.zeros((1, AB_N), F32).at[0, COL_G:COL_G + 32].set(gate_b.astype(F32))
    return w, bias


def _mixer_ab(h, w_in, gate_b, mh_g, sink, rope_tabs, win_bias):
    w, bias = _ab_weights(w_in, gate_b)
    p = _matmul(h, w, bias=bias)
    hf = _mlstm(p, 0)
    hb = _mlstm(p, 1)
    ya = _aout(hf, hb, p, mh_g)
    qr, kr = _rope(p, *rope_tabs)
    sink_b = jnp.repeat(sink.astype(F32), LANE).reshape(1, B_HEADS * LANE)
    yb_l = _attn_latent(qr, 0, 0, kr, 0, 0, p, COL_BV // LANE, N_CTX, p, COL_BK // LANE, p, COL_BV // LANE,
                        win_bias, sink_b, n_heads=B_HEADS, n_kv=B_KV, kvps=1, mq=B_WINDOW, wk=3 * B_WINDOW,
                        back=B_WINDOW, name="window_attn")
    yb_c = _attn_context(p, COL_BQ // (4 * LANE), p, COL_BK // LANE, p, COL_BV // LANE, sink_b,
                         n_heads=B_HEADS, n_kv=B_KV, name="window_attn_ctx")
    return jnp.concatenate([ya, jnp.concatenate([yb_c, yb_l], axis=0)], axis=1)


def _mixer_c(h, w_in, rpb):
    p = _matmul(h, w_in.astype(BF16), out_dtype=BF16)
    bias = _na_table(rpb)
    mq = NA_ROWS * GRID_W
    y_l = _attn_latent(p, N_CTX // mq, 0, p, C_HEADS, N_CTX, p, 2 * C_HEADS, N_CTX, p, C_HEADS, p, 2 * C_HEADS,
                       bias, None, n_heads=C_HEADS, n_kv=C_HEADS, kvps=NA_HEADS_PER_STEP, mq=mq, wk=NA_BAND * GRID_W,
                       back=(NA_KH // 2) * GRID_W, name="na_attn")
    y_c = _attn_context(p, 0, p, C_HEADS, p, 2 * C_HEADS, None, n_heads=C_HEADS, n_kv=C_HEADS, name="na_attn_ctx")
    return jnp.concatenate([y_c, y_l], axis=0)


def _mod_rows(m6, i_shift, i_scale, i_gate):
    z = jnp.zeros((D,), F32)
    pick = lambda r, i: m6[r, i] if i is not None else z
    return jnp.stack([pick(0, i_shift), pick(0, i_scale), pick(1, i_shift), pick(1, i_scale),
                      pick(0, i_gate), pick(1, i_gate), z, z])


def kernel(x, c, ctx, c_ctx, ada_w, ada_b, norm1_g, norm2_g, ab_w_in, ab_gate_b, ab_mh_g, ab_sink, ab_w_out,
           na_w_in, na_rpb, na_w_out, peer_wq, peer_keys, peer_u, peer_v, final_g):
    xs = jnp.concatenate([ctx[0], x[0]], axis=0).astype(F32)
    cc = jnp.zeros((16, D), F32).at[0].set(c[0]).at[1].set(c_ctx)
    mods = _adaln(cc, ada_w, ada_b)[:, :2].reshape(DEPTH, 2, 6, D)
    rope_tabs = _rope_tables()
    win_bias = _window_bias()
    u_all = peer_u.astype(BF16)
    vt_all = peer_v.transpose(0, 2, 1).astype(BF16)

    h = _norm(xs, norm1_g[0], _mod_rows(mods[0], 0, 1, None))
    for l in range(DEPTH):
        m6 = mods[l]
        if l % 2 == 0:
            e = l // 2
            ymix = _mixer_ab(h, ab_w_in[e], ab_gate_b[e], ab_mh_g[e], ab_sink[e], rope_tabs, win_bias)
            w_out = ab_w_out[e]
        else:
            o = l // 2
            ymix = _mixer_c(h, na_w_in[o], na_rpb[o])
            w_out = na_w_out[o]
        gv = jnp.concatenate([m6[:, 2], jnp.zeros((6, D), F32)], axis=0)
        xs = _matmul(ymix, w_out.astype(BF16), resid=xs, gates=gv)
        h2t = _norm(xs, norm2_g[l], _mod_rows(m6, 3, 4, None), h_t=True)
        yt = _peer(h2t, peer_wq[l], peer_keys[l], u_all, vt_all, l)
        if l + 1 < DEPTH:
            mv = _mod_rows(mods[l + 1], 0, 1, None).at[4].set(m6[0, 5]).at[5].set(m6[1, 5])
            xs, h = _norm(xs, norm1_g[l + 1], mv, y=yt, y_t=True)
        else:
            mv = _mod_rows(m6, None, None, 5)
            _, out = _norm(xs, final_g, mv, y=yt, y_t=True, out_dtype=F32, row_off=N_CTX // 256)
    return out[None]
```

```python
import functools

import numpy as np
import jax
import jax.numpy as jnp
from jax import lax
from jax.experimental import pallas as pl
from jax.experimental.pallas import tpu as pltpu

F32 = jnp.float32
BF16 = jnp.bfloat16

D = 2048
N_LAT = 8192
N_CTX = 256
N_TOK = N_CTX + N_LAT
DEPTH = 4
GRID_W = 64
GRID_H = N_LAT // GRID_W
EPS = 1e-6
LANE = 128
NEG = -1e30

A_HEADS = 8
A_DK = 64
A_DV = 128
A_CHUNK = 64
A_SPAN = 256
B_HEADS = 8
B_KV = 2
B_WINDOW = 128
ROPE_THETA = 10000.0
C_HEADS = 16
NA_KH = 8
NA_KW = 16
NA_ROWS = 4
NA_BAND = NA_ROWS + NA_KH - 1
NA_HEADS_PER_STEP = 2
P_HEADS = 8
P_NKEYS = 128
P_TOPK = 16
P_EXPERTS = P_NKEYS * P_NKEYS

AB_N = 5120
COL_QK, COL_AV, COL_AO, COL_BQ, COL_BK, COL_BV, COL_G = 0, 1024, 2048, 3072, 4096, 4352, 4608

VMEM_LIMIT = 52 * 1024 * 1024


def _cparams(sem):
    return pltpu.CompilerParams(dimension_semantics=sem, vmem_limit_bytes=VMEM_LIMIT)


def _dot(a, b):
    return jnp.dot(a, b, preferred_element_type=F32)


def _dot_nt(a, b):
    return lax.dot_general(a, b, (((1,), (1,)), ((), ())), preferred_element_type=F32)


def _ada_kernel(c_ref, w_ref, b_ref, o_ref):
    c = c_ref[...]
    s = c / (1.0 + jnp.exp(-c))
    w = w_ref[0]
    s_hi = s.astype(BF16)
    s_lo = (s - s_hi.astype(F32)).astype(BF16)
    w_hi = w.astype(BF16)
    w_lo = (w - w_hi.astype(F32)).astype(BF16)
    o_ref[0] = _dot(s_hi, w_hi) + _dot(s_lo, w_hi) + _dot(s_hi, w_lo) + b_ref[0]


def _adaln(cc, ada_w, ada_b):
    tn = 1024
    n = ada_w.shape[-1]
    return pl.pallas_call(
        _ada_kernel,
        grid=(DEPTH, n // tn),
        in_specs=[pl.BlockSpec((16, D), lambda l, j: (0, 0)),
                  pl.BlockSpec((1, D, tn), lambda l, j: (l, 0, j)),
                  pl.BlockSpec((1, 1, tn), lambda l, j: (l, 0, j))],
        out_specs=pl.BlockSpec((1, 16, tn), lambda l, j: (l, 0, j)),
        out_shape=jax.ShapeDtypeStruct((DEPTH, 16, n), F32),
        compiler_params=_cparams(("arbitrary", "arbitrary")),
        name="adaln",
    )(cc, ada_w, ada_b.reshape(DEPTH, 1, n))


def _norm_kernel(*refs, has_resid, y_t, h_t, n_ctx, tm, row_off):
    if has_resid:
        x_ref, y_ref, g_ref, mv_ref, xo_ref, h_ref = refs
    else:
        x_ref, g_ref, mv_ref, h_ref = refs
    row = (pl.program_id(0) + row_off) * tm + lax.broadcasted_iota(jnp.int32, (tm, 1), 0)
    is_ctx = row < n_ctx
    x = x_ref[...]
    if has_resid:
        gate = jnp.where(is_ctx, mv_ref[5:6, :], mv_ref[4:5, :])
        y = y_ref[...].T if y_t else y_ref[...]
        x = x + gate * y
        xo_ref[...] = x
    ms = jnp.mean(x * x, axis=-1, keepdims=True)
    yn = (x * lax.rsqrt(ms + EPS)) * g_ref[...]
    shift = jnp.where(is_ctx, mv_ref[2:3, :], mv_ref[0:1, :])
    scale = jnp.where(is_ctx, mv_ref[3:4, :], mv_ref[1:2, :])
    h = yn * (1.0 + scale) + shift
    h_ref[...] = (h.T if h_t else h).astype(h_ref.dtype)


def _norm(x, g, mv, *, y=None, y_t=False, h_t=False, out_dtype=BF16, row_off=0):
    tm = 256
    n_rows = x.shape[0] - row_off * tm
    blk = pl.BlockSpec((tm, D), lambda i: (i + row_off, 0))
    oblk = pl.BlockSpec((tm, D), lambda i: (i, 0))
    yblk = pl.BlockSpec((D, tm), lambda i: (0, i + row_off)) if y_t else blk
    hblk = pl.BlockSpec((D, tm), lambda i: (0, i)) if h_t else oblk
    vec = pl.BlockSpec((1, D), lambda i: (0, 0))
    mvs = pl.BlockSpec((8, D), lambda i: (0, 0))
    has_resid = y is not None
    kern = functools.partial(_norm_kernel, has_resid=has_resid, y_t=y_t, h_t=h_t, n_ctx=N_CTX, tm=tm, row_off=row_off)
    h_shape = jax.ShapeDtypeStruct((D, n_rows) if h_t else (n_rows, D), out_dtype)
    if has_resid:
        return pl.pallas_call(
            kern, grid=(n_rows // tm,),
            in_specs=[blk, yblk, vec, mvs], out_specs=[oblk, hblk],
            out_shape=[jax.ShapeDtypeStruct((n_rows, D), F32), h_shape],
            compiler_params=_cparams(("parallel",)), name="resid_norm",
        )(x, y, g.reshape(1, D), mv)
    return pl.pallas_call(
        kern, grid=(n_rows // tm,),
        in_specs=[blk, vec, mvs], out_specs=hblk, out_shape=h_shape,
        compiler_params=_cparams(("parallel",)), name="norm",
    )(x, g.reshape(1, D), mv)


MM_TM, MM_TN = 768, 1024


def _mm_kernel(*refs, has_bias, has_resid, n_ctx, tm):
    a_ref, w_ref = refs[0], refs[1]
    o_ref = refs[-1]
    acc = _dot(a_ref[...], w_ref[...])
    k = 2
    if has_bias:
        acc = acc + refs[k][...]
        k += 1
    if has_resid:
        x_ref, gv_ref = refs[k], refs[k + 1]
        row = pl.program_id(0) * tm + lax.broadcasted_iota(jnp.int32, (tm, 1), 0)
        gate = jnp.where(row < n_ctx, gv_ref[1:2, :], gv_ref[0:1, :])
        acc = x_ref[...] + gate * acc
    o_ref[...] = acc.astype(o_ref.dtype)


def _matmul(a, w, *, bias=None, resid=None, gates=None, out_dtype=F32, tm=MM_TM, tn=MM_TN):
    m, k = a.shape
    n = w.shape[1]
    in_specs = [pl.BlockSpec((tm, k), lambda i, j: (i, 0)), pl.BlockSpec((k, tn), lambda i, j: (0, j))]
    args = [a, w]
    if bias is not None:
        in_specs.append(pl.BlockSpec((1, tn), lambda i, j: (0, j)))
        args.append(bias)
    if resid is not None:
        in_specs += [pl.BlockSpec((tm, tn), lambda i, j: (i, j)), pl.BlockSpec((8, tn), lambda i, j: (0, j))]
        args += [resid, gates]
    kern = functools.partial(_mm_kernel, has_bias=bias is not None, has_resid=resid is not None, n_ctx=N_CTX, tm=tm)
    return pl.pallas_call(
        kern, grid=(m // tm, n // tn), in_specs=in_specs,
        out_specs=pl.BlockSpec((tm, tn), lambda i, j: (i, j)),
        out_shape=jax.ShapeDtypeStruct((m, n), out_dtype),
        compiler_params=_cparams(("parallel", "arbitrary")), name="matmul",
    )(*args)


def _rope_tile(x, cos, sin):
    lane = lax.broadcasted_iota(jnp.int32, x.shape, 1)
    partner = jnp.where((lane % 64) < 32, pltpu.roll(x, 96, axis=1), pltpu.roll(x, 32, axis=1))
    return x * cos + partner * sin


def _rope_kernel(q_ref, k_ref, cos_ref, sin_ref, qo_ref, ko_ref):
    cos, sin = cos_ref[...], sin_ref[...]
    for h in range(B_HEADS):
        sl = slice(h * LANE, (h + 1) * LANE)
        qo_ref[:, sl] = _rope_tile(q_ref[:, sl], cos, sin).astype(qo_ref.dtype)
    for h in range(B_KV):
        sl = slice(h * LANE, (h + 1) * LANE)
        ko_ref[:, sl] = _rope_tile(k_ref[:, sl], cos, sin).astype(ko_ref.dtype)


def _rope_tables():
    t = jnp.arange(N_LAT)
    freqs = ROPE_THETA ** (-jnp.arange(32, dtype=F32) / 32)
    ar = (t // GRID_W).astype(F32)[:, None] * freqs[None, :]
    ac = (t % GRID_W).astype(F32)[:, None] * freqs[None, :]
    cos = jnp.concatenate([jnp.cos(ar), jnp.cos(ar), jnp.cos(ac), jnp.cos(ac)], axis=1)
    sin = jnp.concatenate([-jnp.sin(ar), jnp.sin(ar), -jnp.sin(ac), jnp.sin(ac)], axis=1)
    return cos, sin


def _rope(p, cos, sin):
    tr = 256
    off = N_CTX // tr
    return pl.pallas_call(
        _rope_kernel, grid=(N_LAT // tr,),
        in_specs=[pl.BlockSpec((tr, 1024), lambda i: (i + off, COL_BQ // 1024)),
                  pl.BlockSpec((tr, 256), lambda i: (i + off, COL_BK // 256)),
                  pl.BlockSpec((tr, LANE), lambda i: (i, 0)),
                  pl.BlockSpec((tr, LANE), lambda i: (i, 0))],
        out_specs=[pl.BlockSpec((tr, 1024), lambda i: (i, 0)), pl.BlockSpec((tr, 256), lambda i: (i, 0))],
        out_shape=[jax.ShapeDtypeStruct((N_LAT, 1024), BF16), jax.ShapeDtypeStruct((N_LAT, 256), BF16)],
        compiler_params=_cparams(("parallel",)), name="rope",
    )(p, p, cos, sin)


def _na_bias_block(b_ref, kk, i):
    lane = lax.broadcasted_iota(jnp.int32, (GRID_W, LANE), 1)
    r0 = i * NA_ROWS
    ub = jnp.clip(r0 - NA_KH // 2, 0, GRID_H - NA_BAND)
    rows = []
    for ri in range(NA_ROWS):
        r = r0 + ri
        rs = jnp.clip(r - NA_KH // 2, 0, GRID_H - NA_KH)
        idx = []
        for a in range(NA_BAND):
            krow = ub + a
            ok = (krow >= rs) & (krow < rs + NA_KH)
            idx.append(jnp.where(ok, krow - r + (NA_KH - 1), 2 * NA_KH - 1))
        pieces = [jnp.where(lane < GRID_W, b_ref[kk, idx[a]], b_ref[kk, idx[a + 1]]) for a in range(0, NA_BAND - 1, 2)]
        pieces.append(b_ref[kk, idx[NA_BAND - 1]][:, :GRID_W])
        rows.append(jnp.concatenate(pieces, axis=1))
    return jnp.concatenate(rows, axis=0)


def _attn_kernel(*refs, g, kvps, bias_per_head, na_table, mq, wk, back, n_keys, k_off, v_off, has_band, has_sink,
                 scale):
    refs = list(refs)
    q_ref = refs.pop(0)
    if has_band:
        k_ref, v_ref = refs.pop(0), refs.pop(0)
    kc_ref, vc_ref = refs.pop(0), refs.pop(0)
    if has_band:
        b_ref = refs.pop(0)
    if has_sink:
        s_ref = refs.pop(0)
    o_ref = refs.pop(0)

    if has_band:
        i = pl.program_id(1)
        ub = pl.multiple_of(jnp.clip(i * mq - back, 0, n_keys - wk), 64)
    for hq in range(kvps * g):
        kk = hq // g
        ksl = slice(kk * LANE, (kk + 1) * LANE)
        kc = kc_ref[:, ksl].astype(BF16)
        vc = vc_ref[:, ksl].astype(BF16)
        if has_band:
            kb = k_ref[pl.ds(k_off + ub, wk), ksl].astype(BF16)
            vb = v_ref[pl.ds(v_off + ub, wk), ksl].astype(BF16)
            bias = _na_bias_block(b_ref, kk, i) if na_table else b_ref[kk if bias_per_head else 0, 0]
        hh = hq
        sl = slice(hh * LANE, (hh + 1) * LANE)
        q = q_ref[:, sl].astype(BF16)
        s_ctx = _dot_nt(q, kc) * scale
        m = jnp.max(s_ctx, axis=-1, keepdims=True)
        if has_band:
            s_loc = _dot_nt(q, kb) * scale + bias
            m = jnp.maximum(m, jnp.max(s_loc, axis=-1, keepdims=True))
        if has_sink:
            snk = jnp.max(s_ref[:, sl], axis=1, keepdims=True)
            m = jnp.maximum(m, snk)
        p_ctx = jnp.exp(s_ctx - m)
        den = jnp.sum(p_ctx, axis=-1, keepdims=True)
        acc = _dot(p_ctx.astype(BF16), vc)
        if has_band:
            p_loc = jnp.exp(s_loc - m)
            den = den + jnp.sum(p_loc, axis=-1, keepdims=True)
            acc = acc + _dot(p_loc.astype(BF16), vb)
        if has_sink:
            den = den + jnp.exp(snk - m)
        o_ref[:, sl] = (acc / den).astype(o_ref.dtype)


def _pattern(i, n):
    return jnp.where(i == 0, 0, jnp.where(i == n - 1, 2, 1))


def _attn_latent(q, q_blk0, q_col0, k, k_col0, k_off, v, v_col0, v_off, kc, kc_col0, vc, vc_col0,
                 bias, sink, *, n_heads, n_kv, kvps, mq, wk, back, name):
    g = n_heads // n_kv
    nq = N_LAT // mq
    per_head_bias = bias.shape[0] > 1
    na_table = bias.shape[1] == 2 * NA_KH
    kw, qw = kvps * LANE, kvps * g * LANE
    if na_table:
        bias_spec = pl.BlockSpec((kvps,) + bias.shape[1:], lambda j, i: (j, 0, 0, 0))
    else:
        bias_spec = pl.BlockSpec((kvps if per_head_bias else 1, 1, mq, wk),
                                 lambda j, i: (j if per_head_bias else 0, _pattern(i, nq), 0, 0))
    in_specs = [
        pl.BlockSpec((mq, qw), lambda j, i: (i + q_blk0, q_col0 // (kvps * g) + j)),
        pl.BlockSpec((k.shape[0], kw), lambda j, i: (0, k_col0 // kvps + j)),
        pl.BlockSpec((v.shape[0], kw), lambda j, i: (0, v_col0 // kvps + j)),
        pl.BlockSpec((N_CTX, kw), lambda j, i: (0, kc_col0 // kvps + j)),
        pl.BlockSpec((N_CTX, kw), lambda j, i: (0, vc_col0 // kvps + j)),
        bias_spec,
    ]
    args = [q, k, v, kc, vc, bias]
    if sink is not None:
        in_specs.append(pl.BlockSpec((1, qw), lambda j, i: (0, j)))
        args.append(sink)
    kern = functools.partial(_attn_kernel, g=g, kvps=kvps, bias_per_head=per_head_bias, na_table=na_table,
                             mq=mq, wk=wk, back=back,
                             n_keys=N_LAT, k_off=k_off, v_off=v_off, has_band=True, has_sink=sink is not None,
                             scale=LANE ** -0.5)
    return pl.pallas_call(
        kern, grid=(n_kv // kvps, nq), in_specs=in_specs,
        out_specs=pl.BlockSpec((mq, qw), lambda j, i: (i, j)),
        out_shape=jax.ShapeDtypeStruct((N_LAT, n_heads * LANE), BF16),
        compiler_params=_cparams(("arbitrary", "arbitrary")), name=name,
    )(*args)


def _attn_context(q, q_col0, kc, kc_col0, vc, vc_col0, sink, *, n_heads, n_kv, name):
    g = n_heads // n_kv
    in_specs = [
        pl.BlockSpec((N_CTX, g * LANE), lambda j, i: (0, q_col0 + j)),
        pl.BlockSpec((N_CTX, LANE), lambda j, i: (0, kc_col0 + j)),
        pl.BlockSpec((N_CTX, LANE), lambda j, i: (0, vc_col0 + j)),
    ]
    args = [q, kc, vc]
    if sink is not None:
        in_specs.append(pl.BlockSpec((1, g * LANE), lambda j, i: (0, j)))
        args.append(sink)
    kern = functools.partial(_attn_kernel, g=g, kvps=1, bias_per_head=False, na_table=False, mq=N_CTX, wk=0, back=0, n_keys=0,
                             k_off=0, v_off=0, has_band=False, has_sink=sink is not None, scale=LANE ** -0.5)
    return pl.pallas_call(
        kern, grid=(n_kv, 1), in_specs=in_specs,
        out_specs=pl.BlockSpec((N_CTX, g * LANE), lambda j, i: (0, j)),
        out_shape=jax.ShapeDtypeStruct((N_CTX, n_heads * LANE), BF16),
        compiler_params=_cparams(("arbitrary", "arbitrary")), name=name,
    )(*args)


def _window_bias():
    t, w = N_LAT, B_WINDOW
    nb = t // w
    out = []
    for bi in (0, 1, nb - 1):
        ub = min(max(bi * w - w, 0), t - 3 * w)
        qpos = bi * w + np.arange(w)[:, None]
        kpos = ub + np.arange(3 * w)[None, :]
        out.append(np.where(np.abs(kpos - qpos) <= w, 0.0, NEG))
    return jnp.asarray(np.stack(out)[None], F32)


def _na_table(rpb):
    col = np.arange(GRID_W)
    c0 = np.clip(col - NA_KW // 2, 0, GRID_W - NA_KW)
    col_ok = (col[None, :] >= c0[:, None]) & (col[None, :] < c0[:, None] + NA_KW)
    dc = np.clip(col[None, :] - col[:, None] + (NA_KW - 1), 0, 2 * NA_KW - 2)
    onehot = (dc[None] == np.arange(2 * NA_KW - 1)[:, None, None]).astype(np.float32)
    t = jnp.einsum("hrd,dck->hrck", rpb.astype(F32), jnp.asarray(onehot), precision=lax.Precision.HIGHEST)
    t = jnp.where(col_ok[None, None], t, NEG)
    t = jnp.concatenate([t, jnp.full((C_HEADS, 1, GRID_W, GRID_W), NEG, F32)], axis=1)
    return jnp.concatenate([t, t], axis=-1)


def _log_sigmoid(x):
    return jnp.minimum(x, 0.0) - jnp.log1p(jnp.exp(-jnp.abs(x)))


def _mlstm_kernel(qk_ref, v_ref, g_ref, h_ref, c_ref, m_ref, *, d):
    L = A_CHUNK
    P2 = 2 * L

    @pl.when(pl.program_id(0) == 0)
    def _():
        c_ref[...] = jnp.zeros_like(c_ref)
        m_ref[...] = jnp.zeros_like(m_ref)

    ri = lax.broadcasted_iota(jnp.int32, (L, P2), 0)
    ci = lax.broadcasted_iota(jnp.int32, (L, P2), 1)
    lane = lax.broadcasted_iota(jnp.int32, (1, P2), 1)
    ones_blk = jnp.ones((P2, LANE), F32)
    i_lane, f_lane = 2 * d * A_HEADS, (2 * d + 1) * A_HEADS
    order = (0, 1) if d == 0 else (1, 0)
    rp = lax.broadcasted_iota(jnp.int32, (P2, P2), 0)
    cp = lax.broadcasted_iota(jnp.int32, (P2, P2), 1)
    before = (rp <= cp) if d == 0 else (rp >= cp)
    cum_mat = (before & ((rp >= L) == (cp >= L))).astype(BF16)

    def pair(pidx, carry):
        pp = pidx if d == 0 else A_SPAN // P2 - 1 - pidx
        r0 = pl.multiple_of(pp * P2, P2)
        gt = g_ref[pl.ds(r0, P2), :].T[:4 * A_HEADS, :]
        f_all = _log_sigmoid(gt)
        f_hi = f_all.astype(BF16)
        f_r1 = f_all - f_hi.astype(F32)
        f_mid = f_r1.astype(BF16)
        f_lo = (f_r1 - f_mid.astype(F32)).astype(BF16)
        cum_all = _dot(f_hi, cum_mat) + _dot(f_mid, cum_mat) + _dot(f_lo, cum_mat)
        c_state = [c_ref[h] for h in range(A_HEADS)]
        m_state = [m_ref[h, 0:1, :] for h in range(A_HEADS)]
        h_out = []
        for h in range(A_HEADS):
            sl = slice(h * LANE, (h + 1) * LANE)
            qk = qk_ref[pl.ds(r0, P2), sl]
            kt = qk.T[A_DK:, :]
            v_aug = jnp.concatenate([v_ref[pl.ds(r0, P2), sl], ones_blk], axis=1).astype(BF16)
            i_row = gt[i_lane + h:i_lane + h + 1, :]
            f_row = f_all[f_lane + h:f_lane + h + 1, :]
            cum_row = cum_all[f_lane + h:f_lane + h + 1, :]
            for sub in order:
                rows = slice(sub * L, (sub + 1) * L)
                own = (lane >= sub * L) & (lane < (sub + 1) * L)
                cs = ci - sub * L
                seen = ((cs <= ri) & (cs >= 0)) if d == 0 else ((cs >= ri) & (cs < L))
                q = (qk[rows, :A_DK] * (A_DK ** -0.5)).astype(BF16)
                cum_col = jnp.sum(jnp.where(seen, f_row, 0.0), axis=1, keepdims=True)
                total = jnp.sum(jnp.where(own, f_row, 0.0), axis=1, keepdims=True)
                m_old, c_old = m_state[h], c_state[h]
                dm = jnp.where(seen, cum_col - cum_row + i_row, NEG)
                inter = cum_col + m_old
                mt = jnp.maximum(inter, jnp.max(dm, axis=1, keepdims=True))
                sw = _dot(q, kt.astype(BF16)) * jnp.exp(dm - mt)
                a = jnp.exp(inter - mt)
                c_bf = c_old.astype(BF16)
                na = _dot(sw.astype(BF16), v_aug)
                num = na[:, :LANE] + a * _dot(q, c_bf[:, :LANE])
                den = na[:, LANE:] + a * _dot(q, c_bf[:, LANE:])
                h_out.append((sub, sl, num / jnp.maximum(jnp.abs(den), jnp.exp(-mt))))
                wend = jnp.where(own, total - cum_row + i_row, NEG)
                m_new = jnp.maximum(total + m_old, jnp.max(wend, axis=1, keepdims=True))
                decay = jnp.exp(total + m_old - m_new)
                wv = jnp.exp(wend - m_new)
                c_state[h] = jnp.concatenate([decay, decay], axis=1) * c_old + _dot((kt * wv).astype(BF16), v_aug)
                m_state[h] = m_new
        for sub, sl, val in h_out:
            h_ref[pl.ds(r0 + sub * L, L), sl] = val
        for h in range(A_HEADS):
            c_ref[h] = c_state[h]
            m_ref[h] = jnp.broadcast_to(m_state[h], (8, LANE))
        return carry

    lax.fori_loop(0, A_SPAN // P2, pair, 0)


def _mlstm(p, d):
    n_span = N_TOK // A_SPAN
    if d == 0:
        span = lambda s: s
    else:
        span = lambda s: jnp.where(s == 0, 0, n_span - s)
    return pl.pallas_call(
        functools.partial(_mlstm_kernel, d=d), grid=(n_span,),
        in_specs=[pl.BlockSpec((A_SPAN, 1024), lambda s: (span(s), COL_QK // 1024)),
                  pl.BlockSpec((A_SPAN, 1024), lambda s: (span(s), COL_AV // 1024)),
                  pl.BlockSpec((A_SPAN, LANE), lambda s: (span(s), COL_G // LANE))],
        out_specs=pl.BlockSpec((A_SPAN, 1024), lambda s: (span(s), 0)),
        out_shape=jax.ShapeDtypeStruct((N_TOK, 1024), F32),
        scratch_shapes=[pltpu.VMEM((A_HEADS, A_DK, 2 * LANE), F32), pltpu.VMEM((A_HEADS, 8, LANE), F32)],
        compiler_params=_cparams(("arbitrary",)), name="mlstm_fwd" if d == 0 else "mlstm_bwd",
    )(p, p, p)


def _aout_kernel(hf_ref, hb_ref, o_ref, g_ref, y_ref):
    for h in range(A_HEADS):
        sl = slice(h * LANE, (h + 1) * LANE)
        x = hf_ref[:, sl] + hb_ref[:, sl]
        x = x * lax.rsqrt(jnp.mean(x * x, axis=-1, keepdims=True) + EPS)
        o = o_ref[:, sl]
        y_ref[:, sl] = ((x * g_ref[:, sl]) * (1.0 / (1.0 + jnp.exp(-o)))).astype(y_ref.dtype)


def _aout(hf, hb, p, mh_g):
    tm = 256
    blk = pl.BlockSpec((tm, 1024), lambda i: (i, 0))
    return pl.pallas_call(
        _aout_kernel, grid=(N_TOK // tm,),
        in_specs=[blk, blk, pl.BlockSpec((tm, 1024), lambda i: (i, COL_AO // 1024)),
                  pl.BlockSpec((1, 1024), lambda i: (0, 0))],
        out_specs=blk, out_shape=jax.ShapeDtypeStruct((N_TOK, 1024), BF16),
        compiler_params=_cparams(("parallel",)), name="mlstm_out",
    )(hf, hb, p, mh_g.reshape(1, 1024))


def _top_values(s, k):
    vals = []
    cur = s
    for _ in range(k):
        mx = jnp.max(cur, axis=0, keepdims=True)
        vals.append(mx)
        cur = jnp.where(cur == mx, NEG, cur)
    return vals


def _sort_network(n):
    def merge(lo, hi, r):
        step = r * 2
        if step < hi - lo:
            yield from merge(lo, hi, step)
            yield from merge(lo + r, hi, step)
            yield from ((i, i + r) for i in range(lo + r, hi - r, step))
        else:
            yield (lo, lo + r)

    def sort(lo, hi):
        if hi - lo >= 1:
            mid = lo + (hi - lo) // 2
            yield from sort(lo, mid)
            yield from sort(mid + 1, hi)
            yield from merge(lo, hi, 1)

    return list(sort(0, n - 1))


def _exchange(xs, i, j):
    xs[i], xs[j] = jnp.maximum(xs[i], xs[j]), jnp.minimum(xs[i], xs[j])


def _top16_sorted(s):
    n = P_TOPK
    xs = [s[n_ * 8:(n_ + 1) * 8, :] for n_ in range(n)]
    for i, j in _sort_network(n):
        _exchange(xs, i, j)
    for shift in (4, 2, 1):
        other = [pltpu.roll(x, shift, axis=0) for x in xs]
        xs = [jnp.maximum(xs[k], other[n - 1 - k]) for k in range(n)]
        for dist in (8, 4, 2, 1):
            for i in range(n):
                if i & dist == 0:
                    _exchange(xs, i, i + dist)
    return xs


def _router_kernel(ht_ref, wqt_ref, keys_ref, s_ref, e_ref, thr_ref, q_scr, *, tn):
    q_scr[...] = _dot(wqt_ref[...], ht_ref[...]).astype(BF16)
    row8 = lax.broadcasted_iota(jnp.int32, (8, tn), 0)
    for h in range(P_HEADS):
        tops, scores = [], []
        for p in range(2):
            hp = 2 * h + p
            s = _dot(keys_ref[hp], q_scr[hp * LANE:(hp + 1) * LANE, :])
            s_ref[hp] = s
            scores.append(s)
            tops.append(_top16_sorted(s))
        ta, tb = tops
        a_lo, a_hi, b_hi = (jnp.full((8, tn), NEG, F32) for _ in range(3))
        for i in range(8):
            a_lo = jnp.where(row8 == i, ta[i], a_lo)
            a_hi = jnp.where(row8 == i, ta[i + 8], a_hi)
            b_hi = jnp.where(row8 == i, tb[i + 8], b_hi)
        parts = [a_lo + tb[0], a_hi + tb[0], a_lo + tb[1]]
        for j in range(2, 8):
            parts.append(jnp.where(row8 < P_TOPK // (j + 1), a_lo + tb[j], NEG))
        parts.append(ta[0] + b_hi)
        cand = jnp.concatenate(parts, axis=0)
        best = _top_values(cand, P_TOPK)
        z = jnp.zeros_like(best[0])
        for c in best:
            z = z + jnp.exp(c - best[0])
        thr_ref[h:h + 1, :] = best[P_TOPK - 1]
        e_ref[2 * h] = jnp.exp(scores[0] - ta[0][0:1, :]) / z
        e_ref[2 * h + 1] = jnp.exp(scores[1] - tb[0][0:1, :])


def _router(ht, wqt, keys):
    tn = 256
    t = ht.shape[1]
    big = pl.BlockSpec((2 * P_HEADS, P_NKEYS, tn), lambda i: (0, 0, i))
    shp = jax.ShapeDtypeStruct((2 * P_HEADS, P_NKEYS, t), F32)
    return pl.pallas_call(
        functools.partial(_router_kernel, tn=tn), grid=(t // tn,),
        in_specs=[pl.BlockSpec((D, tn), lambda i: (0, i)),
                  pl.BlockSpec((D, D), lambda i: (0, 0)),
                  pl.BlockSpec((2 * P_HEADS, P_NKEYS, LANE), lambda i: (0, 0, 0))],
        out_specs=[big, big, pl.BlockSpec((P_HEADS, tn), lambda i: (0, i))],
        out_shape=[shp, shp, jax.ShapeDtypeStruct((P_HEADS, t), F32)],
        scratch_shapes=[pltpu.VMEM((D, tn), BF16)],
        compiler_params=_cparams(("parallel",)), name="peer_router",
    )(ht, wqt, keys)


P_EC = 1024
P_TN = 768


def _gelu(x):
    return 0.5 * x * (1.0 + lax.erf(x * (2.0 ** -0.5)))


P_HALF = P_EC // 2
P_STEPS = P_EXPERTS // P_EC + 1
P_GATE_ROWS = 32


def _expert_gates(z_ref, w_ref, s1_ref, e1_ref, row0, s2_ref, e2_ref, thr_ref):
    n_r = P_HALF // P_NKEYS
    for lb in range(P_TN // LANE):
        sl = slice(lb * LANE, (lb + 1) * LANE)
        for b0 in range(0, P_NKEYS, P_GATE_ROWS):
            gates = [jnp.zeros((P_GATE_ROWS, LANE), F32) for _ in range(n_r)]
            for h in range(P_HEADS):
                s2 = s2_ref[h, 0, b0:b0 + P_GATE_ROWS, sl]
                e2 = e2_ref[h, 0, b0:b0 + P_GATE_ROWS, sl]
                thr = thr_ref[h:h + 1, sl]
                for r in range(n_r):
                    pair = s1_ref[h, 0, row0 + r:row0 + r + 1, sl] + s2
                    w = e1_ref[h, 0, row0 + r:row0 + r + 1, sl] * e2
                    gates[r] = gates[r] + jnp.where(pair >= thr, w, 0.0)
            for r in range(n_r):
                rows = slice(r * P_NKEYS + b0, r * P_NKEYS + b0 + P_GATE_ROWS)
                w_ref[lb, rows, :] = (gates[r] * _gelu(z_ref[lb, rows, :])).astype(BF16)


def _expert_kernel(ht_ref, u_ref, vt_ref, s1p_ref, s1c_ref, s2_ref, e1p_ref, e1c_ref, e2_ref, thr_ref, y_ref,
                   z0, z1, w0, w1):
    j = pl.program_id(1)
    last = P_STEPS - 1
    half_rows = P_HALF // P_NKEYS

    n_lb = P_TN // LANE

    def stage_a(z, half):
        zf = _dot(u_ref[half * P_HALF:(half + 1) * P_HALF, :], ht_ref[...])
        for lb in range(n_lb):
            z[lb] = zf[:, lb * LANE:(lb + 1) * LANE]

    def stage_b(w, half, y_old):
        wf = jnp.concatenate([w[lb] for lb in range(n_lb)], axis=1)
        y_ref[...] = y_old + _dot(vt_ref[:, half * P_HALF:(half + 1) * P_HALF], wf)

    @pl.when(j == 0)
    def _():
        stage_a(z0, 0)
        stage_a(z1, 1)
        _expert_gates(z0, w0, s1c_ref, e1c_ref, 0, s2_ref, e2_ref, thr_ref)

    @pl.when((j > 0) & (j < last))
    def _():
        stage_a(z0, 0)
        _expert_gates(z1, w1, s1p_ref, e1p_ref, half_rows, s2_ref, e2_ref, thr_ref)
        stage_b(w0, 0, jnp.where(j == 1, 0.0, y_ref[...]))
        stage_a(z1, 1)
        _expert_gates(z0, w0, s1c_ref, e1c_ref, 0, s2_ref, e2_ref, thr_ref)
        stage_b(w1, 1, y_ref[...])

    @pl.when(j == last)
    def _():
        _expert_gates(z1, w1, s1p_ref, e1p_ref, half_rows, s2_ref, e2_ref, thr_ref)
        stage_b(w0, 0, y_ref[...])
        stage_b(w1, 1, y_ref[...])


def _experts(ht, u, vt, layer, s, e, thr):
    t = ht.shape[1]
    ac = P_EC // P_NKEYS
    n_blk = P_EXPERTS // P_EC
    s4 = s.reshape(P_HEADS, 2, P_NKEYS, t)
    e4 = e.reshape(P_HEADS, 2, P_NKEYS, t)
    once = pl.Buffered(1)
    cur = lambda j: jnp.minimum(j, n_blk - 1)
    prev = lambda j: jnp.maximum(j - 1, 0)
    first_p = pl.BlockSpec((P_HEADS, 1, ac, P_TN), lambda i, j: (0, 0, prev(j), i))
    first_c = pl.BlockSpec((P_HEADS, 1, ac, P_TN), lambda i, j: (0, 0, cur(j), i))
    second = pl.BlockSpec((P_HEADS, 1, P_NKEYS, P_TN), lambda i, j: (0, 1, 0, i), pipeline_mode=once)
    return pl.pallas_call(
        _expert_kernel, grid=(t // P_TN, P_STEPS),
        in_specs=[pl.BlockSpec((D, P_TN), lambda i, j: (0, i), pipeline_mode=once),
                  pl.BlockSpec((None, P_EC, D), lambda i, j: (layer, cur(j), 0)),
                  pl.BlockSpec((None, D, P_EC), lambda i, j: (layer, 0, prev(j))),
                  first_p, first_c, second, first_p, first_c, second,
                  pl.BlockSpec((P_HEADS, P_TN), lambda i, j: (0, i), pipeline_mode=once)],
        out_specs=pl.BlockSpec((D, P_TN), lambda i, j: (0, i)),
        out_shape=jax.ShapeDtypeStruct((D, t), F32),
        scratch_shapes=[pltpu.VMEM((P_TN // LANE, P_HALF, LANE), F32), pltpu.VMEM((P_TN // LANE, P_HALF, LANE), F32),
                        pltpu.VMEM((P_TN // LANE, P_HALF, LANE), BF16), pltpu.VMEM((P_TN // LANE, P_HALF, LANE), BF16)],
        compiler_params=_cparams(("parallel", "arbitrary")), name="peer_experts",
    )(ht, u, vt, s4, s4, s4, e4, e4, e4, thr)


def _peer(ht, wq, keys, u_all, vt_all, layer):
    s, e, thr = _router(ht, wq.T.astype(BF16), keys.reshape(2 * P_HEADS, P_NKEYS, LANE).astype(BF16))
    return _experts(ht, u_all, vt_all, layer, s, e, thr)


def _ab_weights(w_in, gate_b):
    aq, ak, av, ao, ag, bq, bk, bv = jnp.split(w_in, np.cumsum([512, 512, 1024, 1024, 32, 1024, 256])[:].tolist(), axis=1)
    qk_t = jnp.concatenate([aq, ak], axis=1).astype(BF16).T.reshape(2, A_HEADS, A_DK, D)
    qk = qk_t.transpose(1, 0, 2, 3).reshape(2 * A_HEADS * A_DK, D).T
    pad = jnp.zeros((D, AB_N - COL_G - 32), BF16)
    w = jnp.concatenate([qk] + [a.astype(BF16) for a in (av, ao, bq, bk, bv, ag)] + [pad], axis=1)
    bias = jnp.zeros((1, AB_N), F32).at[0, COL_G:COL_G + 32].set(gate_b.astype(F32))
    return w, bias


def _mixer_ab(h, w_in, gate_b, mh_g, sink, rope_tabs, win_bias):
    w, bias = _ab_weights(w_in, gate_b)
    p = _matmul(h, w, bias=bias)
    hf = _mlstm(p, 0)
    hb = _mlstm(p, 1)
    ya = _aout(hf, hb, p, mh_g)
    qr, kr = _rope(p, *rope_tabs)
    sink_b = jnp.repeat(sink.astype(F32), LANE).reshape(1, B_HEADS * LANE)
    yb_l = _attn_latent(qr, 0, 0, kr, 0, 0, p, COL_BV // LANE, N_CTX, p, COL_BK // LANE, p, COL_BV // LANE,
                        win_bias, sink_b, n_heads=B_HEADS, n_kv=B_KV, kvps=1, mq=B_WINDOW, wk=3 * B_WINDOW,
                        back=B_WINDOW, name="window_attn")
    yb_c = _attn_context(p, COL_BQ // (4 * LANE), p, COL_BK // LANE, p, COL_BV // LANE, sink_b,
                         n_heads=B_HEADS, n_kv=B_KV, name="window_attn_ctx")
    return jnp.concatenate([ya, jnp.concatenate([yb_c, yb_l], axis=0)], axis=1)


def _mixer_c(h, w_in, rpb):
    p = _matmul(h, w_in.astype(BF16), out_dtype=BF16)
    bias = _na_table(rpb)
    mq = NA_ROWS * GRID_W
    y_l = _attn_latent(p, N_CTX // mq, 0, p, C_HEADS, N_CTX, p, 2 * C_HEADS, N_CTX, p, C_HEADS, p, 2 * C_HEADS,
                       bias, None, n_heads=C_HEADS, n_kv=C_HEADS, kvps=NA_HEADS_PER_STEP, mq=mq, wk=NA_BAND * GRID_W,
                       back=(NA_KH // 2) * GRID_W, name="na_attn")
    y_c = _attn_context(p, 0, p, C_HEADS, p, 2 * C_HEADS, None, n_heads=C_HEADS, n_kv=C_HEADS, name="na_attn_ctx")
    return jnp.concatenate([y_c, y_l], axis=0)


def _mod_rows(m6, i_shift, i_scale, i_gate):
    z = jnp.zeros((D,), F32)
    pick = lambda r, i: m6[r, i] if i is not None else z
    return jnp.stack([pick(0, i_shift), pick(0, i_scale), pick(1, i_shift), pick(1, i_scale),
                      pick(0, i_gate), pick(1, i_gate), z, z])


def kernel(x, c, ctx, c_ctx, ada_w, ada_b, norm1_g, norm2_g, ab_w_in, ab_gate_b, ab_mh_g, ab_sink, ab_w_out,
           na_w_in, na_rpb, na_w_out, peer_wq, peer_keys, peer_u, peer_v, final_g):
    xs = jnp.concatenate([ctx[0], x[0]], axis=0).astype(F32)
    cc = jnp.zeros((16, D), F32).at[0].set(c[0]).at[1].set(c_ctx)
    mods = _adaln(cc, ada_w, ada_b)[:, :2].reshape(DEPTH, 2, 6, D)
    rope_tabs = _rope_tables()
    win_bias = _window_bias()
    u_all = peer_u.astype(BF16)
    vt_all = peer_v.transpose(0, 2, 1).astype(BF16)

    h = _norm(xs, norm1_g[0], _mod_rows(mods[0], 0, 1, None))
    for l in range(DEPTH):
        m6 = mods[l]
        if l % 2 == 0:
            e = l // 2
            ymix = _mixer_ab(h, ab_w_in[e], ab_gate_b[e], ab_mh_g[e], ab_sink[e], rope_tabs, win_bias)
            w_out = ab_w_out[e]
        else:
            o = l // 2
            ymix = _mixer_c(h, na_w_in[o], na_rpb[o])
            w_out = na_w_out[o]
        gv = jnp.concatenate([m6[:, 2], jnp.zeros((6, D), F32)], axis=0)
        xs = _matmul(ymix, w_out.astype(BF16), resid=xs, gates=gv)
        h2t = _norm(xs, norm2_g[l], _mod_rows(m6, 3, 4, None), h_t=True)
        yt = _peer(h2t, peer_wq[l], peer_keys[l], u_all, vt_all, l)
        if l + 1 < DEPTH:
            mv = _mod_rows(mods[l + 1], 0, 1, None).at[4].set(m6[0, 5]).at[5].set(m6[1, 5])
            xs, h = _norm(xs, norm1_g[l + 1], mv, y=yt, y_t=True)
        else:
            mv = _mod_rows(m6, None, None, 5)
            _, out = _norm(xs, final_g, mv, y=yt, y_t=True, out_dtype=F32, row_off=N_CTX // 256)
    return out[None]
```

```python
import functools

import numpy as np
import jax
import jax.numpy as jnp
from jax import lax
from jax.experimental import pallas as pl
from jax.experimental.pallas import tpu as pltpu

F32 = jnp.float32
BF16 = jnp.bfloat16

D = 2048
N_LAT = 8192
N_CTX = 256
N_TOK = N_CTX + N_LAT
DEPTH = 4
GRID_W = 64
GRID_H = N_LAT // GRID_W
EPS = 1e-6
LANE = 128
NEG = -1e30

A_HEADS = 8
A_DK = 64
A_DV = 128
A_CHUNK = 64
A_SPAN = 256
B_HEADS = 8
B_KV = 2
B_WINDOW = 128
ROPE_THETA = 10000.0
C_HEADS = 16
NA_KH = 8
NA_KW = 16
NA_ROWS = 4
NA_BAND = NA_ROWS + NA_KH - 1
NA_HEADS_PER_STEP = 4
P_HEADS = 8
P_NKEYS = 128
P_TOPK = 16
P_EXPERTS = P_NKEYS * P_NKEYS

AB_N = 5120
COL_QK, COL_AV, COL_AO, COL_BQ, COL_BK, COL_BV, COL_G = 0, 1024, 2048, 3072, 4096, 4352, 4608

VMEM_LIMIT = 52 * 1024 * 1024


def _cparams(sem):
    return pltpu.CompilerParams(dimension_semantics=sem, vmem_limit_bytes=VMEM_LIMIT)


def _dot(a, b):
    return jnp.dot(a, b, preferred_element_type=F32)


def _dot_nt(a, b):
    return lax.dot_general(a, b, (((1,), (1,)), ((), ())), preferred_element_type=F32)


def _ada_kernel(c_ref, w_ref, b_ref, o_ref):
    c = c_ref[...]
    s = c / (1.0 + jnp.exp(-c))
    w = w_ref[0]
    s_hi = s.astype(BF16)
    s_lo = (s - s_hi.astype(F32)).astype(BF16)
    w_hi = w.astype(BF16)
    w_lo = (w - w_hi.astype(F32)).astype(BF16)
    o_ref[0] = _dot(s_hi, w_hi) + _dot(s_lo, w_hi) + _dot(s_hi, w_lo) + b_ref[0]


def _adaln(cc, ada_w, ada_b):
    tn = 1024
    n = ada_w.shape[-1]
    return pl.pallas_call(
        _ada_kernel,
        grid=(DEPTH, n // tn),
        in_specs=[pl.BlockSpec((16, D), lambda l, j: (0, 0)),
                  pl.BlockSpec((1, D, tn), lambda l, j: (l, 0, j)),
                  pl.BlockSpec((1, 1, tn), lambda l, j: (l, 0, j))],
        out_specs=pl.BlockSpec((1, 16, tn), lambda l, j: (l, 0, j)),
        out_shape=jax.ShapeDtypeStruct((DEPTH, 16, n), F32),
        compiler_params=_cparams(("arbitrary", "arbitrary")),
        name="adaln",
    )(cc, ada_w, ada_b.reshape(DEPTH, 1, n))


def _norm_kernel(*refs, has_resid, y_t, h_t, n_ctx, tm, row_off):
    if has_resid:
        x_ref, y_ref, g_ref, mv_ref, xo_ref, h_ref = refs
    else:
        x_ref, g_ref, mv_ref, h_ref = refs
    row = (pl.program_id(0) + row_off) * tm + lax.broadcasted_iota(jnp.int32, (tm, 1), 0)
    is_ctx = row < n_ctx
    x = x_ref[...]
    if has_resid:
        gate = jnp.where(is_ctx, mv_ref[5:6, :], mv_ref[4:5, :])
        y = y_ref[...].T if y_t else y_ref[...]
        x = x + gate * y
        xo_ref[...] = x
    ms = jnp.mean(x * x, axis=-1, keepdims=True)
    yn = (x * lax.rsqrt(ms + EPS)) * g_ref[...]
    shift = jnp.where(is_ctx, mv_ref[2:3, :], mv_ref[0:1, :])
    scale = jnp.where(is_ctx, mv_ref[3:4, :], mv_ref[1:2, :])
    h = yn * (1.0 + scale) + shift
    h_ref[...] = (h.T if h_t else h).astype(h_ref.dtype)


def _norm(x, g, mv, *, y=None, y_t=False, h_t=False, out_dtype=BF16, row_off=0):
    tm = 256
    n_rows = x.shape[0] - row_off * tm
    blk = pl.BlockSpec((tm, D), lambda i: (i + row_off, 0))
    oblk = pl.BlockSpec((tm, D), lambda i: (i, 0))
    yblk = pl.BlockSpec((D, tm), lambda i: (0, i + row_off)) if y_t else blk
    hblk = pl.BlockSpec((D, tm), lambda i: (0, i)) if h_t else oblk
    vec = pl.BlockSpec((1, D), lambda i: (0, 0))
    mvs = pl.BlockSpec((8, D), lambda i: (0, 0))
    has_resid = y is not None
    kern = functools.partial(_norm_kernel, has_resid=has_resid, y_t=y_t, h_t=h_t, n_ctx=N_CTX, tm=tm, row_off=row_off)
    h_shape = jax.ShapeDtypeStruct((D, n_rows) if h_t else (n_rows, D), out_dtype)
    if has_resid:
        return pl.pallas_call(
            kern, grid=(n_rows // tm,),
            in_specs=[blk, yblk, vec, mvs], out_specs=[oblk, hblk],
            out_shape=[jax.ShapeDtypeStruct((n_rows, D), F32), h_shape],
            compiler_params=_cparams(("parallel",)), name="resid_norm",
        )(x, y, g.reshape(1, D), mv)
    return pl.pallas_call(
        kern, grid=(n_rows // tm,),
        in_specs=[blk, vec, mvs], out_specs=hblk, out_shape=h_shape,
        compiler_params=_cparams(("parallel",)), name="norm",
    )(x, g.reshape(1, D), mv)


MM_TM, MM_TN = 768, 1024


def _mm_kernel(*refs, has_bias, has_resid, n_ctx, tm):
    a_ref, w_ref = refs[0], refs[1]
    o_ref = refs[-1]
    acc = _dot(a_ref[...], w_ref[...])
    k = 2
    if has_bias:
        acc = acc + refs[k][...]
        k += 1
    if has_resid:
        x_ref, gv_ref = refs[k], refs[k + 1]
        row = pl.program_id(0) * tm + lax.broadcasted_iota(jnp.int32, (tm, 1), 0)
        gate = jnp.where(row < n_ctx, gv_ref[1:2, :], gv_ref[0:1, :])
        acc = x_ref[...] + gate * acc
    o_ref[...] = acc.astype(o_ref.dtype)


def _matmul(a, w, *, bias=None, resid=None, gates=None, out_dtype=F32, tm=MM_TM, tn=MM_TN):
    m, k = a.shape
    n = w.shape[1]
    in_specs = [pl.BlockSpec((tm, k), lambda i, j: (i, 0)), pl.BlockSpec((k, tn), lambda i, j: (0, j))]
    args = [a, w]
    if bias is not None:
        in_specs.append(pl.BlockSpec((1, tn), lambda i, j: (0, j)))
        args.append(bias)
    if resid is not None:
        in_specs += [pl.BlockSpec((tm, tn), lambda i, j: (i, j)), pl.BlockSpec((8, tn), lambda i, j: (0, j))]
        args += [resid, gates]
    kern = functools.partial(_mm_kernel, has_bias=bias is not None, has_resid=resid is not None, n_ctx=N_CTX, tm=tm)
    return pl.pallas_call(
        kern, grid=(m // tm, n // tn), in_specs=in_specs,
        out_specs=pl.BlockSpec((tm, tn), lambda i, j: (i, j)),
        out_shape=jax.ShapeDtypeStruct((m, n), out_dtype),
        compiler_params=_cparams(("parallel", "arbitrary")), name="matmul",
    )(*args)


def _rope_tile(x, cos, sin):
    lane = lax.broadcasted_iota(jnp.int32, x.shape, 1)
    partner = jnp.where((lane % 64) < 32, pltpu.roll(x, 96, axis=1), pltpu.roll(x, 32, axis=1))
    return x * cos + partner * sin


def _rope_kernel(q_ref, k_ref, cos_ref, sin_ref, qo_ref, ko_ref):
    cos, sin = cos_ref[...], sin_ref[...]
    for h in range(B_HEADS):
        sl = slice(h * LANE, (h + 1) * LANE)
        qo_ref[:, sl] = _rope_tile(q_ref[:, sl], cos, sin).astype(qo_ref.dtype)
    for h in range(B_KV):
        sl = slice(h * LANE, (h + 1) * LANE)
        ko_ref[:, sl] = _rope_tile(k_ref[:, sl], cos, sin).astype(ko_ref.dtype)


def _rope_tables():
    t = jnp.arange(N_LAT)
    freqs = ROPE_THETA ** (-jnp.arange(32, dtype=F32) / 32)
    ar = (t // GRID_W).astype(F32)[:, None] * freqs[None, :]
    ac = (t % GRID_W).astype(F32)[:, None] * freqs[None, :]
    cos = jnp.concatenate([jnp.cos(ar), jnp.cos(ar), jnp.cos(ac), jnp.cos(ac)], axis=1)
    sin = jnp.concatenate([-jnp.sin(ar), jnp.sin(ar), -jnp.sin(ac), jnp.sin(ac)], axis=1)
    return cos, sin


def _rope(p, cos, sin):
    tr = 256
    off = N_CTX // tr
    return pl.pallas_call(
        _rope_kernel, grid=(N_LAT // tr,),
        in_specs=[pl.BlockSpec((tr, 1024), lambda i: (i + off, COL_BQ // 1024)),
                  pl.BlockSpec((tr, 256), lambda i: (i + off, COL_BK // 256)),
                  pl.BlockSpec((tr, LANE), lambda i: (i, 0)),
                  pl.BlockSpec((tr, LANE), lambda i: (i, 0))],
        out_specs=[pl.BlockSpec((tr, 1024), lambda i: (i, 0)), pl.BlockSpec((tr, 256), lambda i: (i, 0))],
        out_shape=[jax.ShapeDtypeStruct((N_LAT, 1024), BF16), jax.ShapeDtypeStruct((N_LAT, 256), BF16)],
        compiler_params=_cparams(("parallel",)), name="rope",
    )(p, p, cos, sin)


def _na_bias_block(b_ref, kk, i):
    lane = lax.broadcasted_iota(jnp.int32, (GRID_W, LANE), 1)
    r0 = i * NA_ROWS
    ub = jnp.clip(r0 - NA_KH // 2, 0, GRID_H - NA_BAND)
    rows = []
    for ri in range(NA_ROWS):
        r = r0 + ri
        rs = jnp.clip(r - NA_KH // 2, 0, GRID_H - NA_KH)
        idx = []
        for a in range(NA_BAND):
            krow = ub + a
            ok = (krow >= rs) & (krow < rs + NA_KH)
            idx.append(jnp.where(ok, krow - r + (NA_KH - 1), 2 * NA_KH - 1))
        pieces = [jnp.where(lane < GRID_W, b_ref[kk, idx[a]], b_ref[kk, idx[a + 1]]) for a in range(0, NA_BAND - 1, 2)]
        pieces.append(b_ref[kk, idx[NA_BAND - 1]][:, :GRID_W])
        rows.append(jnp.concatenate(pieces, axis=1))
    return jnp.concatenate(rows, axis=0)


def _attn_kernel(*refs, g, kvps, bias_per_head, na_table, mq, wk, back, n_keys, k_off, v_off, has_band, has_sink,
                 scale):
    refs = list(refs)
    q_ref = refs.pop(0)
    if has_band:
        k_ref, v_ref = refs.pop(0), refs.pop(0)
    kc_ref, vc_ref = refs.pop(0), refs.pop(0)
    if has_band:
        b_ref = refs.pop(0)
    if has_sink:
        s_ref = refs.pop(0)
    o_ref = refs.pop(0)

    if has_band:
        i = pl.program_id(1)
        ub = pl.multiple_of(jnp.clip(i * mq - back, 0, n_keys - wk), 64)
    for hq in range(kvps * g):
        kk = hq // g
        ksl = slice(kk * LANE, (kk + 1) * LANE)
        kc = kc_ref[:, ksl].astype(BF16)
        vc = vc_ref[:, ksl].astype(BF16)
        if has_band:
            kb = k_ref[pl.ds(k_off + ub, wk), ksl].astype(BF16)
            vb = v_ref[pl.ds(v_off + ub, wk), ksl].astype(BF16)
            bias = _na_bias_block(b_ref, kk, i) if na_table else b_ref[kk if bias_per_head else 0, 0]
        hh = hq
        sl = slice(hh * LANE, (hh + 1) * LANE)
        q = q_ref[:, sl].astype(BF16)
        s_ctx = _dot_nt(q, kc) * scale
        m = jnp.max(s_ctx, axis=-1, keepdims=True)
        if has_band:
            s_loc = _dot_nt(q, kb) * scale + bias
            m = jnp.maximum(m, jnp.max(s_loc, axis=-1, keepdims=True))
        if has_sink:
            snk = jnp.max(s_ref[:, sl], axis=1, keepdims=True)
            m = jnp.maximum(m, snk)
        p_ctx = jnp.exp(s_ctx - m)
        den = jnp.sum(p_ctx, axis=-1, keepdims=True)
        acc = _dot(p_ctx.astype(BF16), vc)
        if has_band:
            p_loc = jnp.exp(s_loc - m)
            den = den + jnp.sum(p_loc, axis=-1, keepdims=True)
            acc = acc + _dot(p_loc.astype(BF16), vb)
        if has_sink:
            den = den + jnp.exp(snk - m)
        o_ref[:, sl] = (acc / den).astype(o_ref.dtype)


def _pattern(i, n):
    return jnp.where(i == 0, 0, jnp.where(i == n - 1, 2, 1))


def _attn_latent(q, q_blk0, q_col0, k, k_col0, k_off, v, v_col0, v_off, kc, kc_col0, vc, vc_col0,
                 bias, sink, *, n_heads, n_kv, kvps, mq, wk, back, name):
    g = n_heads // n_kv
    nq = N_LAT // mq
    per_head_bias = bias.shape[0] > 1
    na_table = bias.shape[1] == 2 * NA_KH
    kw, qw = kvps * LANE, kvps * g * LANE
    if na_table:
        bias_spec = pl.BlockSpec((kvps,) + bias.shape[1:], lambda j, i: (j, 0, 0, 0))
    else:
        bias_spec = pl.BlockSpec((kvps if per_head_bias else 1, 1, mq, wk),
                                 lambda j, i: (j if per_head_bias else 0, _pattern(i, nq), 0, 0))
    in_specs = [
        pl.BlockSpec((mq, qw), lambda j, i: (i + q_blk0, q_col0 // (kvps * g) + j)),
        pl.BlockSpec((k.shape[0], kw), lambda j, i: (0, k_col0 // kvps + j), pipeline_mode=pl.Buffered(1)),
        pl.BlockSpec((v.shape[0], kw), lambda j, i: (0, v_col0 // kvps + j), pipeline_mode=pl.Buffered(1)),
        pl.BlockSpec((N_CTX, kw), lambda j, i: (0, kc_col0 // kvps + j)),
        pl.BlockSpec((N_CTX, kw), lambda j, i: (0, vc_col0 // kvps + j)),
        bias_spec,
    ]
    args = [q, k, v, kc, vc, bias]
    if sink is not None:
        in_specs.append(pl.BlockSpec((1, qw), lambda j, i: (0, j)))
        args.append(sink)
    kern = functools.partial(_attn_kernel, g=g, kvps=kvps, bias_per_head=per_head_bias, na_table=na_table,
                             mq=mq, wk=wk, back=back,
                             n_keys=N_LAT, k_off=k_off, v_off=v_off, has_band=True, has_sink=sink is not None,
                             scale=LANE ** -0.5)
    return pl.pallas_call(
        kern, grid=(n_kv // kvps, nq), in_specs=in_specs,
        out_specs=pl.BlockSpec((mq, qw), lambda j, i: (i, j)),
        out_shape=jax.ShapeDtypeStruct((N_LAT, n_heads * LANE), BF16),
        compiler_params=_cparams(("arbitrary", "arbitrary")), name=name,
    )(*args)


def _attn_context(q, q_col0, kc, kc_col0, vc, vc_col0, sink, *, n_heads, n_kv, name):
    g = n_heads // n_kv
    in_specs = [
        pl.BlockSpec((N_CTX, g * LANE), lambda j, i: (0, q_col0 + j)),
        pl.BlockSpec((N_CTX, LANE), lambda j, i: (0, kc_col0 + j)),
        pl.BlockSpec((N_CTX, LANE), lambda j, i: (0, vc_col0 + j)),
    ]
    args = [q, kc, vc]
    if sink is not None:
        in_specs.append(pl.BlockSpec((1, g * LANE), lambda j, i: (0, j)))
        args.append(sink)
    kern = functools.partial(_attn_kernel, g=g, kvps=1, bias_per_head=False, na_table=False, mq=N_CTX, wk=0, back=0, n_keys=0,
                             k_off=0, v_off=0, has_band=False, has_sink=sink is not None, scale=LANE ** -0.5)
    return pl.pallas_call(
        kern, grid=(n_kv, 1), in_specs=in_specs,
        out_specs=pl.BlockSpec((N_CTX, g * LANE), lambda j, i: (0, j)),
        out_shape=jax.ShapeDtypeStruct((N_CTX, n_heads * LANE), BF16),
        compiler_params=_cparams(("arbitrary", "arbitrary")), name=name,
    )(*args)


def _window_bias():
    t, w = N_LAT, B_WINDOW
    nb = t // w
    out = []
    for bi in (0, 1, nb - 1):
        ub = min(max(bi * w - w, 0), t - 3 * w)
        qpos = bi * w + np.arange(w)[:, None]
        kpos = ub + np.arange(3 * w)[None, :]
        out.append(np.where(np.abs(kpos - qpos) <= w, 0.0, NEG))
    return jnp.asarray(np.stack(out)[None], F32)


def _na_table(rpb):
    col = np.arange(GRID_W)
    c0 = np.clip(col - NA_KW // 2, 0, GRID_W - NA_KW)
    col_ok = (col[None, :] >= c0[:, None]) & (col[None, :] < c0[:, None] + NA_KW)
    dc = np.clip(col[None, :] - col[:, None] + (NA_KW - 1), 0, 2 * NA_KW - 2)
    onehot = (dc[None] == np.arange(2 * NA_KW - 1)[:, None, None]).astype(np.float32)
    t = jnp.einsum("hrd,dck->hrck", rpb.astype(F32), jnp.asarray(onehot), precision=lax.Precision.HIGHEST)
    t = jnp.where(col_ok[None, None], t, NEG)
    t = jnp.concatenate([t, jnp.full((C_HEADS, 1, GRID_W, GRID_W), NEG, F32)], axis=1)
    return jnp.concatenate([t, t], axis=-1)


def _log_sigmoid(x):
    return jnp.minimum(x, 0.0) - jnp.log1p(jnp.exp(-jnp.abs(x)))


def _mlstm_kernel(qk_ref, v_ref, g_ref, h_ref, c_ref, m_ref, *, d):
    L = A_CHUNK
    P2 = 2 * L

    @pl.when(pl.program_id(0) == 0)
    def _():
        c_ref[...] = jnp.zeros_like(c_ref)
        m_ref[...] = jnp.zeros_like(m_ref)

    ri = lax.broadcasted_iota(jnp.int32, (L, P2), 0)
    ci = lax.broadcasted_iota(jnp.int32, (L, P2), 1)
    lane = lax.broadcasted_iota(jnp.int32, (1, P2), 1)
    ones_blk = jnp.ones((P2, LANE), F32)
    i_lane, f_lane = 2 * d * A_HEADS, (2 * d + 1) * A_HEADS
    order = (0, 1) if d == 0 else (1, 0)
    rp = lax.broadcasted_iota(jnp.int32, (P2, P2), 0)
    cp = lax.broadcasted_iota(jnp.int32, (P2, P2), 1)
    before = (rp <= cp) if d == 0 else (rp >= cp)
    cum_mat = (before & ((rp >= L) == (cp >= L))).astype(BF16)

    def pair(pidx, carry):
        pp = pidx if d == 0 else A_SPAN // P2 - 1 - pidx
        r0 = pl.multiple_of(pp * P2, P2)
        gt = g_ref[pl.ds(r0, P2), :].T[:4 * A_HEADS, :]
        f_all = _log_sigmoid(gt)
        f_hi = f_all.astype(BF16)
        f_r1 = f_all - f_hi.astype(F32)
        f_mid = f_r1.astype(BF16)
        f_lo = (f_r1 - f_mid.astype(F32)).astype(BF16)
        cum_all = _dot(f_hi, cum_mat) + _dot(f_mid, cum_mat) + _dot(f_lo, cum_mat)
        c_state = [c_ref[h] for h in range(A_HEADS)]
        m_state = [m_ref[h, 0:1, :] for h in range(A_HEADS)]
        h_out = []
        for h in range(A_HEADS):
            sl = slice(h * LANE, (h + 1) * LANE)
            qk = qk_ref[pl.ds(r0, P2), sl]
            kt = qk.T[A_DK:, :]
            v_aug = jnp.concatenate([v_ref[pl.ds(r0, P2), sl], ones_blk], axis=1).astype(BF16)
            i_row = gt[i_lane + h:i_lane + h + 1, :]
            f_row = f_all[f_lane + h:f_lane + h + 1, :]
            cum_row = cum_all[f_lane + h:f_lane + h + 1, :]
            for sub in order:
                rows = slice(sub * L, (sub + 1) * L)
                own = (lane >= sub * L) & (lane < (sub + 1) * L)
                cs = ci - sub * L
                seen = ((cs <= ri) & (cs >= 0)) if d == 0 else ((cs >= ri) & (cs < L))
                q = (qk[rows, :A_DK] * (A_DK ** -0.5)).astype(BF16)
                cum_col = jnp.sum(jnp.where(seen, f_row, 0.0), axis=1, keepdims=True)
                total = jnp.sum(jnp.where(own, f_row, 0.0), axis=1, keepdims=True)
                m_old, c_old = m_state[h], c_state[h]
                dm = jnp.where(seen, cum_col - cum_row + i_row, NEG)
                inter = cum_col + m_old
                mt = jnp.maximum(inter, jnp.max(dm, axis=1, keepdims=True))
                sw = _dot(q, kt.astype(BF16)) * jnp.exp(dm - mt)
                a = jnp.exp(inter - mt)
                c_bf = c_old.astype(BF16)
                na = _dot(sw.astype(BF16), v_aug)
                num = na[:, :LANE] + a * _dot(q, c_bf[:, :LANE])
                den = na[:, LANE:] + a * _dot(q, c_bf[:, LANE:])
                h_out.append((sub, sl, num / jnp.maximum(jnp.abs(den), jnp.exp(-mt))))
                wend = jnp.where(own, total - cum_row + i_row, NEG)
                m_new = jnp.maximum(total + m_old, jnp.max(wend, axis=1, keepdims=True))
                decay = jnp.exp(total + m_old - m_new)
                wv = jnp.exp(wend - m_new)
                c_state[h] = jnp.concatenate([decay, decay], axis=1) * c_old + _dot((kt * wv).astype(BF16), v_aug)
                m_state[h] = m_new
        for sub, sl, val in h_out:
            h_ref[pl.ds(r0 + sub * L, L), sl] = val
        for h in range(A_HEADS):
            c_ref[h] = c_state[h]
            m_ref[h] = jnp.broadcast_to(m_state[h], (8, LANE))
        return carry

    lax.fori_loop(0, A_SPAN // P2, pair, 0)


def _mlstm(p, d):
    n_span = N_TOK // A_SPAN
    if d == 0:
        span = lambda s: s
    else:
        span = lambda s: jnp.where(s == 0, 0, n_span - s)
    return pl.pallas_call(
        functools.partial(_mlstm_kernel, d=d), grid=(n_span,),
        in_specs=[pl.BlockSpec((A_SPAN, 1024), lambda s: (span(s), COL_QK // 1024)),
                  pl.BlockSpec((A_SPAN, 1024), lambda s: (span(s), COL_AV // 1024)),
                  pl.BlockSpec((A_SPAN, LANE), lambda s: (span(s), COL_G // LANE))],
        out_specs=pl.BlockSpec((A_SPAN, 1024), lambda s: (span(s), 0)),
        out_shape=jax.ShapeDtypeStruct((N_TOK, 1024), F32),
        scratch_shapes=[pltpu.VMEM((A_HEADS, A_DK, 2 * LANE), F32), pltpu.VMEM((A_HEADS, 8, LANE), F32)],
        compiler_params=_cparams(("arbitrary",)), name="mlstm_fwd" if d == 0 else "mlstm_bwd",
    )(p, p, p)


def _aout_kernel(hf_ref, hb_ref, o_ref, g_ref, y_ref):
    for h in range(A_HEADS):
        sl = slice(h * LANE, (h + 1) * LANE)
        x = hf_ref[:, sl] + hb_ref[:, sl]
        x = x * lax.rsqrt(jnp.mean(x * x, axis=-1, keepdims=True) + EPS)
        o = o_ref[:, sl]
        y_ref[:, sl] = ((x * g_ref[:, sl]) * (1.0 / (1.0 + jnp.exp(-o)))).astype(y_ref.dtype)


def _aout(hf, hb, p, mh_g):
    tm = 256
    blk = pl.BlockSpec((tm, 1024), lambda i: (i, 0))
    return pl.pallas_call(
        _aout_kernel, grid=(N_TOK // tm,),
        in_specs=[blk, blk, pl.BlockSpec((tm, 1024), lambda i: (i, COL_AO // 1024)),
                  pl.BlockSpec((1, 1024), lambda i: (0, 0))],
        out_specs=blk, out_shape=jax.ShapeDtypeStruct((N_TOK, 1024), BF16),
        compiler_params=_cparams(("parallel",)), name="mlstm_out",
    )(hf, hb, p, mh_g.reshape(1, 1024))


def _top_values(s, k):
    vals = []
    cur = s
    for _ in range(k):
        mx = jnp.max(cur, axis=0, keepdims=True)
        vals.append(mx)
        cur = jnp.where(cur == mx, NEG, cur)
    return vals


def _sort_network(n):
    def merge(lo, hi, r):
        step = r * 2
        if step < hi - lo:
            yield from merge(lo, hi, step)
            yield from merge(lo + r, hi, step)
            yield from ((i, i + r) for i in range(lo + r, hi - r, step))
        else:
            yield (lo, lo + r)

    def sort(lo, hi):
        if hi - lo >= 1:
            mid = lo + (hi - lo) // 2
            yield from sort(lo, mid)
            yield from sort(mid + 1, hi)
            yield from merge(lo, hi, 1)

    return list(sort(0, n - 1))


def _exchange(xs, i, j):
    xs[i], xs[j] = jnp.maximum(xs[i], xs[j]), jnp.minimum(xs[i], xs[j])


def _top16_sorted(s):
    n = P_TOPK
    xs = [s[n_ * 8:(n_ + 1) * 8, :] for n_ in range(n)]
    for i, j in _sort_network(n):
        _exchange(xs, i, j)
    for shift in (4, 2, 1):
        other = [pltpu.roll(x, shift, axis=0) for x in xs]
        xs = [jnp.maximum(xs[k], other[n - 1 - k]) for k in range(n)]
        for dist in (8, 4, 2, 1):
            for i in range(n):
                if i & dist == 0:
                    _exchange(xs, i, i + dist)
    return xs


def _router_kernel(ht_ref, wqt_ref, keys_ref, s_ref, e_ref, thr_ref, q_scr, *, tn):
    q_scr[...] = _dot(wqt_ref[...], ht_ref[...]).astype(BF16)
    row8 = lax.broadcasted_iota(jnp.int32, (8, tn), 0)
    for h in range(P_HEADS):
        tops, scores = [], []
        for p in range(2):
            hp = 2 * h + p
            s = _dot(keys_ref[hp], q_scr[hp * LANE:(hp + 1) * LANE, :])
            s_ref[hp] = s
            scores.append(s)
            tops.append(_top16_sorted(s))
        ta, tb = tops
        a_lo, a_hi, b_hi = (jnp.full((8, tn), NEG, F32) for _ in range(3))
        for i in range(8):
            a_lo = jnp.where(row8 == i, ta[i], a_lo)
            a_hi = jnp.where(row8 == i, ta[i + 8], a_hi)
            b_hi = jnp.where(row8 == i, tb[i + 8], b_hi)
        parts = [a_lo + tb[0], a_hi + tb[0], a_lo + tb[1]]
        for j in range(2, 8):
            parts.append(jnp.where(row8 < P_TOPK // (j + 1), a_lo + tb[j], NEG))
        parts.append(ta[0] + b_hi)
        cand = jnp.concatenate(parts, axis=0)
        best = _top_values(cand, P_TOPK)
        z = jnp.zeros_like(best[0])
        for c in best:
            z = z + jnp.exp(c - best[0])
        thr_ref[h:h + 1, :] = best[P_TOPK - 1]
        e_ref[2 * h] = jnp.exp(scores[0] - ta[0][0:1, :]) / z
        e_ref[2 * h + 1] = jnp.exp(scores[1] - tb[0][0:1, :])


def _router(ht, wqt, keys):
    tn = 256
    t = ht.shape[1]
    big = pl.BlockSpec((2 * P_HEADS, P_NKEYS, tn), lambda i: (0, 0, i))
    shp = jax.ShapeDtypeStruct((2 * P_HEADS, P_NKEYS, t), F32)
    return pl.pallas_call(
        functools.partial(_router_kernel, tn=tn), grid=(t // tn,),
        in_specs=[pl.BlockSpec((D, tn), lambda i: (0, i)),
                  pl.BlockSpec((D, D), lambda i: (0, 0)),
                  pl.BlockSpec((2 * P_HEADS, P_NKEYS, LANE), lambda i: (0, 0, 0))],
        out_specs=[big, big, pl.BlockSpec((P_HEADS, tn), lambda i: (0, i))],
        out_shape=[shp, shp, jax.ShapeDtypeStruct((P_HEADS, t), F32)],
        scratch_shapes=[pltpu.VMEM((D, tn), BF16)],
        compiler_params=_cparams(("parallel",)), name="peer_router",
    )(ht, wqt, keys)


P_EC = 1024
P_TN = 768


def _gelu(x):
    return 0.5 * x * (1.0 + lax.erf(x * (2.0 ** -0.5)))


P_HALF = P_EC // 2
P_STEPS = P_EXPERTS // P_EC + 1
P_GATE_ROWS = 32


def _expert_gates(z_ref, w_ref, s1_ref, e1_ref, row0, s2_ref, e2_ref, thr_ref):
    n_r = P_HALF // P_NKEYS
    for lb in range(P_TN // LANE):
        sl = slice(lb * LANE, (lb + 1) * LANE)
        for b0 in range(0, P_NKEYS, P_GATE_ROWS):
            gates = [jnp.zeros((P_GATE_ROWS, LANE), F32) for _ in range(n_r)]
            for h in range(P_HEADS):
                s2 = s2_ref[h, 0, b0:b0 + P_GATE_ROWS, sl]
                e2 = e2_ref[h, 0, b0:b0 + P_GATE_ROWS, sl]
                thr = thr_ref[h:h + 1, sl]
                for r in range(n_r):
                    pair = s1_ref[h, 0, row0 + r:row0 + r + 1, sl] + s2
                    w = e1_ref[h, 0, row0 + r:row0 + r + 1, sl] * e2
                    gates[r] = gates[r] + jnp.where(pair >= thr, w, 0.0)
            for r in range(n_r):
                rows = slice(r * P_NKEYS + b0, r * P_NKEYS + b0 + P_GATE_ROWS)
                w_ref[lb, rows, :] = (gates[r] * _gelu(z_ref[lb, rows, :])).astype(BF16)


def _expert_kernel(ht_ref, u_ref, vt_ref, s1p_ref, s1c_ref, s2_ref, e1p_ref, e1c_ref, e2_ref, thr_ref, y_ref,
                   z0, z1, w0, w1):
    j = pl.program_id(1)
    last = P_STEPS - 1
    half_rows = P_HALF // P_NKEYS

    n_lb = P_TN // LANE

    def stage_a(z, half):
        zf = _dot(u_ref[half * P_HALF:(half + 1) * P_HALF, :], ht_ref[...])
        for lb in range(n_lb):
            z[lb] = zf[:, lb * LANE:(lb + 1) * LANE]

    def stage_b(w, half, y_old):
        wf = jnp.concatenate([w[lb] for lb in range(n_lb)], axis=1)
        y_ref[...] = y_old + _dot(vt_ref[:, half * P_HALF:(half + 1) * P_HALF], wf)

    @pl.when(j == 0)
    def _():
        stage_a(z0, 0)
        stage_a(z1, 1)
        _expert_gates(z0, w0, s1c_ref, e1c_ref, 0, s2_ref, e2_ref, thr_ref)

    @pl.when((j > 0) & (j < last))
    def _():
        stage_a(z0, 0)
        _expert_gates(z1, w1, s1p_ref, e1p_ref, half_rows, s2_ref, e2_ref, thr_ref)
        stage_b(w0, 0, jnp.where(j == 1, 0.0, y_ref[...]))
        stage_a(z1, 1)
        _expert_gates(z0, w0, s1c_ref, e1c_ref, 0, s2_ref, e2_ref, thr_ref)
        stage_b(w1, 1, y_ref[...])

    @pl.when(j == last)
    def _():
        _expert_gates(z1, w1, s1p_ref, e1p_ref, half_rows, s2_ref, e2_ref, thr_ref)
        stage_b(w0, 0, y_ref[...])
        stage_b(w1, 1, y_ref[...])


def _experts(ht, u, vt, layer, s, e, thr):
    t = ht.shape[1]
    ac = P_EC // P_NKEYS
    n_blk = P_EXPERTS // P_EC
    s4 = s.reshape(P_HEADS, 2, P_NKEYS, t)
    e4 = e.reshape(P_HEADS, 2, P_NKEYS, t)
    once = pl.Buffered(1)
    cur = lambda j: jnp.minimum(j, n_blk - 1)
    prev = lambda j: jnp.maximum(j - 1, 0)
    first_p = pl.BlockSpec((P_HEADS, 1, ac, P_TN), lambda i, j: (0, 0, prev(j), i))
    first_c = pl.BlockSpec((P_HEADS, 1, ac, P_TN), lambda i, j: (0, 0, cur(j), i))
    second = pl.BlockSpec((P_HEADS, 1, P_NKEYS, P_TN), lambda i, j: (0, 1, 0, i), pipeline_mode=once)
    return pl.pallas_call(
        _expert_kernel, grid=(t // P_TN, P_STEPS),
        in_specs=[pl.BlockSpec((D, P_TN), lambda i, j: (0, i), pipeline_mode=once),
                  pl.BlockSpec((None, P_EC, D), lambda i, j: (layer, cur(j), 0)),
                  pl.BlockSpec((None, D, P_EC), lambda i, j: (layer, 0, prev(j))),
                  first_p, first_c, second, first_p, first_c, second,
                  pl.BlockSpec((P_HEADS, P_TN), lambda i, j: (0, i), pipeline_mode=once)],
        out_specs=pl.BlockSpec((D, P_TN), lambda i, j: (0, i)),
        out_shape=jax.ShapeDtypeStruct((D, t), F32),
        scratch_shapes=[pltpu.VMEM((P_TN // LANE, P_HALF, LANE), F32), pltpu.VMEM((P_TN // LANE, P_HALF, LANE), F32),
                        pltpu.VMEM((P_TN // LANE, P_HALF, LANE), BF16), pltpu.VMEM((P_TN // LANE, P_HALF, LANE), BF16)],
        compiler_params=_cparams(("parallel", "arbitrary")), name="peer_experts",
    )(ht, u, vt, s4, s4, s4, e4, e4, e4, thr)


def _peer(ht, wq, keys, u_all, vt_all, layer):
    s, e, thr = _router(ht, wq.T.astype(BF16), keys.reshape(2 * P_HEADS, P_NKEYS, LANE).astype(BF16))
    return _experts(ht, u_all, vt_all, layer, s, e, thr)


def _ab_weights(w_in, gate_b):
    aq, ak, av, ao, ag, bq, bk, bv = jnp.split(w_in, np.cumsum([512, 512, 1024, 1024, 32, 1024, 256])[:].tolist(), axis=1)
    qk_t = jnp.concatenate([aq, ak], axis=1).astype(BF16).T.reshape(2, A_HEADS, A_DK, D)
    qk = qk_t.transpose(1, 0, 2, 3).reshape(2 * A_HEADS * A_DK, D).T
    pad = jnp.zeros((D, AB_N - COL_G - 32), BF16)
    w = jnp.concatenate([qk] + [a.astype(BF16) for a in (av, ao, bq, bk, bv, ag)] + [pad], axis=1)
    bias = jnp.zeros((1, AB_N), F32).at[0, COL_G:COL_G + 32].set(gate_b.astype(F32))
    return w, bias


def _mixer_ab(h, w_in, gate_b, mh_g, sink, rope_tabs, win_bias):
    w, bias = _ab_weights(w_in, gate_b)
    p = _matmul(h, w, bias=bias)
    hf = _mlstm(p, 0)
    hb = _mlstm(p, 1)
    ya = _aout(hf, hb, p, mh_g)
    qr, kr = _rope(p, *rope_tabs)
    sink_b = jnp.repeat(sink.astype(F32), LANE).reshape(1, B_HEADS * LANE)
    yb_l = _attn_latent(qr, 0, 0, kr, 0, 0, p, COL_BV // LANE, N_CTX, p, COL_BK // LANE, p, COL_BV // LANE,
                        win_bias, sink_b, n_heads=B_HEADS, n_kv=B_KV, kvps=B_KV, mq=B_WINDOW, wk=3 * B_WINDOW,
                        back=B_WINDOW, name="window_attn")
    yb_c = _attn_context(p, COL_BQ // (4 * LANE), p, COL_BK // LANE, p, COL_BV // LANE, sink_b,
                         n_heads=B_HEADS, n_kv=B_KV, name="window_attn_ctx")
    return jnp.concatenate([ya, jnp.concatenate([yb_c, yb_l], axis=0)], axis=1)


def _mixer_c(h, w_in, rpb):
    p = _matmul(h, w_in.astype(BF16), out_dtype=BF16)
    bias = _na_table(rpb)
    mq = NA_ROWS * GRID_W
    y_l = _attn_latent(p, N_CTX // mq, 0, p, C_HEADS, N_CTX, p, 2 * C_HEADS, N_CTX, p, C_HEADS, p, 2 * C_HEADS,
                       bias, None, n_heads=C_HEADS, n_kv=C_HEADS, kvps=NA_HEADS_PER_STEP, mq=mq, wk=NA_BAND * GRID_W,
                       back=(NA_KH // 2) * GRID_W, name="na_attn")
    y_c = _attn_context(p, 0, p, C_HEADS, p, 2 * C_HEADS, None, n_heads=C_HEADS, n_kv=C_HEADS, name="na_attn_ctx")
    return jnp.concatenate([y_c, y_l], axis=0)


def _mod_rows(m6, i_shift, i_scale, i_gate):
    z = jnp.zeros((D,), F32)
    pick = lambda r, i: m6[r, i] if i is not None else z
    return jnp.stack([pick(0, i_shift), pick(0, i_scale), pick(1, i_shift), pick(1, i_scale),
                      pick(0, i_gate), pick(1, i_gate), z, z])


def kernel(x, c, ctx, c_ctx, ada_w, ada_b, norm1_g, norm2_g, ab_w_in, ab_gate_b, ab_mh_g, ab_sink, ab_w_out,
           na_w_in, na_rpb, na_w_out, peer_wq, peer_keys, peer_u, peer_v, final_g):
    xs = jnp.concatenate([ctx[0], x[0]], axis=0).astype(F32)
    cc = jnp.zeros((16, D), F32).at[0].set(c[0]).at[1].set(c_ctx)
    mods = _adaln(cc, ada_w, ada_b)[:, :2].reshape(DEPTH, 2, 6, D)
    rope_tabs = _rope_tables()
    win_bias = _window_bias()
    u_all = peer_u.astype(BF16)
    vt_all = peer_v.transpose(0, 2, 1).astype(BF16)

    h = _norm(xs, norm1_g[0], _mod_rows(mods[0], 0, 1, None))
    for l in range(DEPTH):
        m6 = mods[l]
        if l % 2 == 0:
            e = l // 2
            ymix = _mixer_ab(h, ab_w_in[e], ab_gate_b[e], ab_mh_g[e], ab_sink[e], rope_tabs, win_bias)
            w_out = ab_w_out[e]
        else:
            o = l // 2
            ymix = _mixer_c(h, na_w_in[o], na_rpb[o])
            w_out = na_w_out[o]
        gv = jnp.concatenate([m6[:, 2], jnp.zeros((6, D), F32)], axis=0)
        xs = _matmul(ymix, w_out.astype(BF16), resid=xs, gates=gv)
        h2t = _norm(xs, norm2_g[l], _mod_rows(m6, 3, 4, None), h_t=True)
        yt = _peer(h2t, peer_wq[l], peer_keys[l], u_all, vt_all, l)
        if l + 1 < DEPTH:
            mv = _mod_rows(mods[l + 1], 0, 1, None).at[4].set(m6[0, 5]).at[5].set(m6[1, 5])
            xs, h = _norm(xs, norm1_g[l + 1], mv, y=yt, y_t=True)
        else:
            mv = _mod_rows(m6, None, None, 5)
            _, out = _norm(xs, final_g, mv, y=yt, y_t=True, out_dtype=F32, row_off=N_CTX // 256)
    return out[None]
```

```python
import functools

import numpy as np
import jax
import jax.numpy as jnp
from jax import lax
from jax.experimental import pallas as pl
from jax.experimental.pallas import tpu as pltpu

F32 = jnp.float32
BF16 = jnp.bfloat16

D = 2048
N_LAT = 8192
N_CTX = 256
N_TOK = N_CTX + N_LAT
DEPTH = 4
GRID_W = 64
GRID_H = N_LAT // GRID_W
EPS = 1e-6
LANE = 128
NEG = -1e30

A_HEADS = 8
A_DK = 64
A_DV = 128
A_CHUNK = 64
A_SPAN = 256
B_HEADS = 8
B_KV = 2
B_WINDOW = 128
ROPE_THETA = 10000.0
C_HEADS = 16
NA_KH = 8
NA_KW = 16
NA_ROWS = 4
NA_BAND = NA_ROWS + NA_KH - 1
NA_HEADS_PER_STEP = 4
P_HEADS = 8
P_NKEYS = 128
P_TOPK = 16
P_EXPERTS = P_NKEYS * P_NKEYS

AB_N = 5120
COL_QK, COL_AV, COL_AO, COL_BQ, COL_BK, COL_BV, COL_G = 0, 1024, 2048, 3072, 4096, 4352, 4608

VMEM_LIMIT = 52 * 1024 * 1024


def _cparams(sem):
    return pltpu.CompilerParams(dimension_semantics=sem, vmem_limit_bytes=VMEM_LIMIT)


def _dot(a, b):
    return jnp.dot(a, b, preferred_element_type=F32)


def _dot_nt(a, b):
    return lax.dot_general(a, b, (((1,), (1,)), ((), ())), preferred_element_type=F32)


def _ada_kernel(c_ref, w_ref, b_ref, o_ref):
    c = c_ref[...]
    s = c / (1.0 + jnp.exp(-c))
    w = w_ref[0]
    s_hi = s.astype(BF16)
    s_lo = (s - s_hi.astype(F32)).astype(BF16)
    w_hi = w.astype(BF16)
    w_lo = (w - w_hi.astype(F32)).astype(BF16)
    o_ref[0] = _dot(s_hi, w_hi) + _dot(s_lo, w_hi) + _dot(s_hi, w_lo) + b_ref[0]


def _adaln(cc, ada_w, ada_b):
    tn = 1024
    n = ada_w.shape[-1]
    return pl.pallas_call(
        _ada_kernel,
        grid=(DEPTH, n // tn),
        in_specs=[pl.BlockSpec((16, D), lambda l, j: (0, 0)),
                  pl.BlockSpec((1, D, tn), lambda l, j: (l, 0, j)),
                  pl.BlockSpec((1, 1, tn), lambda l, j: (l, 0, j))],
        out_specs=pl.BlockSpec((1, 16, tn), lambda l, j: (l, 0, j)),
        out_shape=jax.ShapeDtypeStruct((DEPTH, 16, n), F32),
        compiler_params=_cparams(("arbitrary", "arbitrary")),
        name="adaln",
    )(cc, ada_w, ada_b.reshape(DEPTH, 1, n))


def _norm_kernel(*refs, has_resid, y_t, h_t, n_ctx, tm, row_off):
    if has_resid:
        x_ref, y_ref, g_ref, mv_ref, xo_ref, h_ref = refs
    else:
        x_ref, g_ref, mv_ref, h_ref = refs
    row = (pl.program_id(0) + row_off) * tm + lax.broadcasted_iota(jnp.int32, (tm, 1), 0)
    is_ctx = row < n_ctx
    x = x_ref[...]
    if has_resid:
        gate = jnp.where(is_ctx, mv_ref[5:6, :], mv_ref[4:5, :])
        y = y_ref[...].T if y_t else y_ref[...]
        x = x + gate * y
        xo_ref[...] = x
    ms = jnp.mean(x * x, axis=-1, keepdims=True)
    yn = (x * lax.rsqrt(ms + EPS)) * g_ref[...]
    shift = jnp.where(is_ctx, mv_ref[2:3, :], mv_ref[0:1, :])
    scale = jnp.where(is_ctx, mv_ref[3:4, :], mv_ref[1:2, :])
    h = yn * (1.0 + scale) + shift
    h_ref[...] = (h.T if h_t else h).astype(h_ref.dtype)


def _norm(x, g, mv, *, y=None, y_t=False, h_t=False, out_dtype=BF16, row_off=0):
    tm = 256
    n_rows = x.shape[0] - row_off * tm
    blk = pl.BlockSpec((tm, D), lambda i: (i + row_off, 0))
    oblk = pl.BlockSpec((tm, D), lambda i: (i, 0))
    yblk = pl.BlockSpec((D, tm), lambda i: (0, i + row_off)) if y_t else blk
    hblk = pl.BlockSpec((D, tm), lambda i: (0, i)) if h_t else oblk
    vec = pl.BlockSpec((1, D), lambda i: (0, 0))
    mvs = pl.BlockSpec((8, D), lambda i: (0, 0))
    has_resid = y is not None
    kern = functools.partial(_norm_kernel, has_resid=has_resid, y_t=y_t, h_t=h_t, n_ctx=N_CTX, tm=tm, row_off=row_off)
    h_shape = jax.ShapeDtypeStruct((D, n_rows) if h_t else (n_rows, D), out_dtype)
    if has_resid:
        return pl.pallas_call(
            kern, grid=(n_rows // tm,),
            in_specs=[blk, yblk, vec, mvs], out_specs=[oblk, hblk],
            out_shape=[jax.ShapeDtypeStruct((n_rows, D), F32), h_shape],
            compiler_params=_cparams(("parallel",)), name="resid_norm",
        )(x, y, g.reshape(1, D), mv)
    return pl.pallas_call(
        kern, grid=(n_rows // tm,),
        in_specs=[blk, vec, mvs], out_specs=hblk, out_shape=h_shape,
        compiler_params=_cparams(("parallel",)), name="norm",
    )(x, g.reshape(1, D), mv)


MM_TM, MM_TN = 768, 1024


def _mm_kernel(*refs, has_bias, has_resid, n_ctx, tm):
    a_ref, w_ref = refs[0], refs[1]
    o_ref = refs[-1]
    acc = _dot(a_ref[...], w_ref[...])
    k = 2
    if has_bias:
        acc = acc + refs[k][...]
        k += 1
    if has_resid:
        x_ref, gv_ref = refs[k], refs[k + 1]
        row = pl.program_id(0) * tm + lax.broadcasted_iota(jnp.int32, (tm, 1), 0)
        gate = jnp.where(row < n_ctx, gv_ref[1:2, :], gv_ref[0:1, :])
        acc = x_ref[...] + gate * acc
    o_ref[...] = acc.astype(o_ref.dtype)


def _matmul(a, w, *, bias=None, resid=None, gates=None, out_dtype=F32, tm=MM_TM, tn=MM_TN):
    m, k = a.shape
    n = w.shape[1]
    in_specs = [pl.BlockSpec((tm, k), lambda i, j: (i, 0)), pl.BlockSpec((k, tn), lambda i, j: (0, j))]
    args = [a, w]
    if bias is not None:
        in_specs.append(pl.BlockSpec((1, tn), lambda i, j: (0, j)))
        args.append(bias)
    if resid is not None:
        in_specs += [pl.BlockSpec((tm, tn), lambda i, j: (i, j)), pl.BlockSpec((8, tn), lambda i, j: (0, j))]
        args += [resid, gates]
    kern = functools.partial(_mm_kernel, has_bias=bias is not None, has_resid=resid is not None, n_ctx=N_CTX, tm=tm)
    return pl.pallas_call(
        kern, grid=(m // tm, n // tn), in_specs=in_specs,
        out_specs=pl.BlockSpec((tm, tn), lambda i, j: (i, j)),
        out_shape=jax.ShapeDtypeStruct((m, n), out_dtype),
        compiler_params=_cparams(("parallel", "arbitrary")), name="matmul",
    )(*args)


def _rope_tile(x, cos, sin):
    lane = lax.broadcasted_iota(jnp.int32, x.shape, 1)
    partner = jnp.where((lane % 64) < 32, pltpu.roll(x, 96, axis=1), pltpu.roll(x, 32, axis=1))
    return x * cos + partner * sin


def _rope_kernel(q_ref, k_ref, cos_ref, sin_ref, qo_ref, ko_ref):
    cos, sin = cos_ref[...], sin_ref[...]
    for h in range(B_HEADS):
        sl = slice(h * LANE, (h + 1) * LANE)
        qo_ref[:, sl] = _rope_tile(q_ref[:, sl], cos, sin).astype(qo_ref.dtype)
    for h in range(B_KV):
        sl = slice(h * LANE, (h + 1) * LANE)
        ko_ref[:, sl] = _rope_tile(k_ref[:, sl], cos, sin).astype(ko_ref.dtype)


def _rope_tables():
    t = jnp.arange(N_LAT)
    freqs = ROPE_THETA ** (-jnp.arange(32, dtype=F32) / 32)
    ar = (t // GRID_W).astype(F32)[:, None] * freqs[None, :]
    ac = (t % GRID_W).astype(F32)[:, None] * freqs[None, :]
    cos = jnp.concatenate([jnp.cos(ar), jnp.cos(ar), jnp.cos(ac), jnp.cos(ac)], axis=1)
    sin = jnp.concatenate([-jnp.sin(ar), jnp.sin(ar), -jnp.sin(ac), jnp.sin(ac)], axis=1)
    return cos, sin


def _rope(p, cos, sin):
    tr = 256
    off = N_CTX // tr
    return pl.pallas_call(
        _rope_kernel, grid=(N_LAT // tr,),
        in_specs=[pl.BlockSpec((tr, 1024), lambda i: (i + off, COL_BQ // 1024)),
                  pl.BlockSpec((tr, 256), lambda i: (i + off, COL_BK // 256)),
                  pl.BlockSpec((tr, LANE), lambda i: (i, 0)),
                  pl.BlockSpec((tr, LANE), lambda i: (i, 0))],
        out_specs=[pl.BlockSpec((tr, 1024), lambda i: (i, 0)), pl.BlockSpec((tr, 256), lambda i: (i, 0))],
        out_shape=[jax.ShapeDtypeStruct((N_LAT, 1024), BF16), jax.ShapeDtypeStruct((N_LAT, 256), BF16)],
        compiler_params=_cparams(("parallel",)), name="rope",
    )(p, p, cos, sin)


def _na_bias_block(b_ref, kk, i):
    lane = lax.broadcasted_iota(jnp.int32, (GRID_W, LANE), 1)
    r0 = i * NA_ROWS
    ub = jnp.clip(r0 - NA_KH // 2, 0, GRID_H - NA_BAND)
    rows = []
    for ri in range(NA_ROWS):
        r = r0 + ri
        rs = jnp.clip(r - NA_KH // 2, 0, GRID_H - NA_KH)
        idx = []
        for a in range(NA_BAND):
            krow = ub + a
            ok = (krow >= rs) & (krow < rs + NA_KH)
            idx.append(jnp.where(ok, krow - r + (NA_KH - 1), 2 * NA_KH - 1))
        pieces = [jnp.where(lane < GRID_W, b_ref[kk, idx[a]], b_ref[kk, idx[a + 1]]) for a in range(0, NA_BAND - 1, 2)]
        pieces.append(b_ref[kk, idx[NA_BAND - 1]][:, :GRID_W])
        rows.append(jnp.concatenate(pieces, axis=1))
    return jnp.concatenate(rows, axis=0)


def _attn_kernel(*refs, g, kvps, bias_per_head, na_table, mq, wk, back, n_keys, k_off, v_off, has_band, has_sink,
                 scale):
    refs = list(refs)
    q_ref = refs.pop(0)
    if has_band:
        k_ref, v_ref = refs.pop(0), refs.pop(0)
    kc_ref, vc_ref = refs.pop(0), refs.pop(0)
    if has_band:
        b_ref = refs.pop(0)
    if has_sink:
        s_ref = refs.pop(0)
    o_ref = refs.pop(0)

    if has_band:
        i = pl.program_id(1)
        ub = pl.multiple_of(jnp.clip(i * mq - back, 0, n_keys - wk), 64)
    for hq in range(kvps * g):
        kk = hq // g
        ksl = slice(kk * LANE, (kk + 1) * LANE)
        kc = kc_ref[:, ksl].astype(BF16)
        vc = jnp.concatenate([vc_ref[:, ksl].astype(BF16), jnp.ones((kc.shape[0], LANE), BF16)], axis=1)
        if has_band:
            kb = k_ref[pl.ds(k_off + ub, wk), ksl].astype(BF16)
            vb = jnp.concatenate([v_ref[pl.ds(v_off + ub, wk), ksl].astype(BF16), jnp.ones((wk, LANE), BF16)], axis=1)
            bias = _na_bias_block(b_ref, kk, i) if na_table else b_ref[kk if bias_per_head else 0, 0]
        hh = hq
        sl = slice(hh * LANE, (hh + 1) * LANE)
        q = q_ref[:, sl].astype(BF16)
        s_ctx = _dot_nt(q, kc) * scale
        m = jnp.max(s_ctx, axis=-1, keepdims=True)
        if has_band:
            s_loc = _dot_nt(q, kb) * scale + bias
            m = jnp.maximum(m, jnp.max(s_loc, axis=-1, keepdims=True))
        if has_sink:
            snk = jnp.max(s_ref[:, sl], axis=1, keepdims=True)
            m = jnp.maximum(m, snk)
        acc = _dot(jnp.exp(s_ctx - m).astype(BF16), vc)
        if has_band:
            acc = acc + _dot(jnp.exp(s_loc - m).astype(BF16), vb)
        den = acc[:, LANE:]
        if has_sink:
            den = den + jnp.exp(snk - m)
        o_ref[:, sl] = (acc[:, :LANE] / den).astype(o_ref.dtype)


def _pattern(i, n):
    return jnp.where(i == 0, 0, jnp.where(i == n - 1, 2, 1))


def _attn_latent(q, q_blk0, q_col0, k, k_col0, k_off, v, v_col0, v_off, kc, kc_col0, vc, vc_col0,
                 bias, sink, *, n_heads, n_kv, kvps, mq, wk, back, name):
    g = n_heads // n_kv
    nq = N_LAT // mq
    per_head_bias = bias.shape[0] > 1
    na_table = bias.shape[1] == 2 * NA_KH
    kw, qw = kvps * LANE, kvps * g * LANE
    if na_table:
        bias_spec = pl.BlockSpec((kvps,) + bias.shape[1:], lambda j, i: (j, 0, 0, 0))
    else:
        bias_spec = pl.BlockSpec((kvps if per_head_bias else 1, 1, mq, wk),
                                 lambda j, i: (j if per_head_bias else 0, _pattern(i, nq), 0, 0))
    in_specs = [
        pl.BlockSpec((mq, qw), lambda j, i: (i + q_blk0, q_col0 // (kvps * g) + j)),
        pl.BlockSpec((k.shape[0], kw), lambda j, i: (0, k_col0 // kvps + j), pipeline_mode=pl.Buffered(1)),
        pl.BlockSpec((v.shape[0], kw), lambda j, i: (0, v_col0 // kvps + j), pipeline_mode=pl.Buffered(1)),
        pl.BlockSpec((N_CTX, kw), lambda j, i: (0, kc_col0 // kvps + j)),
        pl.BlockSpec((N_CTX, kw), lambda j, i: (0, vc_col0 // kvps + j)),
        bias_spec,
    ]
    args = [q, k, v, kc, vc, bias]
    if sink is not None:
        in_specs.append(pl.BlockSpec((1, qw), lambda j, i: (0, j)))
        args.append(sink)
    kern = functools.partial(_attn_kernel, g=g, kvps=kvps, bias_per_head=per_head_bias, na_table=na_table,
                             mq=mq, wk=wk, back=back,
                             n_keys=N_LAT, k_off=k_off, v_off=v_off, has_band=True, has_sink=sink is not None,
                             scale=LANE ** -0.5)
    return pl.pallas_call(
        kern, grid=(n_kv // kvps, nq), in_specs=in_specs,
        out_specs=pl.BlockSpec((mq, qw), lambda j, i: (i, j)),
        out_shape=jax.ShapeDtypeStruct((N_LAT, n_heads * LANE), BF16),
        compiler_params=_cparams(("arbitrary", "arbitrary")), name=name,
    )(*args)


def _attn_context(q, q_col0, kc, kc_col0, vc, vc_col0, sink, *, n_heads, n_kv, name):
    g = n_heads // n_kv
    in_specs = [
        pl.BlockSpec((N_CTX, g * LANE), lambda j, i: (0, q_col0 + j)),
        pl.BlockSpec((N_CTX, LANE), lambda j, i: (0, kc_col0 + j)),
        pl.BlockSpec((N_CTX, LANE), lambda j, i: (0, vc_col0 + j)),
    ]
    args = [q, kc, vc]
    if sink is not None:
        in_specs.append(pl.BlockSpec((1, g * LANE), lambda j, i: (0, j)))
        args.append(sink)
    kern = functools.partial(_attn_kernel, g=g, kvps=1, bias_per_head=False, na_table=False, mq=N_CTX, wk=0, back=0, n_keys=0,
                             k_off=0, v_off=0, has_band=False, has_sink=sink is not None, scale=LANE ** -0.5)
    return pl.pallas_call(
        kern, grid=(n_kv, 1), in_specs=in_specs,
        out_specs=pl.BlockSpec((N_CTX, g * LANE), lambda j, i: (0, j)),
        out_shape=jax.ShapeDtypeStruct((N_CTX, n_heads * LANE), BF16),
        compiler_params=_cparams(("arbitrary", "arbitrary")), name=name,
    )(*args)


def _window_bias():
    t, w = N_LAT, B_WINDOW
    nb = t // w
    out = []
    for bi in (0, 1, nb - 1):
        ub = min(max(bi * w - w, 0), t - 3 * w)
        qpos = bi * w + np.arange(w)[:, None]
        kpos = ub + np.arange(3 * w)[None, :]
        out.append(np.where(np.abs(kpos - qpos) <= w, 0.0, NEG))
    return jnp.asarray(np.stack(out)[None], F32)


def _na_table(rpb):
    col = np.arange(GRID_W)
    c0 = np.clip(col - NA_KW // 2, 0, GRID_W - NA_KW)
    col_ok = (col[None, :] >= c0[:, None]) & (col[None, :] < c0[:, None] + NA_KW)
    dc = np.clip(col[None, :] - col[:, None] + (NA_KW - 1), 0, 2 * NA_KW - 2)
    onehot = (dc[None] == np.arange(2 * NA_KW - 1)[:, None, None]).astype(np.float32)
    t = jnp.einsum("hrd,dck->hrck", rpb.astype(F32), jnp.asarray(onehot), precision=lax.Precision.HIGHEST)
    t = jnp.where(col_ok[None, None], t, NEG)
    t = jnp.concatenate([t, jnp.full((C_HEADS, 1, GRID_W, GRID_W), NEG, F32)], axis=1)
    return jnp.concatenate([t, t], axis=-1)


def _log_sigmoid(x):
    return jnp.minimum(x, 0.0) - jnp.log1p(jnp.exp(-jnp.abs(x)))


def _mlstm_kernel(qk_ref, v_ref, g_ref, h_ref, c_ref, m_ref, *, d):
    L = A_CHUNK
    P2 = 2 * L

    @pl.when(pl.program_id(0) == 0)
    def _():
        c_ref[...] = jnp.zeros_like(c_ref)
        m_ref[...] = jnp.zeros_like(m_ref)

    ri = lax.broadcasted_iota(jnp.int32, (L, P2), 0)
    ci = lax.broadcasted_iota(jnp.int32, (L, P2), 1)
    lane = lax.broadcasted_iota(jnp.int32, (1, P2), 1)
    ones_blk = jnp.ones((P2, LANE), F32)
    i_lane, f_lane = 2 * d * A_HEADS, (2 * d + 1) * A_HEADS
    order = (0, 1) if d == 0 else (1, 0)
    rp = lax.broadcasted_iota(jnp.int32, (P2, P2), 0)
    cp = lax.broadcasted_iota(jnp.int32, (P2, P2), 1)
    before = (rp <= cp) if d == 0 else (rp >= cp)
    cum_mat = (before & ((rp >= L) == (cp >= L))).astype(BF16)

    def pair(pidx, carry):
        pp = pidx if d == 0 else A_SPAN // P2 - 1 - pidx
        r0 = pl.multiple_of(pp * P2, P2)
        gt = g_ref[pl.ds(r0, P2), :].T[:4 * A_HEADS, :]
        f_all = _log_sigmoid(gt)
        f_hi = f_all.astype(BF16)
        f_r1 = f_all - f_hi.astype(F32)
        f_mid = f_r1.astype(BF16)
        f_lo = (f_r1 - f_mid.astype(F32)).astype(BF16)
        cum_all = _dot(f_hi, cum_mat) + _dot(f_mid, cum_mat) + _dot(f_lo, cum_mat)
        c_state = [c_ref[h] for h in range(A_HEADS)]
        m_state = [m_ref[h, 0:1, :] for h in range(A_HEADS)]
        h_out = []
        for h in range(A_HEADS):
            sl = slice(h * LANE, (h + 1) * LANE)
            qk = qk_ref[pl.ds(r0, P2), sl]
            kt = qk.T[A_DK:, :]
            v_aug = jnp.concatenate([v_ref[pl.ds(r0, P2), sl], ones_blk], axis=1).astype(BF16)
            i_row = gt[i_lane + h:i_lane + h + 1, :]
            f_row = f_all[f_lane + h:f_lane + h + 1, :]
            cum_row = cum_all[f_lane + h:f_lane + h + 1, :]
            for sub in order:
                rows = slice(sub * L, (sub + 1) * L)
                own = (lane >= sub * L) & (lane < (sub + 1) * L)
                cs = ci - sub * L
                seen = ((cs <= ri) & (cs >= 0)) if d == 0 else ((cs >= ri) & (cs < L))
                q = (qk[rows, :A_DK] * (A_DK ** -0.5)).astype(BF16)
                cum_col = jnp.sum(jnp.where(seen, f_row, 0.0), axis=1, keepdims=True)
                total = jnp.sum(jnp.where(own, f_row, 0.0), axis=1, keepdims=True)
                m_old, c_old = m_state[h], c_state[h]
                dm = jnp.where(seen, cum_col - cum_row + i_row, NEG)
                inter = cum_col + m_old
                mt = jnp.maximum(inter, jnp.max(dm, axis=1, keepdims=True))
                sw = _dot(q, kt.astype(BF16)) * jnp.exp(dm - mt)
                a = jnp.exp(inter - mt)
                c_bf = c_old.astype(BF16)
                na = _dot(sw.astype(BF16), v_aug)
                num = na[:, :LANE] + a * _dot(q, c_bf[:, :LANE])
                den = na[:, LANE:] + a * _dot(q, c_bf[:, LANE:])
                h_out.append((sub, sl, num / jnp.maximum(jnp.abs(den), jnp.exp(-mt))))
                wend = jnp.where(own, total - cum_row + i_row, NEG)
                m_new = jnp.maximum(total + m_old, jnp.max(wend, axis=1, keepdims=True))
                decay = jnp.exp(total + m_old - m_new)
                wv = jnp.exp(wend - m_new)
                c_state[h] = jnp.concatenate([decay, decay], axis=1) * c_old + _dot((kt * wv).astype(BF16), v_aug)
                m_state[h] = m_new
        for sub, sl, val in h_out:
            h_ref[pl.ds(r0 + sub * L, L), sl] = val
        for h in range(A_HEADS):
            c_ref[h] = c_state[h]
            m_ref[h] = jnp.broadcast_to(m_state[h], (8, LANE))
        return carry

    lax.fori_loop(0, A_SPAN // P2, pair, 0)


def _mlstm(p, d):
    n_span = N_TOK // A_SPAN
    if d == 0:
        span = lambda s: s
    else:
        span = lambda s: jnp.where(s == 0, 0, n_span - s)
    return pl.pallas_call(
        functools.partial(_mlstm_kernel, d=d), grid=(n_span,),
        in_specs=[pl.BlockSpec((A_SPAN, 1024), lambda s: (span(s), COL_QK // 1024)),
                  pl.BlockSpec((A_SPAN, 1024), lambda s: (span(s), COL_AV // 1024)),
                  pl.BlockSpec((A_SPAN, LANE), lambda s: (span(s), COL_G // LANE))],
        out_specs=pl.BlockSpec((A_SPAN, 1024), lambda s: (span(s), 0)),
        out_shape=jax.ShapeDtypeStruct((N_TOK, 1024), F32),
        scratch_shapes=[pltpu.VMEM((A_HEADS, A_DK, 2 * LANE), F32), pltpu.VMEM((A_HEADS, 8, LANE), F32)],
        compiler_params=_cparams(("arbitrary",)), name="mlstm_fwd" if d == 0 else "mlstm_bwd",
    )(p, p, p)


def _aout_kernel(hf_ref, hb_ref, o_ref, g_ref, y_ref):
    for h in range(A_HEADS):
        sl = slice(h * LANE, (h + 1) * LANE)
        x = hf_ref[:, sl] + hb_ref[:, sl]
        x = x * lax.rsqrt(jnp.mean(x * x, axis=-1, keepdims=True) + EPS)
        o = o_ref[:, sl]
        y_ref[:, sl] = ((x * g_ref[:, sl]) * (1.0 / (1.0 + jnp.exp(-o)))).astype(y_ref.dtype)


def _aout(hf, hb, p, mh_g):
    tm = 256
    blk = pl.BlockSpec((tm, 1024), lambda i: (i, 0))
    return pl.pallas_call(
        _aout_kernel, grid=(N_TOK // tm,),
        in_specs=[blk, blk, pl.BlockSpec((tm, 1024), lambda i: (i, COL_AO // 1024)),
                  pl.BlockSpec((1, 1024), lambda i: (0, 0))],
        out_specs=blk, out_shape=jax.ShapeDtypeStruct((N_TOK, 1024), BF16),
        compiler_params=_cparams(("parallel",)), name="mlstm_out",
    )(hf, hb, p, mh_g.reshape(1, 1024))


def _top_values(s, k):
    vals = []
    cur = s
    for _ in range(k):
        mx = jnp.max(cur, axis=0, keepdims=True)
        vals.append(mx)
        cur = jnp.where(cur == mx, NEG, cur)
    return vals


def _sort_network(n):
    def merge(lo, hi, r):
        step = r * 2
        if step < hi - lo:
            yield from merge(lo, hi, step)
            yield from merge(lo + r, hi, step)
            yield from ((i, i + r) for i in range(lo + r, hi - r, step))
        else:
            yield (lo, lo + r)

    def sort(lo, hi):
        if hi - lo >= 1:
            mid = lo + (hi - lo) // 2
            yield from sort(lo, mid)
            yield from sort(mid + 1, hi)
            yield from merge(lo, hi, 1)

    return list(sort(0, n - 1))


def _exchange(xs, i, j):
    xs[i], xs[j] = jnp.maximum(xs[i], xs[j]), jnp.minimum(xs[i], xs[j])


def _top16_sorted(s):
    n = P_TOPK
    xs = [s[n_ * 8:(n_ + 1) * 8, :] for n_ in range(n)]
    for i, j in _sort_network(n):
        _exchange(xs, i, j)
    for shift in (4, 2, 1):
        other = [pltpu.roll(x, shift, axis=0) for x in xs]
        xs = [jnp.maximum(xs[k], other[n - 1 - k]) for k in range(n)]
        for dist in (8, 4, 2, 1):
            for i in range(n):
                if i & dist == 0:
                    _exchange(xs, i, i + dist)
    return xs


def _router_kernel(ht_ref, wqt_ref, keys_ref, s_ref, e_ref, thr_ref, q_scr, *, tn):
    q_scr[...] = _dot(wqt_ref[...], ht_ref[...]).astype(BF16)
    row8 = lax.broadcasted_iota(jnp.int32, (8, tn), 0)
    for h in range(P_HEADS):
        tops, scores = [], []
        for p in range(2):
            hp = 2 * h + p
            s = _dot(keys_ref[hp], q_scr[hp * LANE:(hp + 1) * LANE, :])
            s_ref[hp] = s
            scores.append(s)
            tops.append(_top16_sorted(s))
        ta, tb = tops
        a_lo, a_hi, b_hi = (jnp.full((8, tn), NEG, F32) for _ in range(3))
        for i in range(8):
            a_lo = jnp.where(row8 == i, ta[i], a_lo)
            a_hi = jnp.where(row8 == i, ta[i + 8], a_hi)
            b_hi = jnp.where(row8 == i, tb[i + 8], b_hi)
        parts = [a_lo + tb[0], a_hi + tb[0], a_lo + tb[1]]
        for j in range(2, 8):
            parts.append(jnp.where(row8 < P_TOPK // (j + 1), a_lo + tb[j], NEG))
        parts.append(ta[0] + b_hi)
        cand = jnp.concatenate(parts, axis=0)
        best = _top_values(cand, P_TOPK)
        z = jnp.zeros_like(best[0])
        for c in best:
            z = z + jnp.exp(c - best[0])
        thr_ref[h:h + 1, :] = best[P_TOPK - 1]
        e_ref[2 * h] = jnp.exp(scores[0] - ta[0][0:1, :]) / z
        e_ref[2 * h + 1] = jnp.exp(scores[1] - tb[0][0:1, :])


def _router(ht, wqt, keys):
    tn = 256
    t = ht.shape[1]
    big = pl.BlockSpec((2 * P_HEADS, P_NKEYS, tn), lambda i: (0, 0, i))
    shp = jax.ShapeDtypeStruct((2 * P_HEADS, P_NKEYS, t), F32)
    return pl.pallas_call(
        functools.partial(_router_kernel, tn=tn), grid=(t // tn,),
        in_specs=[pl.BlockSpec((D, tn), lambda i: (0, i)),
                  pl.BlockSpec((D, D), lambda i: (0, 0)),
                  pl.BlockSpec((2 * P_HEADS, P_NKEYS, LANE), lambda i: (0, 0, 0))],
        out_specs=[big, big, pl.BlockSpec((P_HEADS, tn), lambda i: (0, i))],
        out_shape=[shp, shp, jax.ShapeDtypeStruct((P_HEADS, t), F32)],
        scratch_shapes=[pltpu.VMEM((D, tn), BF16)],
        compiler_params=_cparams(("parallel",)), name="peer_router",
    )(ht, wqt, keys)


P_EC = 1024
P_TN = 768


def _gelu(x):
    return 0.5 * x * (1.0 + lax.erf(x * (2.0 ** -0.5)))


P_HALF = P_EC // 2
P_STEPS = P_EXPERTS // P_EC + 1
P_GATE_ROWS = 32


def _expert_gates(z_ref, w_ref, s1_ref, e1_ref, row0, s2_ref, e2_ref, thr_ref):
    n_r = P_HALF // P_NKEYS
    for lb in range(P_TN // LANE):
        sl = slice(lb * LANE, (lb + 1) * LANE)
        for b0 in range(0, P_NKEYS, P_GATE_ROWS):
            gates = [jnp.zeros((P_GATE_ROWS, LANE), F32) for _ in range(n_r)]
            for h in range(P_HEADS):
                s2 = s2_ref[h, 0, b0:b0 + P_GATE_ROWS, sl]
                e2 = e2_ref[h, 0, b0:b0 + P_GATE_ROWS, sl]
                thr = thr_ref[h:h + 1, sl]
                for r in range(n_r):
                    pair = s1_ref[h, 0, row0 + r:row0 + r + 1, sl] + s2
                    w = e1_ref[h, 0, row0 + r:row0 + r + 1, sl] * e2
                    gates[r] = gates[r] + jnp.where(pair >= thr, w, 0.0)
            for r in range(n_r):
                rows = slice(r * P_NKEYS + b0, r * P_NKEYS + b0 + P_GATE_ROWS)
                w_ref[lb, rows, :] = (gates[r] * _gelu(z_ref[lb, rows, :])).astype(BF16)


def _expert_kernel(ht_ref, u_ref, vt_ref, s1p_ref, s1c_ref, s2_ref, e1p_ref, e1c_ref, e2_ref, thr_ref, y_ref,
                   z0, z1, w0, w1):
    j = pl.program_id(1)
    last = P_STEPS - 1
    half_rows = P_HALF // P_NKEYS

    n_lb = P_TN // LANE

    def stage_a(z, half):
        zf = _dot(u_ref[half * P_HALF:(half + 1) * P_HALF, :], ht_ref[...])
        for lb in range(n_lb):
            z[lb] = zf[:, lb * LANE:(lb + 1) * LANE]

    def stage_b(w, half, y_old):
        wf = jnp.concatenate([w[lb] for lb in range(n_lb)], axis=1)
        y_ref[...] = y_old + _dot(vt_ref[:, half * P_HALF:(half + 1) * P_HALF], wf)

    @pl.when(j == 0)
    def _():
        stage_a(z0, 0)
        stage_a(z1, 1)
        _expert_gates(z0, w0, s1c_ref, e1c_ref, 0, s2_ref, e2_ref, thr_ref)

    @pl.when((j > 0) & (j < last))
    def _():
        stage_a(z0, 0)
        _expert_gates(z1, w1, s1p_ref, e1p_ref, half_rows, s2_ref, e2_ref, thr_ref)
        stage_b(w0, 0, jnp.where(j == 1, 0.0, y_ref[...]))
        stage_a(z1, 1)
        _expert_gates(z0, w0, s1c_ref, e1c_ref, 0, s2_ref, e2_ref, thr_ref)
        stage_b(w1, 1, y_ref[...])

    @pl.when(j == last)
    def _():
        _expert_gates(z1, w1, s1p_ref, e1p_ref, half_rows, s2_ref, e2_ref, thr_ref)
        stage_b(w0, 0, y_ref[...])
        stage_b(w1, 1, y_ref[...])


def _experts(ht, u, vt, layer, s, e, thr):
    t = ht.shape[1]
    ac = P_EC // P_NKEYS
    n_blk = P_EXPERTS // P_EC
    s4 = s.reshape(P_HEADS, 2, P_NKEYS, t)
    e4 = e.reshape(P_HEADS, 2, P_NKEYS, t)
    once = pl.Buffered(1)
    cur = lambda j: jnp.minimum(j, n_blk - 1)
    prev = lambda j: jnp.maximum(j - 1, 0)
    first_p = pl.BlockSpec((P_HEADS, 1, ac, P_TN), lambda i, j: (0, 0, prev(j), i))
    first_c = pl.BlockSpec((P_HEADS, 1, ac, P_TN), lambda i, j: (0, 0, cur(j), i))
    second = pl.BlockSpec((P_HEADS, 1, P_NKEYS, P_TN), lambda i, j: (0, 1, 0, i), pipeline_mode=once)
    return pl.pallas_call(
        _expert_kernel, grid=(t // P_TN, P_STEPS),
        in_specs=[pl.BlockSpec((D, P_TN), lambda i, j: (0, i), pipeline_mode=once),
                  pl.BlockSpec((None, P_EC, D), lambda i, j: (layer, cur(j), 0)),
                  pl.BlockSpec((None, D, P_EC), lambda i, j: (layer, 0, prev(j))),
                  first_p, first_c, second, first_p, first_c, second,
                  pl.BlockSpec((P_HEADS, P_TN), lambda i, j: (0, i), pipeline_mode=once)],
        out_specs=pl.BlockSpec((D, P_TN), lambda i, j: (0, i)),
        out_shape=jax.ShapeDtypeStruct((D, t), F32),
        scratch_shapes=[pltpu.VMEM((P_TN // LANE, P_HALF, LANE), F32), pltpu.VMEM((P_TN // LANE, P_HALF, LANE), F32),
                        pltpu.VMEM((P_TN // LANE, P_HALF, LANE), BF16), pltpu.VMEM((P_TN // LANE, P_HALF, LANE), BF16)],
        compiler_params=_cparams(("parallel", "arbitrary")), name="peer_experts",
    )(ht, u, vt, s4, s4, s4, e4, e4, e4, thr)


def _peer(ht, wq, keys, u_all, vt_all, layer):
    s, e, thr = _router(ht, wq.T.astype(BF16), keys.reshape(2 * P_HEADS, P_NKEYS, LANE).astype(BF16))
    return _experts(ht, u_all, vt_all, layer, s, e, thr)


def _ab_weights(w_in, gate_b):
    aq, ak, av, ao, ag, bq, bk, bv = jnp.split(w_in, np.cumsum([512, 512, 1024, 1024, 32, 1024, 256])[:].tolist(), axis=1)
    qk_t = jnp.concatenate([aq, ak], axis=1).astype(BF16).T.reshape(2, A_HEADS, A_DK, D)
    qk = qk_t.transpose(1, 0, 2, 3).reshape(2 * A_HEADS * A_DK, D).T
    pad = jnp.zeros((D, AB_N - COL_G - 32), BF16)
    w = jnp.concatenate([qk] + [a.astype(BF16) for a in (av, ao, bq, bk, bv, ag)] + [pad], axis=1)
    bias = jnp.zeros((1, AB_N), F32).at[0, COL_G:COL_G + 32].set(gate_b.astype(F32))
    return w, bias


def _mixer_ab(h, w_in, gate_b, mh_g, sink, rope_tabs, win_bias):
    w, bias = _ab_weights(w_in, gate_b)
    p = _matmul(h, w, bias=bias)
    hf = _mlstm(p, 0)
    hb = _mlstm(p, 1)
    ya = _aout(hf, hb, p, mh_g)
    qr, kr = _rope(p, *rope_tabs)
    sink_b = jnp.repeat(sink.astype(F32), LANE).reshape(1, B_HEADS * LANE)
    yb_l = _attn_latent(qr, 0, 0, kr, 0, 0, p, COL_BV // LANE, N_CTX, p, COL_BK // LANE, p, COL_BV // LANE,
                        win_bias, sink_b, n_heads=B_HEADS, n_kv=B_KV, kvps=B_KV, mq=B_WINDOW, wk=3 * B_WINDOW,
                        back=B_WINDOW, name="window_attn")
    yb_c = _attn_context(p, COL_BQ // (4 * LANE), p, COL_BK // LANE, p, COL_BV // LANE, sink_b,
                         n_heads=B_HEADS, n_kv=B_KV, name="window_attn_ctx")
    return jnp.concatenate([ya, jnp.concatenate([yb_c, yb_l], axis=0)], axis=1)


def _mixer_c(h, w_in, rpb):
    p = _matmul(h, w_in.astype(BF16), out_dtype=BF16)
    bias = _na_table(rpb)
    mq = NA_ROWS * GRID_W
    y_l = _attn_latent(p, N_CTX // mq, 0, p, C_HEADS, N_CTX, p, 2 * C_HEADS, N_CTX, p, C_HEADS, p, 2 * C_HEADS,
                       bias, None, n_heads=C_HEADS, n_kv=C_HEADS, kvps=NA_HEADS_PER_STEP, mq=mq, wk=NA_BAND * GRID_W,
                       back=(NA_KH // 2) * GRID_W, name="na_attn")
    y_c = _attn_context(p, 0, p, C_HEADS, p, 2 * C_HEADS, None, n_heads=C_HEADS, n_kv=C_HEADS, name="na_attn_ctx")
    return jnp.concatenate([y_c, y_l], axis=0)


def _mod_rows(m6, i_shift, i_scale, i_gate):
    z = jnp.zeros((D,), F32)
    pick = lambda r, i: m6[r, i] if i is not None else z
    return jnp.stack([pick(0, i_shift), pick(0, i_scale), pick(1, i_shift), pick(1, i_scale),
                      pick(0, i_gate), pick(1, i_gate), z, z])


def kernel(x, c, ctx, c_ctx, ada_w, ada_b, norm1_g, norm2_g, ab_w_in, ab_gate_b, ab_mh_g, ab_sink, ab_w_out,
           na_w_in, na_rpb, na_w_out, peer_wq, peer_keys, peer_u, peer_v, final_g):
    xs = jnp.concatenate([ctx[0], x[0]], axis=0).astype(F32)
    cc = jnp.zeros((16, D), F32).at[0].set(c[0]).at[1].set(c_ctx)
    mods = _adaln(cc, ada_w, ada_b)[:, :2].reshape(DEPTH, 2, 6, D)
    rope_tabs = _rope_tables()
    win_bias = _window_bias()
    u_all = peer_u.astype(BF16)
    vt_all = peer_v.transpose(0, 2, 1).astype(BF16)

    h = _norm(xs, norm1_g[0], _mod_rows(mods[0], 0, 1, None))
    for l in range(DEPTH):
        m6 = mods[l]
        if l % 2 == 0:
            e = l // 2
            ymix = _mixer_ab(h, ab_w_in[e], ab_gate_b[e], ab_mh_g[e], ab_sink[e], rope_tabs, win_bias)
            w_out = ab_w_out[e]
        else:
            o = l // 2
            ymix = _mixer_c(h, na_w_in[o], na_rpb[o])
            w_out = na_w_out[o]
        gv = jnp.concatenate([m6[:, 2], jnp.zeros((6, D), F32)], axis=0)
        xs = _matmul(ymix, w_out.astype(BF16), resid=xs, gates=gv)
        h2t = _norm(xs, norm2_g[l], _mod_rows(m6, 3, 4, None), h_t=True)
        yt = _peer(h2t, peer_wq[l], peer_keys[l], u_all, vt_all, l)
        if l + 1 < DEPTH:
            mv = _mod_rows(mods[l + 1], 0, 1, None).at[4].set(m6[0, 5]).at[5].set(m6[1, 5])
            xs, h = _norm(xs, norm1_g[l + 1], mv, y=yt, y_t=True)
        else:
            mv = _mod_rows(m6, None, None, 5)
            _, out = _norm(xs, final_g, mv, y=yt, y_t=True, out_dtype=F32, row_off=N_CTX // 256)
    return out[None]
```

```python
import functools

import numpy as np
import jax
import jax.numpy as jnp
from jax import lax
from jax.experimental import pallas as pl
from jax.experimental.pallas import tpu as pltpu

F32 = jnp.float32
BF16 = jnp.bfloat16

D = 2048
N_LAT = 8192
N_CTX = 256
N_TOK = N_CTX + N_LAT
DEPTH = 4
GRID_W = 64
GRID_H = N_LAT // GRID_W
EPS = 1e-6
LANE = 128
NEG = -1e30
LOG2E = 1.4426950408889634

A_HEADS = 8
A_DK = 64
A_DV = 128
A_CHUNK = 64
A_SPAN = 256
B_HEADS = 8
B_KV = 2
B_WINDOW = 128
ROPE_THETA = 10000.0
C_HEADS = 16
NA_KH = 8
NA_KW = 16
NA_ROWS = 4
NA_BAND = NA_ROWS + NA_KH - 1
NA_HEADS_PER_STEP = 4
P_HEADS = 8
P_NKEYS = 128
P_TOPK = 16
P_EXPERTS = P_NKEYS * P_NKEYS

AB_N = 5120
COL_QK, COL_AV, COL_AO, COL_BQ, COL_BK, COL_BV, COL_G = 0, 1024, 2048, 3072, 4096, 4352, 4608

VMEM_LIMIT = 52 * 1024 * 1024


def _cparams(sem):
    return pltpu.CompilerParams(dimension_semantics=sem, vmem_limit_bytes=VMEM_LIMIT)


def _dot(a, b):
    return jnp.dot(a, b, preferred_element_type=F32)


def _dot_nt(a, b):
    return lax.dot_general(a, b, (((1,), (1,)), ((), ())), preferred_element_type=F32)


def _ada_kernel(c_ref, w_ref, b_ref, o_ref):
    c = c_ref[...]
    s = c / (1.0 + jnp.exp(-c))
    w = w_ref[0]
    s_hi = s.astype(BF16)
    s_lo = (s - s_hi.astype(F32)).astype(BF16)
    w_hi = w.astype(BF16)
    w_lo = (w - w_hi.astype(F32)).astype(BF16)
    o_ref[0] = _dot(s_hi, w_hi) + _dot(s_lo, w_hi) + _dot(s_hi, w_lo) + b_ref[0]


def _adaln(cc, ada_w, ada_b):
    tn = 1024
    n = ada_w.shape[-1]
    return pl.pallas_call(
        _ada_kernel,
        grid=(DEPTH, n // tn),
        in_specs=[pl.BlockSpec((16, D), lambda l, j: (0, 0)),
                  pl.BlockSpec((1, D, tn), lambda l, j: (l, 0, j)),
                  pl.BlockSpec((1, 1, tn), lambda l, j: (l, 0, j))],
        out_specs=pl.BlockSpec((1, 16, tn), lambda l, j: (l, 0, j)),
        out_shape=jax.ShapeDtypeStruct((DEPTH, 16, n), F32),
        compiler_params=_cparams(("arbitrary", "arbitrary")),
        name="adaln",
    )(cc, ada_w, ada_b.reshape(DEPTH, 1, n))


def _norm_kernel(*refs, has_resid, y_t, h_t, n_ctx, tm, row_off):
    if has_resid:
        x_ref, y_ref, g_ref, mv_ref, xo_ref, h_ref = refs
    else:
        x_ref, g_ref, mv_ref, h_ref = refs
    row = (pl.program_id(0) + row_off) * tm + lax.broadcasted_iota(jnp.int32, (tm, 1), 0)
    is_ctx = row < n_ctx
    x = x_ref[...]
    if has_resid:
        gate = jnp.where(is_ctx, mv_ref[5:6, :], mv_ref[4:5, :])
        y = y_ref[...].T if y_t else y_ref[...]
        x = x + gate * y
        xo_ref[...] = x
    ms = jnp.mean(x * x, axis=-1, keepdims=True)
    yn = (x * lax.rsqrt(ms + EPS)) * g_ref[...]
    shift = jnp.where(is_ctx, mv_ref[2:3, :], mv_ref[0:1, :])
    scale = jnp.where(is_ctx, mv_ref[3:4, :], mv_ref[1:2, :])
    h = yn * (1.0 + scale) + shift
    h_ref[...] = (h.T if h_t else h).astype(h_ref.dtype)


def _norm(x, g, mv, *, y=None, y_t=False, h_t=False, out_dtype=BF16, row_off=0):
    tm = 256
    n_rows = x.shape[0] - row_off * tm
    blk = pl.BlockSpec((tm, D), lambda i: (i + row_off, 0))
    oblk = pl.BlockSpec((tm, D), lambda i: (i, 0))
    yblk = pl.BlockSpec((D, tm), lambda i: (0, i + row_off)) if y_t else blk
    hblk = pl.BlockSpec((D, tm), lambda i: (0, i)) if h_t else oblk
    vec = pl.BlockSpec((1, D), lambda i: (0, 0))
    mvs = pl.BlockSpec((8, D), lambda i: (0, 0))
    has_resid = y is not None
    kern = functools.partial(_norm_kernel, has_resid=has_resid, y_t=y_t, h_t=h_t, n_ctx=N_CTX, tm=tm, row_off=row_off)
    h_shape = jax.ShapeDtypeStruct((D, n_rows) if h_t else (n_rows, D), out_dtype)
    if has_resid:
        return pl.pallas_call(
            kern, grid=(n_rows // tm,),
            in_specs=[blk, yblk, vec, mvs], out_specs=[oblk, hblk],
            out_shape=[jax.ShapeDtypeStruct((n_rows, D), F32), h_shape],
            compiler_params=_cparams(("parallel",)), name="resid_norm",
        )(x, y, g.reshape(1, D), mv)
    return pl.pallas_call(
        kern, grid=(n_rows // tm,),
        in_specs=[blk, vec, mvs], out_specs=hblk, out_shape=h_shape,
        compiler_params=_cparams(("parallel",)), name="norm",
    )(x, g.reshape(1, D), mv)


MM_TM, MM_TN = 768, 1024


def _mm_kernel(*refs, has_bias, has_resid, n_ctx, tm):
    a_ref, w_ref = refs[0], refs[1]
    o_ref = refs[-1]
    acc = _dot(a_ref[...], w_ref[...])
    k = 2
    if has_bias:
        acc = acc + refs[k][...]
        k += 1
    if has_resid:
        x_ref, gv_ref = refs[k], refs[k + 1]
        row = pl.program_id(0) * tm + lax.broadcasted_iota(jnp.int32, (tm, 1), 0)
        gate = jnp.where(row < n_ctx, gv_ref[1:2, :], gv_ref[0:1, :])
        acc = x_ref[...] + gate * acc
    o_ref[...] = acc.astype(o_ref.dtype)


def _matmul(a, w, *, bias=None, resid=None, gates=None, out_dtype=F32, tm=MM_TM, tn=MM_TN):
    m, k = a.shape
    n = w.shape[1]
    in_specs = [pl.BlockSpec((tm, k), lambda i, j: (i, 0)), pl.BlockSpec((k, tn), lambda i, j: (0, j))]
    args = [a, w]
    if bias is not None:
        in_specs.append(pl.BlockSpec((1, tn), lambda i, j: (0, j)))
        args.append(bias)
    if resid is not None:
        in_specs += [pl.BlockSpec((tm, tn), lambda i, j: (i, j)), pl.BlockSpec((8, tn), lambda i, j: (0, j))]
        args += [resid, gates]
    kern = functools.partial(_mm_kernel, has_bias=bias is not None, has_resid=resid is not None, n_ctx=N_CTX, tm=tm)
    return pl.pallas_call(
        kern, grid=(m // tm, n // tn), in_specs=in_specs,
        out_specs=pl.BlockSpec((tm, tn), lambda i, j: (i, j)),
        out_shape=jax.ShapeDtypeStruct((m, n), out_dtype),
        compiler_params=_cparams(("parallel", "arbitrary")), name="matmul",
    )(*args)


def _rope_tile(x, cos, sin):
    lane = lax.broadcasted_iota(jnp.int32, x.shape, 1)
    partner = jnp.where((lane % 64) < 32, pltpu.roll(x, 96, axis=1), pltpu.roll(x, 32, axis=1))
    return x * cos + partner * sin


def _rope_kernel(q_ref, k_ref, cos_ref, sin_ref, qo_ref, ko_ref):
    cos, sin = cos_ref[...], sin_ref[...]
    for h in range(B_HEADS):
        sl = slice(h * LANE, (h + 1) * LANE)
        qo_ref[:, sl] = _rope_tile(q_ref[:, sl], cos, sin).astype(qo_ref.dtype)
    for h in range(B_KV):
        sl = slice(h * LANE, (h + 1) * LANE)
        ko_ref[:, sl] = _rope_tile(k_ref[:, sl], cos, sin).astype(ko_ref.dtype)


def _rope_tables():
    t = jnp.arange(N_LAT)
    freqs = ROPE_THETA ** (-jnp.arange(32, dtype=F32) / 32)
    ar = (t // GRID_W).astype(F32)[:, None] * freqs[None, :]
    ac = (t % GRID_W).astype(F32)[:, None] * freqs[None, :]
    cos = jnp.concatenate([jnp.cos(ar), jnp.cos(ar), jnp.cos(ac), jnp.cos(ac)], axis=1)
    sin = jnp.concatenate([-jnp.sin(ar), jnp.sin(ar), -jnp.sin(ac), jnp.sin(ac)], axis=1)
    return cos, sin


def _rope(p, cos, sin):
    tr = 256
    off = N_CTX // tr
    return pl.pallas_call(
        _rope_kernel, grid=(N_LAT // tr,),
        in_specs=[pl.BlockSpec((tr, 1024), lambda i: (i + off, COL_BQ // 1024)),
                  pl.BlockSpec((tr, 256), lambda i: (i + off, COL_BK // 256)),
                  pl.BlockSpec((tr, LANE), lambda i: (i, 0)),
                  pl.BlockSpec((tr, LANE), lambda i: (i, 0))],
        out_specs=[pl.BlockSpec((tr, 1024), lambda i: (i, 0)), pl.BlockSpec((tr, 256), lambda i: (i, 0))],
        out_shape=[jax.ShapeDtypeStruct((N_LAT, 1024), BF16), jax.ShapeDtypeStruct((N_LAT, 256), BF16)],
        compiler_params=_cparams(("parallel",)), name="rope",
    )(p, p, cos, sin)


def _na_bias_block(b_ref, kk, i):
    lane = lax.broadcasted_iota(jnp.int32, (GRID_W, LANE), 1)
    r0 = i * NA_ROWS
    ub = jnp.clip(r0 - NA_KH // 2, 0, GRID_H - NA_BAND)
    rows = []
    for ri in range(NA_ROWS):
        r = r0 + ri
        rs = jnp.clip(r - NA_KH // 2, 0, GRID_H - NA_KH)
        idx = []
        for a in range(NA_BAND):
            krow = ub + a
            ok = (krow >= rs) & (krow < rs + NA_KH)
            idx.append(jnp.where(ok, krow - r + (NA_KH - 1), 2 * NA_KH - 1))
        pieces = [jnp.where(lane < GRID_W, b_ref[kk, idx[a]], b_ref[kk, idx[a + 1]]) for a in range(0, NA_BAND - 1, 2)]
        pieces.append(b_ref[kk, idx[NA_BAND - 1]][:, :GRID_W])
        rows.append(jnp.concatenate(pieces, axis=1))
    return jnp.concatenate(rows, axis=0)


def _attn_kernel(*refs, g, kvps, bias_per_head, na_table, mq, wk, back, n_keys, k_off, v_off, has_band, has_sink,
                 scale):
    refs = list(refs)
    q_ref = refs.pop(0)
    if has_band:
        k_ref, v_ref = refs.pop(0), refs.pop(0)
    kc_ref, vc_ref = refs.pop(0), refs.pop(0)
    if has_band:
        b_ref = refs.pop(0)
    if has_sink:
        s_ref = refs.pop(0)
    o_ref = refs.pop(0)

    if has_band:
        i = pl.program_id(1)
        ub = pl.multiple_of(jnp.clip(i * mq - back, 0, n_keys - wk), 64)
    for hq in range(kvps * g):
        kk = hq // g
        ksl = slice(kk * LANE, (kk + 1) * LANE)
        kc = kc_ref[:, ksl].astype(BF16)
        vc = jnp.concatenate([vc_ref[:, ksl].astype(BF16), jnp.ones((kc.shape[0], LANE), BF16)], axis=1)
        if has_band:
            kb = k_ref[pl.ds(k_off + ub, wk), ksl].astype(BF16)
            vb = jnp.concatenate([v_ref[pl.ds(v_off + ub, wk), ksl].astype(BF16), jnp.ones((wk, LANE), BF16)], axis=1)
            bias = _na_bias_block(b_ref, kk, i) if na_table else b_ref[kk if bias_per_head else 0, 0]
        hh = hq
        sl = slice(hh * LANE, (hh + 1) * LANE)
        q = q_ref[:, sl].astype(BF16)
        s_ctx = _dot_nt(q, kc) * scale
        m = jnp.max(s_ctx, axis=-1, keepdims=True)
        if has_band:
            s_loc = _dot_nt(q, kb) * scale + bias
            m = jnp.maximum(m, jnp.max(s_loc, axis=-1, keepdims=True))
        if has_sink:
            snk = jnp.max(s_ref[:, sl], axis=1, keepdims=True)
            m = jnp.maximum(m, snk)
        acc = _dot(jnp.exp2(s_ctx - m).astype(BF16), vc)
        if has_band:
            acc = acc + _dot(jnp.exp2(s_loc - m).astype(BF16), vb)
        den = acc[:, LANE:]
        if has_sink:
            den = den + jnp.exp2(snk - m)
        o_ref[:, sl] = (acc[:, :LANE] / den).astype(o_ref.dtype)


def _pattern(i, n):
    return jnp.where(i == 0, 0, jnp.where(i == n - 1, 2, 1))


def _attn_latent(q, q_blk0, q_col0, k, k_col0, k_off, v, v_col0, v_off, kc, kc_col0, vc, vc_col0,
                 bias, sink, *, n_heads, n_kv, kvps, mq, wk, back, name):
    g = n_heads // n_kv
    nq = N_LAT // mq
    per_head_bias = bias.shape[0] > 1
    na_table = bias.shape[1] == 2 * NA_KH
    kw, qw = kvps * LANE, kvps * g * LANE
    if na_table:
        bias_spec = pl.BlockSpec((kvps,) + bias.shape[1:], lambda j, i: (j, 0, 0, 0))
    else:
        bias_spec = pl.BlockSpec((kvps if per_head_bias else 1, 1, mq, wk),
                                 lambda j, i: (j if per_head_bias else 0, _pattern(i, nq), 0, 0))
    in_specs = [
        pl.BlockSpec((mq, qw), lambda j, i: (i + q_blk0, q_col0 // (kvps * g) + j)),
        pl.BlockSpec((k.shape[0], kw), lambda j, i: (0, k_col0 // kvps + j), pipeline_mode=pl.Buffered(1)),
        pl.BlockSpec((v.shape[0], kw), lambda j, i: (0, v_col0 // kvps + j), pipeline_mode=pl.Buffered(1)),
        pl.BlockSpec((N_CTX, kw), lambda j, i: (0, kc_col0 // kvps + j)),
        pl.BlockSpec((N_CTX, kw), lambda j, i: (0, vc_col0 // kvps + j)),
        bias_spec,
    ]
    args = [q, k, v, kc, vc, bias]
    if sink is not None:
        in_specs.append(pl.BlockSpec((1, qw), lambda j, i: (0, j)))
        args.append(sink)
    kern = functools.partial(_attn_kernel, g=g, kvps=kvps, bias_per_head=per_head_bias, na_table=na_table,
                             mq=mq, wk=wk, back=back,
                             n_keys=N_LAT, k_off=k_off, v_off=v_off, has_band=True, has_sink=sink is not None,
                             scale=LANE ** -0.5 * LOG2E)
    return pl.pallas_call(
        kern, grid=(n_kv // kvps, nq), in_specs=in_specs,
        out_specs=pl.BlockSpec((mq, qw), lambda j, i: (i, j)),
        out_shape=jax.ShapeDtypeStruct((N_LAT, n_heads * LANE), BF16),
        compiler_params=_cparams(("arbitrary", "arbitrary")), name=name,
    )(*args)


def _attn_context(q, q_col0, kc, kc_col0, vc, vc_col0, sink, *, n_heads, n_kv, name):
    g = n_heads // n_kv
    in_specs = [
        pl.BlockSpec((N_CTX, g * LANE), lambda j, i: (0, q_col0 + j)),
        pl.BlockSpec((N_CTX, LANE), lambda j, i: (0, kc_col0 + j)),
        pl.BlockSpec((N_CTX, LANE), lambda j, i: (0, vc_col0 + j)),
    ]
    args = [q, kc, vc]
    if sink is not None:
        in_specs.append(pl.BlockSpec((1, g * LANE), lambda j, i: (0, j)))
        args.append(sink)
    kern = functools.partial(_attn_kernel, g=g, kvps=1, bias_per_head=False, na_table=False, mq=N_CTX, wk=0, back=0, n_keys=0,
                             k_off=0, v_off=0, has_band=False, has_sink=sink is not None, scale=LANE ** -0.5 * LOG2E)
    return pl.pallas_call(
        kern, grid=(n_kv, 1), in_specs=in_specs,
        out_specs=pl.BlockSpec((N_CTX, g * LANE), lambda j, i: (0, j)),
        out_shape=jax.ShapeDtypeStruct((N_CTX, n_heads * LANE), BF16),
        compiler_params=_cparams(("arbitrary", "arbitrary")), name=name,
    )(*args)


def _window_bias():
    t, w = N_LAT, B_WINDOW
    nb = t // w
    out = []
    for bi in (0, 1, nb - 1):
        ub = min(max(bi * w - w, 0), t - 3 * w)
        qpos = bi * w + np.arange(w)[:, None]
        kpos = ub + np.arange(3 * w)[None, :]
        out.append(np.where(np.abs(kpos - qpos) <= w, 0.0, NEG))
    return jnp.asarray(np.stack(out)[None], F32)


def _na_table(rpb):
    col = np.arange(GRID_W)
    c0 = np.clip(col - NA_KW // 2, 0, GRID_W - NA_KW)
    col_ok = (col[None, :] >= c0[:, None]) & (col[None, :] < c0[:, None] + NA_KW)
    dc = np.clip(col[None, :] - col[:, None] + (NA_KW - 1), 0, 2 * NA_KW - 2)
    onehot = (dc[None] == np.arange(2 * NA_KW - 1)[:, None, None]).astype(np.float32)
    t = jnp.einsum("hrd,dck->hrck", rpb.astype(F32), jnp.asarray(onehot), precision=lax.Precision.HIGHEST)
    t = jnp.where(col_ok[None, None], t * LOG2E, NEG)
    t = jnp.concatenate([t, jnp.full((C_HEADS, 1, GRID_W, GRID_W), NEG, F32)], axis=1)
    return jnp.concatenate([t, t], axis=-1)


def _log_sigmoid(x):
    return jnp.minimum(x, 0.0) - jnp.log1p(jnp.exp(-jnp.abs(x)))


def _mlstm_kernel(qk_ref, v_ref, g_ref, h_ref, c_ref, m_ref, *, d):
    L = A_CHUNK
    P2 = 2 * L

    @pl.when(pl.program_id(0) == 0)
    def _():
        c_ref[...] = jnp.zeros_like(c_ref)
        m_ref[...] = jnp.zeros_like(m_ref)

    ri = lax.broadcasted_iota(jnp.int32, (L, P2), 0)
    ci = lax.broadcasted_iota(jnp.int32, (L, P2), 1)
    lane = lax.broadcasted_iota(jnp.int32, (1, P2), 1)
    ones_blk = jnp.ones((P2, LANE), F32)
    i_lane, f_lane = 2 * d * A_HEADS, (2 * d + 1) * A_HEADS
    order = (0, 1) if d == 0 else (1, 0)
    rp = lax.broadcasted_iota(jnp.int32, (P2, P2), 0)
    cp = lax.broadcasted_iota(jnp.int32, (P2, P2), 1)
    before = (rp <= cp) if d == 0 else (rp >= cp)
    cum_mat = (before & ((rp >= L) == (cp >= L))).astype(BF16)

    def pair(pidx, carry):
        pp = pidx if d == 0 else A_SPAN // P2 - 1 - pidx
        r0 = pl.multiple_of(pp * P2, P2)
        gt = g_ref[pl.ds(r0, P2), :].T[:4 * A_HEADS, :]
        f_all = _log_sigmoid(gt)
        f_hi = f_all.astype(BF16)
        f_r1 = f_all - f_hi.astype(F32)
        f_mid = f_r1.astype(BF16)
        f_lo = (f_r1 - f_mid.astype(F32)).astype(BF16)
        cum_all = _dot(f_hi, cum_mat) + _dot(f_mid, cum_mat) + _dot(f_lo, cum_mat)
        c_state = [c_ref[h] for h in range(A_HEADS)]
        m_state = [m_ref[h, 0:1, :] for h in range(A_HEADS)]
        h_out = []
        for h in range(A_HEADS):
            sl = slice(h * LANE, (h + 1) * LANE)
            qk = qk_ref[pl.ds(r0, P2), sl]
            kt = qk.T[A_DK:, :]
            v_aug = jnp.concatenate([v_ref[pl.ds(r0, P2), sl], ones_blk], axis=1).astype(BF16)
            i_row = gt[i_lane + h:i_lane + h + 1, :]
            f_row = f_all[f_lane + h:f_lane + h + 1, :]
            cum_row = cum_all[f_lane + h:f_lane + h + 1, :]
            for sub in order:
                rows = slice(sub * L, (sub + 1) * L)
                own = (lane >= sub * L) & (lane < (sub + 1) * L)
                cs = ci - sub * L
                seen = ((cs <= ri) & (cs >= 0)) if d == 0 else ((cs >= ri) & (cs < L))
                q = (qk[rows, :A_DK] * (A_DK ** -0.5)).astype(BF16)
                cum_col = jnp.sum(jnp.where(seen, f_row, 0.0), axis=1, keepdims=True)
                total = jnp.sum(jnp.where(own, f_row, 0.0), axis=1, keepdims=True)
                m_old, c_old = m_state[h], c_state[h]
                dm = jnp.where(seen, cum_col - cum_row + i_row, NEG)
                inter = cum_col + m_old
                mt = jnp.maximum(inter, jnp.max(dm, axis=1, keepdims=True))
                sw = _dot(q, kt.astype(BF16)) * jnp.exp(dm - mt)
                a = jnp.exp(inter - mt)
                c_bf = c_old.astype(BF16)
                na = _dot(sw.astype(BF16), v_aug)
                num = na[:, :LANE] + a * _dot(q, c_bf[:, :LANE])
                den = na[:, LANE:] + a * _dot(q, c_bf[:, LANE:])
                h_out.append((sub, sl, num / jnp.maximum(jnp.abs(den), jnp.exp(-mt))))
                wend = jnp.where(own, total - cum_row + i_row, NEG)
                m_new = jnp.maximum(total + m_old, jnp.max(wend, axis=1, keepdims=True))
                decay = jnp.exp(total + m_old - m_new)
                wv = jnp.exp(wend - m_new)
                c_state[h] = jnp.concatenate([decay, decay], axis=1) * c_old + _dot((kt * wv).astype(BF16), v_aug)
                m_state[h] = m_new
        for sub, sl, val in h_out:
            h_ref[pl.ds(r0 + sub * L, L), sl] = val
        for h in range(A_HEADS):
            c_ref[h] = c_state[h]
            m_ref[h] = jnp.broadcast_to(m_state[h], (8, LANE))
        return carry

    lax.fori_loop(0, A_SPAN // P2, pair, 0)


def _mlstm(p, d):
    n_span = N_TOK // A_SPAN
    if d == 0:
        span = lambda s: s
    else:
        span = lambda s: jnp.where(s == 0, 0, n_span - s)
    return pl.pallas_call(
        functools.partial(_mlstm_kernel, d=d), grid=(n_span,),
        in_specs=[pl.BlockSpec((A_SPAN, 1024), lambda s: (span(s), COL_QK // 1024)),
                  pl.BlockSpec((A_SPAN, 1024), lambda s: (span(s), COL_AV // 1024)),
                  pl.BlockSpec((A_SPAN, LANE), lambda s: (span(s), COL_G // LANE))],
        out_specs=pl.BlockSpec((A_SPAN, 1024), lambda s: (span(s), 0)),
        out_shape=jax.ShapeDtypeStruct((N_TOK, 1024), F32),
        scratch_shapes=[pltpu.VMEM((A_HEADS, A_DK, 2 * LANE), F32), pltpu.VMEM((A_HEADS, 8, LANE), F32)],
        compiler_params=_cparams(("arbitrary",)), name="mlstm_fwd" if d == 0 else "mlstm_bwd",
    )(p, p, p)


def _aout_kernel(hf_ref, hb_ref, o_ref, g_ref, y_ref):
    for h in range(A_HEADS):
        sl = slice(h * LANE, (h + 1) * LANE)
        x = hf_ref[:, sl] + hb_ref[:, sl]
        x = x * lax.rsqrt(jnp.mean(x * x, axis=-1, keepdims=True) + EPS)
        o = o_ref[:, sl]
        y_ref[:, sl] = ((x * g_ref[:, sl]) * (1.0 / (1.0 + jnp.exp(-o)))).astype(y_ref.dtype)


def _aout(hf, hb, p, mh_g):
    tm = 256
    blk = pl.BlockSpec((tm, 1024), lambda i: (i, 0))
    return pl.pallas_call(
        _aout_kernel, grid=(N_TOK // tm,),
        in_specs=[blk, blk, pl.BlockSpec((tm, 1024), lambda i: (i, COL_AO // 1024)),
                  pl.BlockSpec((1, 1024), lambda i: (0, 0))],
        out_specs=blk, out_shape=jax.ShapeDtypeStruct((N_TOK, 1024), BF16),
        compiler_params=_cparams(("parallel",)), name="mlstm_out",
    )(hf, hb, p, mh_g.reshape(1, 1024))


def _top_values(s, k):
    vals = []
    cur = s
    for _ in range(k):
        mx = jnp.max(cur, axis=0, keepdims=True)
        vals.append(mx)
        cur = jnp.where(cur == mx, NEG, cur)
    return vals


def _sort_network(n):
    def merge(lo, hi, r):
        step = r * 2
        if step < hi - lo:
            yield from merge(lo, hi, step)
            yield from merge(lo + r, hi, step)
            yield from ((i, i + r) for i in range(lo + r, hi - r, step))
        else:
            yield (lo, lo + r)

    def sort(lo, hi):
        if hi - lo >= 1:
            mid = lo + (hi - lo) // 2
            yield from sort(lo, mid)
            yield from sort(mid + 1, hi)
            yield from merge(lo, hi, 1)

    return list(sort(0, n - 1))


def _exchange(xs, i, j):
    xs[i], xs[j] = jnp.maximum(xs[i], xs[j]), jnp.minimum(xs[i], xs[j])


def _top16_sorted(s):
    n = P_TOPK
    xs = [s[n_ * 8:(n_ + 1) * 8, :] for n_ in range(n)]
    for i, j in _sort_network(n):
        _exchange(xs, i, j)
    for shift in (4, 2, 1):
        other = [pltpu.roll(x, shift, axis=0) for x in xs]
        xs = [jnp.maximum(xs[k], other[n - 1 - k]) for k in range(n)]
        for dist in (8, 4, 2, 1):
            for i in range(n):
                if i & dist == 0:
                    _exchange(xs, i, i + dist)
    return xs


def _router_kernel(ht_ref, wqt_ref, keys_ref, s_ref, e_ref, thr_ref, q_scr, *, tn):
    q_scr[...] = _dot(wqt_ref[...], ht_ref[...]).astype(BF16)
    row8 = lax.broadcasted_iota(jnp.int32, (8, tn), 0)
    for h in range(P_HEADS):
        tops, scores = [], []
        for p in range(2):
            hp = 2 * h + p
            s = _dot(keys_ref[hp], q_scr[hp * LANE:(hp + 1) * LANE, :])
            s_ref[hp] = s
            scores.append(s)
            tops.append(_top16_sorted(s))
        ta, tb = tops
        a_lo, a_hi, b_hi = (jnp.full((8, tn), NEG, F32) for _ in range(3))
        for i in range(8):
            a_lo = jnp.where(row8 == i, ta[i], a_lo)
            a_hi = jnp.where(row8 == i, ta[i + 8], a_hi)
            b_hi = jnp.where(row8 == i, tb[i + 8], b_hi)
        parts = [a_lo + tb[0], a_hi + tb[0], a_lo + tb[1]]
        for j in range(2, 8):
            parts.append(jnp.where(row8 < P_TOPK // (j + 1), a_lo + tb[j], NEG))
        parts.append(ta[0] + b_hi)
        cand = jnp.concatenate(parts, axis=0)
        best = _top_values(cand, P_TOPK)
        z = jnp.zeros_like(best[0])
        for c in best:
            z = z + jnp.exp(c - best[0])
        thr_ref[h:h + 1, :] = best[P_TOPK - 1]
        e_ref[2 * h] = jnp.exp(scores[0] - ta[0][0:1, :]) / z
        e_ref[2 * h + 1] = jnp.exp(scores[1] - tb[0][0:1, :])


def _router(ht, wqt, keys):
    tn = 256
    t = ht.shape[1]
    big = pl.BlockSpec((2 * P_HEADS, P_NKEYS, tn), lambda i: (0, 0, i))
    shp = jax.ShapeDtypeStruct((2 * P_HEADS, P_NKEYS, t), F32)
    return pl.pallas_call(
        functools.partial(_router_kernel, tn=tn), grid=(t // tn,),
        in_specs=[pl.BlockSpec((D, tn), lambda i: (0, i)),
                  pl.BlockSpec((D, D), lambda i: (0, 0)),
                  pl.BlockSpec((2 * P_HEADS, P_NKEYS, LANE), lambda i: (0, 0, 0))],
        out_specs=[big, big, pl.BlockSpec((P_HEADS, tn), lambda i: (0, i))],
        out_shape=[shp, shp, jax.ShapeDtypeStruct((P_HEADS, t), F32)],
        scratch_shapes=[pltpu.VMEM((D, tn), BF16)],
        compiler_params=_cparams(("parallel",)), name="peer_router",
    )(ht, wqt, keys)


P_EC = 1024
P_TN = 768


def _gelu(x):
    return 0.5 * x * (1.0 + lax.erf(x * (2.0 ** -0.5)))


P_HALF = P_EC // 2
P_STEPS = P_EXPERTS // P_EC + 1
P_GATE_ROWS = 32


def _expert_gates(z_ref, w_ref, s1_ref, e1_ref, row0, s2_ref, e2_ref, thr_ref):
    n_r = P_HALF // P_NKEYS
    for lb in range(P_TN // LANE):
        sl = slice(lb * LANE, (lb + 1) * LANE)
        for b0 in range(0, P_NKEYS, P_GATE_ROWS):
            gates = [jnp.zeros((P_GATE_ROWS, LANE), F32) for _ in range(n_r)]
            for h in range(P_HEADS):
                s2 = s2_ref[h, 0, b0:b0 + P_GATE_ROWS, sl]
                e2 = e2_ref[h, 0, b0:b0 + P_GATE_ROWS, sl]
                thr = thr_ref[h:h + 1, sl]
                for r in range(n_r):
                    pair = s1_ref[h, 0, row0 + r:row0 + r + 1, sl] + s2
                    w = e1_ref[h, 0, row0 + r:row0 + r + 1, sl] * e2
                    gates[r] = gates[r] + jnp.where(pair >= thr, w, 0.0)
            for r in range(n_r):
                rows = slice(r * P_NKEYS + b0, r * P_NKEYS + b0 + P_GATE_ROWS)
                w_ref[lb, rows, :] = (gates[r] * _gelu(z_ref[lb, rows, :])).astype(BF16)


def _expert_kernel(ht_ref, u_ref, vt_ref, s1p_ref, s1c_ref, s2_ref, e1p_ref, e1c_ref, e2_ref, thr_ref, y_ref,
                   z0, z1, w0, w1):
    j = pl.program_id(1)
    last = P_STEPS - 1
    half_rows = P_HALF // P_NKEYS

    n_lb = P_TN // LANE

    def stage_a(z, half):
        zf = _dot(u_ref[half * P_HALF:(half + 1) * P_HALF, :], ht_ref[...])
        for lb in range(n_lb):
            z[lb] = zf[:, lb * LANE:(lb + 1) * LANE]

    def stage_b(w, half, y_old):
        wf = jnp.concatenate([w[lb] for lb in range(n_lb)], axis=1)
        y_ref[...] = y_old + _dot(vt_ref[:, half * P_HALF:(half + 1) * P_HALF], wf)

    @pl.when(j == 0)
    def _():
        stage_a(z0, 0)
        stage_a(z1, 1)
        _expert_gates(z0, w0, s1c_ref, e1c_ref, 0, s2_ref, e2_ref, thr_ref)

    @pl.when((j > 0) & (j < last))
    def _():
        stage_a(z0, 0)
        _expert_gates(z1, w1, s1p_ref, e1p_ref, half_rows, s2_ref, e2_ref, thr_ref)
        stage_b(w0, 0, jnp.where(j == 1, 0.0, y_ref[...]))
        stage_a(z1, 1)
        _expert_gates(z0, w0, s1c_ref, e1c_ref, 0, s2_ref, e2_ref, thr_ref)
        stage_b(w1, 1, y_ref[...])

    @pl.when(j == last)
    def _():
        _expert_gates(z1, w1, s1p_ref, e1p_ref, half_rows, s2_ref, e2_ref, thr_ref)
        stage_b(w0, 0, y_ref[...])
        stage_b(w1, 1, y_ref[...])


def _experts(ht, u, vt, layer, s, e, thr):
    t = ht.shape[1]
    ac = P_EC // P_NKEYS
    n_blk = P_EXPERTS // P_EC
    s4 = s.reshape(P_HEADS, 2, P_NKEYS, t)
    e4 = e.reshape(P_HEADS, 2, P_NKEYS, t)
    once = pl.Buffered(1)
    cur = lambda j: jnp.minimum(j, n_blk - 1)
    prev = lambda j: jnp.maximum(j - 1, 0)
    first_p = pl.BlockSpec((P_HEADS, 1, ac, P_TN), lambda i, j: (0, 0, prev(j), i))
    first_c = pl.BlockSpec((P_HEADS, 1, ac, P_TN), lambda i, j: (0, 0, cur(j), i))
    second = pl.BlockSpec((P_HEADS, 1, P_NKEYS, P_TN), lambda i, j: (0, 1, 0, i), pipeline_mode=once)
    return pl.pallas_call(
        _expert_kernel, grid=(t // P_TN, P_STEPS),
        in_specs=[pl.BlockSpec((D, P_TN), lambda i, j: (0, i), pipeline_mode=once),
                  pl.BlockSpec((None, P_EC, D), lambda i, j: (layer, cur(j), 0)),
                  pl.BlockSpec((None, D, P_EC), lambda i, j: (layer, 0, prev(j))),
                  first_p, first_c, second, first_p, first_c, second,
                  pl.BlockSpec((P_HEADS, P_TN), lambda i, j: (0, i), pipeline_mode=once)],
        out_specs=pl.BlockSpec((D, P_TN), lambda i, j: (0, i)),
        out_shape=jax.ShapeDtypeStruct((D, t), F32),
        scratch_shapes=[pltpu.VMEM((P_TN // LANE, P_HALF, LANE), F32), pltpu.VMEM((P_TN // LANE, P_HALF, LANE), F32),
                        pltpu.VMEM((P_TN // LANE, P_HALF, LANE), BF16), pltpu.VMEM((P_TN // LANE, P_HALF, LANE), BF16)],
        compiler_params=_cparams(("parallel", "arbitrary")), name="peer_experts",
    )(ht, u, vt, s4, s4, s4, e4, e4, e4, thr)


def _peer(ht, wq, keys, u_all, vt_all, layer):
    s, e, thr = _router(ht, wq.T.astype(BF16), keys.reshape(2 * P_HEADS, P_NKEYS, LANE).astype(BF16))
    return _experts(ht, u_all, vt_all, layer, s, e, thr)


def _ab_weights(w_in, gate_b):
    aq, ak, av, ao, ag, bq, bk, bv = jnp.split(w_in, np.cumsum([512, 512, 1024, 1024, 32, 1024, 256])[:].tolist(), axis=1)
    qk_t = jnp.concatenate([aq, ak], axis=1).astype(BF16).T.reshape(2, A_HEADS, A_DK, D)
    qk = qk_t.transpose(1, 0, 2, 3).reshape(2 * A_HEADS * A_DK, D).T
    pad = jnp.zeros((D, AB_N - COL_G - 32), BF16)
    w = jnp.concatenate([qk] + [a.astype(BF16) for a in (av, ao, bq, bk, bv, ag)] + [pad], axis=1)
    bias = jnp.zeros((1, AB_N), F32).at[0, COL_G:COL_G + 32].set(gate_b.astype(F32))
    return w, bias


def _mixer_ab(h, w_in, gate_b, mh_g, sink, rope_tabs, win_bias):
    w, bias = _ab_weights(w_in, gate_b)
    p = _matmul(h, w, bias=bias)
    hf = _mlstm(p, 0)
    hb = _mlstm(p, 1)
    ya = _aout(hf, hb, p, mh_g)
    qr, kr = _rope(p, *rope_tabs)
    sink_b = jnp.repeat(sink.astype(F32) * LOG2E, LANE).reshape(1, B_HEADS * LANE)
    yb_l = _attn_latent(qr, 0, 0, kr, 0, 0, p, COL_BV // LANE, N_CTX, p, COL_BK // LANE, p, COL_BV // LANE,
                        win_bias, sink_b, n_heads=B_HEADS, n_kv=B_KV, kvps=B_KV, mq=B_WINDOW, wk=3 * B_WINDOW,
                        back=B_WINDOW, name="window_attn")
    yb_c = _attn_context(p, COL_BQ // (4 * LANE), p, COL_BK // LANE, p, COL_BV // LANE, sink_b,
                         n_heads=B_HEADS, n_kv=B_KV, name="window_attn_ctx")
    return jnp.concatenate([ya, jnp.concatenate([yb_c, yb_l], axis=0)], axis=1)


def _mixer_c(h, w_in, rpb):
    p = _matmul(h, w_in.astype(BF16), out_dtype=BF16)
    bias = _na_table(rpb)
    mq = NA_ROWS * GRID_W
    y_l = _attn_latent(p, N_CTX // mq, 0, p, C_HEADS, N_CTX, p, 2 * C_HEADS, N_CTX, p, C_HEADS, p, 2 * C_HEADS,
                       bias, None, n_heads=C_HEADS, n_kv=C_HEADS, kvps=NA_HEADS_PER_STEP, mq=mq, wk=NA_BAND * GRID_W,
                       back=(NA_KH // 2) * GRID_W, name="na_attn")
    y_c = _attn_context(p, 0, p, C_HEADS, p, 2 * C_HEADS, None, n_heads=C_HEADS, n_kv=C_HEADS, name="na_attn_ctx")
    return jnp.concatenate([y_c, y_l], axis=0)


def _mod_rows(m6, i_shift, i_scale, i_gate):
    z = jnp.zeros((D,), F32)
    pick = lambda r, i: m6[r, i] if i is not None else z
    return jnp.stack([pick(0, i_shift), pick(0, i_scale), pick(1, i_shift), pick(1, i_scale),
                      pick(0, i_gate), pick(1, i_gate), z, z])


def kernel(x, c, ctx, c_ctx, ada_w, ada_b, norm1_g, norm2_g, ab_w_in, ab_gate_b, ab_mh_g, ab_sink, ab_w_out,
           na_w_in, na_rpb, na_w_out, peer_wq, peer_keys, peer_u, peer_v, final_g):
    xs = jnp.concatenate([ctx[0], x[0]], axis=0).astype(F32)
    cc = jnp.zeros((16, D), F32).at[0].set(c[0]).at[1].set(c_ctx)
    mods = _adaln(cc, ada_w, ada_b)[:, :2].reshape(DEPTH, 2, 6, D)
    rope_tabs = _rope_tables()
    win_bias = _window_bias()
    u_all = peer_u.astype(BF16)
    vt_all = peer_v.transpose(0, 2, 1).astype(BF16)

    h = _norm(xs, norm1_g[0], _mod_rows(mods[0], 0, 1, None))
    for l in range(DEPTH):
        m6 = mods[l]
        if l % 2 == 0:
            e = l // 2
            ymix = _mixer_ab(h, ab_w_in[e], ab_gate_b[e], ab_mh_g[e], ab_sink[e], rope_tabs, win_bias)
            w_out = ab_w_out[e]
        else:
            o = l // 2
            ymix = _mixer_c(h, na_w_in[o], na_rpb[o])
            w_out = na_w_out[o]
        gv = jnp.concatenate([m6[:, 2], jnp.zeros((6, D), F32)], axis=0)
        xs = _matmul(ymix, w_out.astype(BF16), resid=xs, gates=gv)
        h2t = _norm(xs, norm2_g[l], _mod_rows(m6, 3, 4, None), h_t=True)
        yt = _peer(h2t, peer_wq[l], peer_keys[l], u_all, vt_all, l)
        if l + 1 < DEPTH:
            mv = _mod_rows(mods[l + 1], 0, 1, None).at[4].set(m6[0, 5]).at[5].set(m6[1, 5])
            xs, h = _norm(xs, norm1_g[l + 1], mv, y=yt, y_t=True)
        else:
            mv = _mod_rows(m6, None, None, 5)
            _, out = _norm(xs, final_g, mv, y=yt, y_t=True, out_dtype=F32, row_off=N_CTX // 256)
    return out[None]
```

```python
import functools

import numpy as np
import jax
import jax.numpy as jnp
from jax import lax
from jax.experimental import pallas as pl
from jax.experimental.pallas import tpu as pltpu

F32 = jnp.float32
BF16 = jnp.bfloat16

D = 2048
N_LAT = 8192
N_CTX = 256
N_TOK = N_CTX + N_LAT
DEPTH = 4
GRID_W = 64
GRID_H = N_LAT // GRID_W
EPS = 1e-6
LANE = 128
NEG = -1e30

A_HEADS = 8
A_DK = 64
A_DV = 128
A_CHUNK = 64
A_SPAN = 256
B_HEADS = 8
B_KV = 2
B_WINDOW = 128
ROPE_THETA = 10000.0
C_HEADS = 16
NA_KH = 8
NA_KW = 16
NA_ROWS = 4
NA_BAND = NA_ROWS + NA_KH - 1
NA_HEADS_PER_STEP = 4
P_HEADS = 8
P_NKEYS = 128
P_TOPK = 16
P_EXPERTS = P_NKEYS * P_NKEYS

AB_N = 5120
COL_QK, COL_AV, COL_AO, COL_BQ, COL_BK, COL_BV, COL_G = 0, 1024, 2048, 3072, 4096, 4352, 4608

VMEM_LIMIT = 52 * 1024 * 1024


def _cparams(sem, limit=VMEM_LIMIT):
    return pltpu.CompilerParams(dimension_semantics=sem, vmem_limit_bytes=limit)


def _dot(a, b):
    return jnp.dot(a, b, preferred_element_type=F32)


def _dot_nt(a, b):
    return lax.dot_general(a, b, (((1,), (1,)), ((), ())), preferred_element_type=F32)


def _ada_kernel(c_ref, w_ref, b_ref, o_ref):
    c = c_ref[...]
    s = c / (1.0 + jnp.exp(-c))
    w = w_ref[0]
    s_hi = s.astype(BF16)
    s_lo = (s - s_hi.astype(F32)).astype(BF16)
    w_hi = w.astype(BF16)
    w_lo = (w - w_hi.astype(F32)).astype(BF16)
    o_ref[0] = _dot(s_hi, w_hi) + _dot(s_lo, w_hi) + _dot(s_hi, w_lo) + b_ref[0]


def _adaln(cc, ada_w, ada_b):
    tn = 1024
    n = ada_w.shape[-1]
    return pl.pallas_call(
        _ada_kernel,
        grid=(DEPTH, n // tn),
        in_specs=[pl.BlockSpec((16, D), lambda l, j: (0, 0)),
                  pl.BlockSpec((1, D, tn), lambda l, j: (l, 0, j)),
                  pl.BlockSpec((1, 1, tn), lambda l, j: (l, 0, j))],
        out_specs=pl.BlockSpec((1, 16, tn), lambda l, j: (l, 0, j)),
        out_shape=jax.ShapeDtypeStruct((DEPTH, 16, n), F32),
        compiler_params=_cparams(("arbitrary", "arbitrary")),
        name="adaln",
    )(cc, ada_w, ada_b.reshape(DEPTH, 1, n))


def _norm_kernel(*refs, has_resid, y_t, h_t, n_ctx, tm, row_off):
    if has_resid:
        x_ref, y_ref, g_ref, mv_ref, xo_ref, h_ref = refs
    else:
        x_ref, g_ref, mv_ref, h_ref = refs
    row = (pl.program_id(0) + row_off) * tm + lax.broadcasted_iota(jnp.int32, (tm, 1), 0)
    is_ctx = row < n_ctx
    x = x_ref[...]
    if has_resid:
        gate = jnp.where(is_ctx, mv_ref[5:6, :], mv_ref[4:5, :])
        y = y_ref[...].T if y_t else y_ref[...]
        x = x + gate * y
        xo_ref[...] = x
    ms = jnp.mean(x * x, axis=-1, keepdims=True)
    yn = (x * lax.rsqrt(ms + EPS)) * g_ref[...]
    shift = jnp.where(is_ctx, mv_ref[2:3, :], mv_ref[0:1, :])
    scale = jnp.where(is_ctx, mv_ref[3:4, :], mv_ref[1:2, :])
    h = yn * (1.0 + scale) + shift
    h_ref[...] = (h.T if h_t else h).astype(h_ref.dtype)


def _norm(x, g, mv, *, y=None, y_t=False, h_t=False, out_dtype=BF16, row_off=0):
    tm = 256
    n_rows = x.shape[0] - row_off * tm
    blk = pl.BlockSpec((tm, D), lambda i: (i + row_off, 0))
    oblk = pl.BlockSpec((tm, D), lambda i: (i, 0))
    yblk = pl.BlockSpec((D, tm), lambda i: (0, i + row_off)) if y_t else blk
    hblk = pl.BlockSpec((D, tm), lambda i: (0, i)) if h_t else oblk
    vec = pl.BlockSpec((1, D), lambda i: (0, 0))
    mvs = pl.BlockSpec((8, D), lambda i: (0, 0))
    has_resid = y is not None
    kern = functools.partial(_norm_kernel, has_resid=has_resid, y_t=y_t, h_t=h_t, n_ctx=N_CTX, tm=tm, row_off=row_off)
    h_shape = jax.ShapeDtypeStruct((D, n_rows) if h_t else (n_rows, D), out_dtype)
    if has_resid:
        return pl.pallas_call(
            kern, grid=(n_rows // tm,),
            in_specs=[blk, yblk, vec, mvs], out_specs=[oblk, hblk],
            out_shape=[jax.ShapeDtypeStruct((n_rows, D), F32), h_shape],
            compiler_params=_cparams(("parallel",)), name="resid_norm",
        )(x, y, g.reshape(1, D), mv)
    return pl.pallas_call(
        kern, grid=(n_rows // tm,),
        in_specs=[blk, vec, mvs], out_specs=hblk, out_shape=h_shape,
        compiler_params=_cparams(("parallel",)), name="norm",
    )(x, g.reshape(1, D), mv)


MM_TM, MM_TN = 768, 1024


def _mm_kernel(*refs, has_bias, has_resid, n_ctx, tm):
    a_ref, w_ref = refs[0], refs[1]
    o_ref = refs[-1]
    acc = _dot(a_ref[...], w_ref[...])
    k = 2
    if has_bias:
        acc = acc + refs[k][...]
        k += 1
    if has_resid:
        x_ref, gv_ref = refs[k], refs[k + 1]
        row = pl.program_id(0) * tm + lax.broadcasted_iota(jnp.int32, (tm, 1), 0)
        gate = jnp.where(row < n_ctx, gv_ref[1:2, :], gv_ref[0:1, :])
        acc = x_ref[...] + gate * acc
    o_ref[...] = acc.astype(o_ref.dtype)


def _matmul(a, w, *, bias=None, resid=None, gates=None, out_dtype=F32, tm=MM_TM, tn=MM_TN):
    m, k = a.shape
    n = w.shape[1]
    in_specs = [pl.BlockSpec((tm, k), lambda i, j: (i, 0)), pl.BlockSpec((k, tn), lambda i, j: (0, j))]
    args = [a, w]
    if bias is not None:
        in_specs.append(pl.BlockSpec((1, tn), lambda i, j: (0, j)))
        args.append(bias)
    if resid is not None:
        in_specs += [pl.BlockSpec((tm, tn), lambda i, j: (i, j)), pl.BlockSpec((8, tn), lambda i, j: (0, j))]
        args += [resid, gates]
    kern = functools.partial(_mm_kernel, has_bias=bias is not None, has_resid=resid is not None, n_ctx=N_CTX, tm=tm)
    return pl.pallas_call(
        kern, grid=(m // tm, n // tn), in_specs=in_specs,
        out_specs=pl.BlockSpec((tm, tn), lambda i, j: (i, j)),
        out_shape=jax.ShapeDtypeStruct((m, n), out_dtype),
        compiler_params=_cparams(("parallel", "arbitrary")), name="matmul",
    )(*args)


def _rope_tile(x, cos, sin):
    lane = lax.broadcasted_iota(jnp.int32, x.shape, 1)
    partner = jnp.where((lane % 64) < 32, pltpu.roll(x, 96, axis=1), pltpu.roll(x, 32, axis=1))
    return x * cos + partner * sin


def _rope_kernel(q_ref, k_ref, cos_ref, sin_ref, qo_ref, ko_ref):
    cos, sin = cos_ref[...], sin_ref[...]
    for h in range(B_HEADS):
        sl = slice(h * LANE, (h + 1) * LANE)
        qo_ref[:, sl] = _rope_tile(q_ref[:, sl], cos, sin).astype(qo_ref.dtype)
    for h in range(B_KV):
        sl = slice(h * LANE, (h + 1) * LANE)
        ko_ref[:, sl] = _rope_tile(k_ref[:, sl], cos, sin).astype(ko_ref.dtype)


def _rope_tables():
    t = jnp.arange(N_LAT)
    freqs = ROPE_THETA ** (-jnp.arange(32, dtype=F32) / 32)
    ar = (t // GRID_W).astype(F32)[:, None] * freqs[None, :]
    ac = (t % GRID_W).astype(F32)[:, None] * freqs[None, :]
    cos = jnp.concatenate([jnp.cos(ar), jnp.cos(ar), jnp.cos(ac), jnp.cos(ac)], axis=1)
    sin = jnp.concatenate([-jnp.sin(ar), jnp.sin(ar), -jnp.sin(ac), jnp.sin(ac)], axis=1)
    return cos, sin


def _rope(p, cos, sin):
    tr = 256
    off = N_CTX // tr
    return pl.pallas_call(
        _rope_kernel, grid=(N_LAT // tr,),
        in_specs=[pl.BlockSpec((tr, 1024), lambda i: (i + off, COL_BQ // 1024)),
                  pl.BlockSpec((tr, 256), lambda i: (i + off, COL_BK // 256)),
                  pl.BlockSpec((tr, LANE), lambda i: (i, 0)),
                  pl.BlockSpec((tr, LANE), lambda i: (i, 0))],
        out_specs=[pl.BlockSpec((tr, 1024), lambda i: (i, 0)), pl.BlockSpec((tr, 256), lambda i: (i, 0))],
        out_shape=[jax.ShapeDtypeStruct((N_LAT, 1024), BF16), jax.ShapeDtypeStruct((N_LAT, 256), BF16)],
        compiler_params=_cparams(("parallel",)), name="rope",
    )(p, p, cos, sin)


def _na_bias_block(b_ref, kk, i):
    lane = lax.broadcasted_iota(jnp.int32, (GRID_W, LANE), 1)
    r0 = i * NA_ROWS
    ub = jnp.clip(r0 - NA_KH // 2, 0, GRID_H - NA_BAND)
    rows = []
    for ri in range(NA_ROWS):
        r = r0 + ri
        rs = jnp.clip(r - NA_KH // 2, 0, GRID_H - NA_KH)
        idx = []
        for a in range(NA_BAND):
            krow = ub + a
            ok = (krow >= rs) & (krow < rs + NA_KH)
            idx.append(jnp.where(ok, krow - r + (NA_KH - 1), 2 * NA_KH - 1))
        pieces = [jnp.where(lane < GRID_W, b_ref[kk, idx[a]], b_ref[kk, idx[a + 1]]) for a in range(0, NA_BAND - 1, 2)]
        pieces.append(b_ref[kk, idx[NA_BAND - 1]][:, :GRID_W])
        rows.append(jnp.concatenate(pieces, axis=1))
    return jnp.concatenate(rows, axis=0)


def _attn_kernel(*refs, g, kvps, bias_per_head, na_table, mq, wk, back, n_keys, k_off, v_off, has_band, has_sink,
                 scale):
    refs = list(refs)
    q_ref = refs.pop(0)
    if has_band:
        k_ref, v_ref = refs.pop(0), refs.pop(0)
    kc_ref, vc_ref = refs.pop(0), refs.pop(0)
    if has_band:
        b_ref = refs.pop(0)
    if has_sink:
        s_ref = refs.pop(0)
    o_ref = refs.pop(0)

    if has_band:
        i = pl.program_id(1)
        ub = pl.multiple_of(jnp.clip(i * mq - back, 0, n_keys - wk), 64)
    for hq in range(kvps * g):
        kk = hq // g
        ksl = slice(kk * LANE, (kk + 1) * LANE)
        kc = kc_ref[:, ksl].astype(BF16)
        vc = jnp.concatenate([vc_ref[:, ksl].astype(BF16), jnp.ones((kc.shape[0], LANE), BF16)], axis=1)
        if has_band:
            kb = k_ref[pl.ds(k_off + ub, wk), ksl].astype(BF16)
            vb = jnp.concatenate([v_ref[pl.ds(v_off + ub, wk), ksl].astype(BF16), jnp.ones((wk, LANE), BF16)], axis=1)
            bias = _na_bias_block(b_ref, kk, i) if na_table else b_ref[kk if bias_per_head else 0, 0]
        hh = hq
        sl = slice(hh * LANE, (hh + 1) * LANE)
        q = q_ref[:, sl].astype(BF16)
        s_ctx = _dot_nt(q, kc) * scale
        m = jnp.max(s_ctx, axis=-1, keepdims=True)
        if has_band:
            s_loc = _dot_nt(q, kb) * scale + bias
            m = jnp.maximum(m, jnp.max(s_loc, axis=-1, keepdims=True))
        if has_sink:
            snk = jnp.max(s_ref[:, sl], axis=1, keepdims=True)
            m = jnp.maximum(m, snk)
        acc = _dot(jnp.exp(s_ctx - m).astype(BF16), vc)
        if has_band:
            acc = acc + _dot(jnp.exp(s_loc - m).astype(BF16), vb)
        den = acc[:, LANE:]
        if has_sink:
            den = den + jnp.exp(snk - m)
        o_ref[:, sl] = (acc[:, :LANE] / den).astype(o_ref.dtype)


def _pattern(i, n):
    return jnp.where(i == 0, 0, jnp.where(i == n - 1, 2, 1))


def _attn_latent(q, q_blk0, q_col0, k, k_col0, k_off, v, v_col0, v_off, kc, kc_col0, vc, vc_col0,
                 bias, sink, *, n_heads, n_kv, kvps, mq, wk, back, name):
    g = n_heads // n_kv
    nq = N_LAT // mq
    per_head_bias = bias.shape[0] > 1
    na_table = bias.shape[1] == 2 * NA_KH
    kw, qw = kvps * LANE, kvps * g * LANE
    if na_table:
        bias_spec = pl.BlockSpec((kvps,) + bias.shape[1:], lambda j, i: (j, 0, 0, 0))
    else:
        bias_spec = pl.BlockSpec((kvps if per_head_bias else 1, 1, mq, wk),
                                 lambda j, i: (j if per_head_bias else 0, _pattern(i, nq), 0, 0))
    in_specs = [
        pl.BlockSpec((mq, qw), lambda j, i: (i + q_blk0, q_col0 // (kvps * g) + j)),
        pl.BlockSpec((k.shape[0], kw), lambda j, i: (0, k_col0 // kvps + j), pipeline_mode=pl.Buffered(1)),
        pl.BlockSpec((v.shape[0], kw), lambda j, i: (0, v_col0 // kvps + j), pipeline_mode=pl.Buffered(1)),
        pl.BlockSpec((N_CTX, kw), lambda j, i: (0, kc_col0 // kvps + j)),
        pl.BlockSpec((N_CTX, kw), lambda j, i: (0, vc_col0 // kvps + j)),
        bias_spec,
    ]
    args = [q, k, v, kc, vc, bias]
    if sink is not None:
        in_specs.append(pl.BlockSpec((1, qw), lambda j, i: (0, j)))
        args.append(sink)
    kern = functools.partial(_attn_kernel, g=g, kvps=kvps, bias_per_head=per_head_bias, na_table=na_table,
                             mq=mq, wk=wk, back=back,
                             n_keys=N_LAT, k_off=k_off, v_off=v_off, has_band=True, has_sink=sink is not None,
                             scale=LANE ** -0.5)
    return pl.pallas_call(
        kern, grid=(n_kv // kvps, nq), in_specs=in_specs,
        out_specs=pl.BlockSpec((mq, qw), lambda j, i: (i, j)),
        out_shape=jax.ShapeDtypeStruct((N_LAT, n_heads * LANE), BF16),
        compiler_params=_cparams(("arbitrary", "arbitrary")), name=name,
    )(*args)


def _attn_context(q, q_col0, kc, kc_col0, vc, vc_col0, sink, *, n_heads, n_kv, name):
    g = n_heads // n_kv
    in_specs = [
        pl.BlockSpec((N_CTX, g * LANE), lambda j, i: (0, q_col0 + j)),
        pl.BlockSpec((N_CTX, LANE), lambda j, i: (0, kc_col0 + j)),
        pl.BlockSpec((N_CTX, LANE), lambda j, i: (0, vc_col0 + j)),
    ]
    args = [q, kc, vc]
    if sink is not None:
        in_specs.append(pl.BlockSpec((1, g * LANE), lambda j, i: (0, j)))
        args.append(sink)
    kern = functools.partial(_attn_kernel, g=g, kvps=1, bias_per_head=False, na_table=False, mq=N_CTX, wk=0, back=0, n_keys=0,
                             k_off=0, v_off=0, has_band=False, has_sink=sink is not None, scale=LANE ** -0.5)
    return pl.pallas_call(
        kern, grid=(n_kv, 1), in_specs=in_specs,
        out_specs=pl.BlockSpec((N_CTX, g * LANE), lambda j, i: (0, j)),
        out_shape=jax.ShapeDtypeStruct((N_CTX, n_heads * LANE), BF16),
        compiler_params=_cparams(("arbitrary", "arbitrary")), name=name,
    )(*args)


def _window_bias():
    t, w = N_LAT, B_WINDOW
    nb = t // w
    out = []
    for bi in (0, 1, nb - 1):
        ub = min(max(bi * w - w, 0), t - 3 * w)
        qpos = bi * w + np.arange(w)[:, None]
        kpos = ub + np.arange(3 * w)[None, :]
        out.append(np.where(np.abs(kpos - qpos) <= w, 0.0, NEG))
    return jnp.asarray(np.stack(out)[None], F32)


def _na_table(rpb):
    col = np.arange(GRID_W)
    c0 = np.clip(col - NA_KW // 2, 0, GRID_W - NA_KW)
    col_ok = (col[None, :] >= c0[:, None]) & (col[None, :] < c0[:, None] + NA_KW)
    dc = np.clip(col[None, :] - col[:, None] + (NA_KW - 1), 0, 2 * NA_KW - 2)
    onehot = (dc[None] == np.arange(2 * NA_KW - 1)[:, None, None]).astype(np.float32)
    t = jnp.einsum("hrd,dck->hrck", rpb.astype(F32), jnp.asarray(onehot), precision=lax.Precision.HIGHEST)
    t = jnp.where(col_ok[None, None], t, NEG)
    t = jnp.concatenate([t, jnp.full((C_HEADS, 1, GRID_W, GRID_W), NEG, F32)], axis=1)
    return jnp.concatenate([t, t], axis=-1)


def _log_sigmoid(x):
    return jnp.minimum(x, 0.0) - jnp.log1p(jnp.exp(-jnp.abs(x)))


def _mlstm_kernel(qk_ref, v_ref, g_ref, h_ref, c_ref, m_ref, *, d):
    L = A_CHUNK
    P2 = 2 * L

    @pl.when(pl.program_id(0) == 0)
    def _():
        c_ref[...] = jnp.zeros_like(c_ref)
        m_ref[...] = jnp.zeros_like(m_ref)

    ri = lax.broadcasted_iota(jnp.int32, (L, P2), 0)
    ci = lax.broadcasted_iota(jnp.int32, (L, P2), 1)
    lane = lax.broadcasted_iota(jnp.int32, (1, P2), 1)
    ones_blk = jnp.ones((P2, LANE), F32)
    i_lane, f_lane = 2 * d * A_HEADS, (2 * d + 1) * A_HEADS
    order = (0, 1) if d == 0 else (1, 0)
    rp = lax.broadcasted_iota(jnp.int32, (P2, P2), 0)
    cp = lax.broadcasted_iota(jnp.int32, (P2, P2), 1)
    before = (rp <= cp) if d == 0 else (rp >= cp)
    cum_mat = (before & ((rp >= L) == (cp >= L))).astype(BF16)

    def pair(pidx, carry):
        pp = pidx if d == 0 else A_SPAN // P2 - 1 - pidx
        r0 = pl.multiple_of(pp * P2, P2)
        gt = g_ref[pl.ds(r0, P2), :].T[:4 * A_HEADS, :]
        f_all = _log_sigmoid(gt)
        f_hi = f_all.astype(BF16)
        f_r1 = f_all - f_hi.astype(F32)
        f_mid = f_r1.astype(BF16)
        f_lo = (f_r1 - f_mid.astype(F32)).astype(BF16)
        cum_all = _dot(f_hi, cum_mat) + _dot(f_mid, cum_mat) + _dot(f_lo, cum_mat)
        c_state = [c_ref[h] for h in range(A_HEADS)]
        m_state = [m_ref[h, 0:1, :] for h in range(A_HEADS)]
        h_out = []
        for h in range(A_HEADS):
            sl = slice(h * LANE, (h + 1) * LANE)
            qk = qk_ref[pl.ds(r0, P2), sl]
            kt = qk.T[A_DK:, :]
            v_aug = jnp.concatenate([v_ref[pl.ds(r0, P2), sl], ones_blk], axis=1).astype(BF16)
            i_row = gt[i_lane + h:i_lane + h + 1, :]
            f_row = f_all[f_lane + h:f_lane + h + 1, :]
            cum_row = cum_all[f_lane + h:f_lane + h + 1, :]
            for sub in order:
                rows = slice(sub * L, (sub + 1) * L)
                own = (lane >= sub * L) & (lane < (sub + 1) * L)
                cs = ci - sub * L
                seen = ((cs <= ri) & (cs >= 0)) if d == 0 else ((cs >= ri) & (cs < L))
                q = (qk[rows, :A_DK] * (A_DK ** -0.5)).astype(BF16)
                cum_col = jnp.sum(jnp.where(seen, f_row, 0.0), axis=1, keepdims=True)
                total = jnp.sum(jnp.where(own, f_row, 0.0), axis=1, keepdims=True)
                m_old, c_old = m_state[h], c_state[h]
                dm = jnp.where(seen, cum_col - cum_row + i_row, NEG)
                inter = cum_col + m_old
                mt = jnp.maximum(inter, jnp.max(dm, axis=1, keepdims=True))
                sw = _dot(q, kt.astype(BF16)) * jnp.exp(dm - mt)
                a = jnp.exp(inter - mt)
                c_bf = c_old.astype(BF16)
                na = _dot(sw.astype(BF16), v_aug)
                num = na[:, :LANE] + a * _dot(q, c_bf[:, :LANE])
                den = na[:, LANE:] + a * _dot(q, c_bf[:, LANE:])
                h_out.append((sub, sl, num / jnp.maximum(jnp.abs(den), jnp.exp(-mt))))
                wend = jnp.where(own, total - cum_row + i_row, NEG)
                m_new = jnp.maximum(total + m_old, jnp.max(wend, axis=1, keepdims=True))
                decay = jnp.exp(total + m_old - m_new)
                wv = jnp.exp(wend - m_new)
                c_state[h] = jnp.concatenate([decay, decay], axis=1) * c_old + _dot((kt * wv).astype(BF16), v_aug)
                m_state[h] = m_new
        for sub, sl, val in h_out:
            h_ref[pl.ds(r0 + sub * L, L), sl] = val
        for h in range(A_HEADS):
            c_ref[h] = c_state[h]
            m_ref[h] = jnp.broadcast_to(m_state[h], (8, LANE))
        return carry

    lax.fori_loop(0, A_SPAN // P2, pair, 0)


def _mlstm(p, d):
    n_span = N_TOK // A_SPAN
    if d == 0:
        span = lambda s: s
    else:
        span = lambda s: jnp.where(s == 0, 0, n_span - s)
    return pl.pallas_call(
        functools.partial(_mlstm_kernel, d=d), grid=(n_span,),
        in_specs=[pl.BlockSpec((A_SPAN, 1024), lambda s: (span(s), COL_QK // 1024)),
                  pl.BlockSpec((A_SPAN, 1024), lambda s: (span(s), COL_AV // 1024)),
                  pl.BlockSpec((A_SPAN, LANE), lambda s: (span(s), COL_G // LANE))],
        out_specs=pl.BlockSpec((A_SPAN, 1024), lambda s: (span(s), 0)),
        out_shape=jax.ShapeDtypeStruct((N_TOK, 1024), F32),
        scratch_shapes=[pltpu.VMEM((A_HEADS, A_DK, 2 * LANE), F32), pltpu.VMEM((A_HEADS, 8, LANE), F32)],
        compiler_params=_cparams(("arbitrary",)), name="mlstm_fwd" if d == 0 else "mlstm_bwd",
    )(p, p, p)


def _aout_kernel(hf_ref, hb_ref, o_ref, g_ref, y_ref):
    for h in range(A_HEADS):
        sl = slice(h * LANE, (h + 1) * LANE)
        x = hf_ref[:, sl] + hb_ref[:, sl]
        x = x * lax.rsqrt(jnp.mean(x * x, axis=-1, keepdims=True) + EPS)
        o = o_ref[:, sl]
        y_ref[:, sl] = ((x * g_ref[:, sl]) * (1.0 / (1.0 + jnp.exp(-o)))).astype(y_ref.dtype)


def _aout(hf, hb, p, mh_g):
    tm = 256
    blk = pl.BlockSpec((tm, 1024), lambda i: (i, 0))
    return pl.pallas_call(
        _aout_kernel, grid=(N_TOK // tm,),
        in_specs=[blk, blk, pl.BlockSpec((tm, 1024), lambda i: (i, COL_AO // 1024)),
                  pl.BlockSpec((1, 1024), lambda i: (0, 0))],
        out_specs=blk, out_shape=jax.ShapeDtypeStruct((N_TOK, 1024), BF16),
        compiler_params=_cparams(("parallel",)), name="mlstm_out",
    )(hf, hb, p, mh_g.reshape(1, 1024))


def _top_values(s, k):
    vals = []
    cur = s
    for _ in range(k):
        mx = jnp.max(cur, axis=0, keepdims=True)
        vals.append(mx)
        cur = jnp.where(cur == mx, NEG, cur)
    return vals


def _sort_network(n):
    def merge(lo, hi, r):
        step = r * 2
        if step < hi - lo:
            yield from merge(lo, hi, step)
            yield from merge(lo + r, hi, step)
            yield from ((i, i + r) for i in range(lo + r, hi - r, step))
        else:
            yield (lo, lo + r)

    def sort(lo, hi):
        if hi - lo >= 1:
            mid = lo + (hi - lo) // 2
            yield from sort(lo, mid)
            yield from sort(mid + 1, hi)
            yield from merge(lo, hi, 1)

    return list(sort(0, n - 1))


def _exchange(xs, i, j):
    xs[i], xs[j] = jnp.maximum(xs[i], xs[j]), jnp.minimum(xs[i], xs[j])


def _top16_sorted(s):
    n = P_TOPK
    xs = [s[n_ * 8:(n_ + 1) * 8, :] for n_ in range(n)]
    for i, j in _sort_network(n):
        _exchange(xs, i, j)
    for shift in (4, 2, 1):
        other = [pltpu.roll(x, shift, axis=0) for x in xs]
        xs = [jnp.maximum(xs[k], other[n - 1 - k]) for k in range(n)]
        for dist in (8, 4, 2, 1):
            for i in range(n):
                if i & dist == 0:
                    _exchange(xs, i, i + dist)
    return xs


def _router_kernel(ht_ref, wqt_ref, keys_ref, s_ref, e_ref, thr_ref, q_scr, *, tn):
    q_scr[...] = _dot(wqt_ref[...], ht_ref[...]).astype(BF16)
    row8 = lax.broadcasted_iota(jnp.int32, (8, tn), 0)
    for h in range(P_HEADS):
        tops, scores = [], []
        for p in range(2):
            hp = 2 * h + p
            s = _dot(keys_ref[hp], q_scr[hp * LANE:(hp + 1) * LANE, :])
            s_ref[hp] = s
            scores.append(s)
            tops.append(_top16_sorted(s))
        ta, tb = tops
        a_lo, a_hi, b_hi = (jnp.full((8, tn), NEG, F32) for _ in range(3))
        for i in range(8):
            a_lo = jnp.where(row8 == i, ta[i], a_lo)
            a_hi = jnp.where(row8 == i, ta[i + 8], a_hi)
            b_hi = jnp.where(row8 == i, tb[i + 8], b_hi)
        parts = [a_lo + tb[0], a_hi + tb[0], a_lo + tb[1]]
        for j in range(2, 8):
            parts.append(jnp.where(row8 < P_TOPK // (j + 1), a_lo + tb[j], NEG))
        parts.append(ta[0] + b_hi)
        cand = jnp.concatenate(parts, axis=0)
        best = _top_values(cand, P_TOPK)
        z = jnp.zeros_like(best[0])
        for c in best:
            z = z + jnp.exp(c - best[0])
        thr_ref[h:h + 1, :] = best[P_TOPK - 1]
        e_ref[2 * h] = jnp.exp(scores[0] - ta[0][0:1, :]) / z
        e_ref[2 * h + 1] = jnp.exp(scores[1] - tb[0][0:1, :])


def _router(ht, wqt, keys):
    tn = 256
    t = ht.shape[1]
    big = pl.BlockSpec((2 * P_HEADS, P_NKEYS, tn), lambda i: (0, 0, i))
    shp = jax.ShapeDtypeStruct((2 * P_HEADS, P_NKEYS, t), F32)
    return pl.pallas_call(
        functools.partial(_router_kernel, tn=tn), grid=(t // tn,),
        in_specs=[pl.BlockSpec((D, tn), lambda i: (0, i)),
                  pl.BlockSpec((D, D), lambda i: (0, 0)),
                  pl.BlockSpec((2 * P_HEADS, P_NKEYS, LANE), lambda i: (0, 0, 0))],
        out_specs=[big, big, pl.BlockSpec((P_HEADS, tn), lambda i: (0, i))],
        out_shape=[shp, shp, jax.ShapeDtypeStruct((P_HEADS, t), F32)],
        scratch_shapes=[pltpu.VMEM((D, tn), BF16)],
        compiler_params=_cparams(("parallel",)), name="peer_router",
    )(ht, wqt, keys)


P_EC = 1024
P_TN = 768


def _gelu(x):
    return 0.5 * x * (1.0 + lax.erf(x * (2.0 ** -0.5)))


P_HALF = P_EC // 2
P_STEPS = P_EXPERTS // P_EC + 1
P_GATE_ROWS = 32
P_VMEM_LIMIT = 60 * 1024 * 1024


def _expert_gates(z_ref, w_ref, s1_ref, e1_ref, row0, s2_ref, e2_ref, thr_ref):
    n_r = P_HALF // P_NKEYS
    for lb in range(P_TN // LANE):
        sl = slice(lb * LANE, (lb + 1) * LANE)
        for b0 in range(0, P_NKEYS, P_GATE_ROWS):
            gates = [jnp.zeros((P_GATE_ROWS, LANE), F32) for _ in range(n_r)]
            for h in range(P_HEADS):
                s2 = s2_ref[h, 0, b0:b0 + P_GATE_ROWS, sl]
                e2 = e2_ref[h, 0, b0:b0 + P_GATE_ROWS, sl]
                thr = thr_ref[h:h + 1, sl]
                for r in range(n_r):
                    pair = s1_ref[h, 0, row0 + r:row0 + r + 1, sl] + s2
                    w = e1_ref[h, 0, row0 + r:row0 + r + 1, sl] * e2
                    gates[r] = gates[r] + jnp.where(pair >= thr, w, 0.0)
            for r in range(n_r):
                rows = slice(r * P_NKEYS + b0, r * P_NKEYS + b0 + P_GATE_ROWS)
                w_ref[lb, rows, :] = (gates[r] * _gelu(z_ref[lb, rows, :])).astype(BF16)


def _expert_kernel(ht_ref, u_ref, vt_ref, s1p_ref, s1c_ref, s2_ref, e1p_ref, e1c_ref, e2_ref, thr_ref, y_ref,
                   z0, z1, w0, w1):
    j = pl.program_id(1)
    last = P_STEPS - 1
    half_rows = P_HALF // P_NKEYS

    n_lb = P_TN // LANE

    def stage_a(z, half):
        zf = _dot(u_ref[half * P_HALF:(half + 1) * P_HALF, :], ht_ref[...])
        for lb in range(n_lb):
            z[lb] = zf[:, lb * LANE:(lb + 1) * LANE]

    def stage_b(w, half, y_old):
        wf = jnp.concatenate([w[lb] for lb in range(n_lb)], axis=1)
        y_ref[...] = y_old + _dot(vt_ref[:, half * P_HALF:(half + 1) * P_HALF], wf)

    @pl.when(j == 0)
    def _():
        stage_a(z0, 0)
        stage_a(z1, 1)
        _expert_gates(z0, w0, s1c_ref, e1c_ref, 0, s2_ref, e2_ref, thr_ref)

    @pl.when((j > 0) & (j < last))
    def _():
        stage_a(z0, 0)
        _expert_gates(z1, w1, s1p_ref, e1p_ref, half_rows, s2_ref, e2_ref, thr_ref)
        stage_b(w0, 0, jnp.where(j == 1, 0.0, y_ref[...]))
        stage_a(z1, 1)
        _expert_gates(z0, w0, s1c_ref, e1c_ref, 0, s2_ref, e2_ref, thr_ref)
        stage_b(w1, 1, y_ref[...])

    @pl.when(j == last)
    def _():
        _expert_gates(z1, w1, s1p_ref, e1p_ref, half_rows, s2_ref, e2_ref, thr_ref)
        stage_b(w0, 0, y_ref[...])
        stage_b(w1, 1, y_ref[...])


def _experts(ht, u, vt, layer, s, e, thr):
    t = ht.shape[1]
    ac = P_EC // P_NKEYS
    n_blk = P_EXPERTS // P_EC
    s4 = s.reshape(P_HEADS, 2, P_NKEYS, t)
    e4 = e.reshape(P_HEADS, 2, P_NKEYS, t)
    once = None
    cur = lambda j: jnp.minimum(j, n_blk - 1)
    prev = lambda j: jnp.maximum(j - 1, 0)
    first_p = pl.BlockSpec((P_HEADS, 1, ac, P_TN), lambda i, j: (0, 0, prev(j), i))
    first_c = pl.BlockSpec((P_HEADS, 1, ac, P_TN), lambda i, j: (0, 0, cur(j), i))
    second = pl.BlockSpec((P_HEADS, 1, P_NKEYS, P_TN), lambda i, j: (0, 1, 0, i), pipeline_mode=once)
    return pl.pallas_call(
        _expert_kernel, grid=(t // P_TN, P_STEPS),
        in_specs=[pl.BlockSpec((D, P_TN), lambda i, j: (0, i), pipeline_mode=once),
                  pl.BlockSpec((None, P_EC, D), lambda i, j: (layer, cur(j), 0)),
                  pl.BlockSpec((None, D, P_EC), lambda i, j: (layer, 0, prev(j))),
                  first_p, first_c, second, first_p, first_c, second,
                  pl.BlockSpec((P_HEADS, P_TN), lambda i, j: (0, i), pipeline_mode=once)],
        out_specs=pl.BlockSpec((D, P_TN), lambda i, j: (0, i)),
        out_shape=jax.ShapeDtypeStruct((D, t), F32),
        scratch_shapes=[pltpu.VMEM((P_TN // LANE, P_HALF, LANE), F32), pltpu.VMEM((P_TN // LANE, P_HALF, LANE), F32),
                        pltpu.VMEM((P_TN // LANE, P_HALF, LANE), BF16), pltpu.VMEM((P_TN // LANE, P_HALF, LANE), BF16)],
        compiler_params=_cparams(("parallel", "arbitrary"), limit=P_VMEM_LIMIT), name="peer_experts",
    )(ht, u, vt, s4, s4, s4, e4, e4, e4, thr)


def _peer(ht, wq, keys, u_all, vt_all, layer):
    s, e, thr = _router(ht, wq.T.astype(BF16), keys.reshape(2 * P_HEADS, P_NKEYS, LANE).astype(BF16))
    return _experts(ht, u_all, vt_all, layer, s, e, thr)


def _ab_weights(w_in, gate_b):
    aq, ak, av, ao, ag, bq, bk, bv = jnp.split(w_in, np.cumsum([512, 512, 1024, 1024, 32, 1024, 256])[:].tolist(), axis=1)
    qk_t = jnp.concatenate([aq, ak], axis=1).astype(BF16).T.reshape(2, A_HEADS, A_DK, D)
    qk = qk_t.transpose(1, 0, 2, 3).reshape(2 * A_HEADS * A_DK, D).T
    pad = jnp.zeros((D, AB_N - COL_G - 32), BF16)
    w = jnp.concatenate([qk] + [a.astype(BF16) for a in (av, ao, bq, bk, bv, ag)] + [pad], axis=1)
    bias = jnp.zeros((1, AB_N), F32).at[0, COL_G:COL_G + 32].set(gate_b.astype(F32))
    return w, bias


def _mixer_ab(h, w_in, gate_b, mh_g, sink, rope_tabs, win_bias):
    w, bias = _ab_weights(w_in, gate_b)
    p = _matmul(h, w, bias=bias)
    hf = _mlstm(p, 0)
    hb = _mlstm(p, 1)
    ya = _aout(hf, hb, p, mh_g)
    qr, kr = _rope(p, *rope_tabs)
    sink_b = jnp.repeat(sink.astype(F32), LANE).reshape(1, B_HEADS * LANE)
    yb_l = _attn_latent(qr, 0, 0, kr, 0, 0, p, COL_BV // LANE, N_CTX, p, COL_BK // LANE, p, COL_BV // LANE,
                        win_bias, sink_b, n_heads=B_HEADS, n_kv=B_KV, kvps=B_KV, mq=B_WINDOW, wk=3 * B_WINDOW,
                        back=B_WINDOW, name="window_attn")
    yb_c = _attn_context(p, COL_BQ // (4 * LANE), p, COL_BK // LANE, p, COL_BV // LANE, sink_b,
                         n_heads=B_HEADS, n_kv=B_KV, name="window_attn_ctx")
    return jnp.concatenate([ya, jnp.concatenate([yb_c, yb_l], axis=0)], axis=1)


def _mixer_c(h, w_in, rpb):
    p = _matmul(h, w_in.astype(BF16), out_dtype=BF16)
    bias = _na_table(rpb)
    mq = NA_ROWS * GRID_W
    y_l = _attn_latent(p, N_CTX // mq, 0, p, C_HEADS, N_CTX, p, 2 * C_HEADS, N_CTX, p, C_HEADS, p, 2 * C_HEADS,
                       bias, None, n_heads=C_HEADS, n_kv=C_HEADS, kvps=NA_HEADS_PER_STEP, mq=mq, wk=NA_BAND * GRID_W,
                       back=(NA_KH // 2) * GRID_W, name="na_attn")
    y_c = _attn_context(p, 0, p, C_HEADS, p, 2 * C_HEADS, None, n_heads=C_HEADS, n_kv=C_HEADS, name="na_attn_ctx")
    return jnp.concatenate([y_c, y_l], axis=0)


def _mod_rows(m6, i_shift, i_scale, i_gate):
    z = jnp.zeros((D,), F32)
    pick = lambda r, i: m6[r, i] if i is not None else z
    return jnp.stack([pick(0, i_shift), pick(0, i_scale), pick(1, i_shift), pick(1, i_scale),
                      pick(0, i_gate), pick(1, i_gate), z, z])


def kernel(x, c, ctx, c_ctx, ada_w, ada_b, norm1_g, norm2_g, ab_w_in, ab_gate_b, ab_mh_g, ab_sink, ab_w_out,
           na_w_in, na_rpb, na_w_out, peer_wq, peer_keys, peer_u, peer_v, final_g):
    xs = jnp.concatenate([ctx[0], x[0]], axis=0).astype(F32)
    cc = jnp.zeros((16, D), F32).at[0].set(c[0]).at[1].set(c_ctx)
    mods = _adaln(cc, ada_w, ada_b)[:, :2].reshape(DEPTH, 2, 6, D)
    rope_tabs = _rope_tables()
    win_bias = _window_bias()
    u_all = peer_u.astype(BF16)
    vt_all = peer_v.transpose(0, 2, 1).astype(BF16)

    h = _norm(xs, norm1_g[0], _mod_rows(mods[0], 0, 1, None))
    for l in range(DEPTH):
        m6 = mods[l]
        if l % 2 == 0:
            e = l // 2
            ymix = _mixer_ab(h, ab_w_in[e], ab_gate_b[e], ab_mh_g[e], ab_sink[e], rope_tabs, win_bias)
            w_out = ab_w_out[e]
        else:
            o = l // 2
            ymix = _mixer_c(h, na_w_in[o], na_rpb[o])
            w_out = na_w_out[o]
        gv = jnp.concatenate([m6[:, 2], jnp.zeros((6, D), F32)], axis=0)
        xs = _matmul(ymix, w_out.astype(BF16), resid=xs, gates=gv)
        h2t = _norm(xs, norm2_g[l], _mod_rows(m6, 3, 4, None), h_t=True)
        yt = _peer(h2t, peer_wq[l], peer_keys[l], u_all, vt_all, l)
        if l + 1 < DEPTH:
            mv = _mod_rows(mods[l + 1], 0, 1, None).at[4].set(m6[0, 5]).at[5].set(m6[1, 5])
            xs, h = _norm(xs, norm1_g[l + 1], mv, y=yt, y_t=True)
        else:
            mv = _mod_rows(m6, None, None, 5)
            _, out = _norm(xs, final_g, mv, y=yt, y_t=True, out_dtype=F32, row_off=N_CTX // 256)
    return out[None]
```
